```python
import math
import jax
import jax.numpy as jnp
from jax import lax
import numpy as np

D_MODEL = 1024
BATCH = 2
SEQ = 8192
DEPTH = 4
DEC_BATCH = 128
DEC_SEQ = 8
PAST_LEN = 2048
PAGE_SIZE = 128

H_A = 8
HD_A = 64
A_WIDTH = H_A * HD_A
DIL_PATTERNS = ((128, 1), (512, 4), (2048, 16))
WIN_MAX = 2048
H_B = 4
DK_B = 64
DV_B = 128
RET_QK_WIDTH = H_B * DK_B
RET_V_WIDTH = H_B * DV_B
RET_CHUNK = 128
MIX_WIDTH = A_WIDTH + RET_V_WIDTH
PROJ_COLS = 3 * A_WIDTH + 2 * RET_QK_WIDTH + 2 * RET_V_WIDTH
CONV_W = 31
D_FF = 2816
T5_BUCKETS = 32
T5_MAX_DIST = 2048
N_EVEN = (DEPTH + 1) // 2
N_ODD = DEPTH // 2
RMS_EPS = 1e-6
LN_EPS = 1e-5

kernel_name = 'hybrid_dilated_retention_conformer_step'


def rmsnorm(x, g):
    xf = x.astype(jnp.float32)
    y = xf * lax.rsqrt(jnp.mean(xf * xf, axis=-1, keepdims=True) + RMS_EPS)
    return (y * g.astype(jnp.float32)).astype(x.dtype)


def swiglu(h, w_gate, w_up, w_down):
    return (jax.nn.silu(h @ w_gate) * (h @ w_up)) @ w_down


def t5_bucket(dist):
    dist = np.asarray(dist).astype(np.int32)
    max_exact = T5_BUCKETS // 2
    log_ratio = np.log(np.maximum(dist, 1) / max_exact) / math.log(T5_MAX_DIST / max_exact)
    large = np.minimum(max_exact + (log_ratio * (T5_BUCKETS - max_exact)).astype(np.int32), T5_BUCKETS - 1)
    return np.where(dist < max_exact, dist, large).astype(np.int32)


def split_projection(h, w_in):
    B, L, _ = h.shape
    offs = (A_WIDTH, 2 * A_WIDTH, 3 * A_WIDTH,
            3 * A_WIDTH + RET_QK_WIDTH,
            3 * A_WIDTH + 2 * RET_QK_WIDTH,
            3 * A_WIDTH + 2 * RET_QK_WIDTH + RET_V_WIDTH)
    qa, ka, va, qb, kb, vb, gb = jnp.split(h @ w_in, offs, axis=-1)
    heads = lambda t, n: t.reshape(B, L, n, -1)
    return (heads(qa, H_A), heads(ka, H_A), heads(va, H_A),
            heads(qb, H_B), heads(kb, H_B), heads(vb, H_B), gb)


def dilated_prompt(q, k, v, t5_table, win, dil):
    B, L, H, Dh = q.shape
    n = win // dil
    span = dil * n
    l_pad = -(-L // span) * span
    m_len = l_pad // dil
    nb = m_len // n

    def to_blocks(t):
        t = jnp.pad(t.astype(jnp.float32), ((0, 0), (0, l_pad - L), (0, 0), (0, 0)))
        t = t.reshape(B, m_len, dil, H, Dh).transpose(0, 2, 1, 3, 4)
        return t.reshape(B, dil, nb, n, H, Dh)

    def to_band(t):
        prev = jnp.pad(t, ((0, 0), (0, 0), (1, 0), (0, 0), (0, 0), (0, 0)))[:, :, :-1]
        return jnp.concatenate([prev, t], axis=3)

    def from_blocks(t):
        t = t.reshape(B, dil, m_len, *t.shape[4:])
        t = jnp.moveaxis(t, 1, 2)
        return t.reshape(B, l_pad, *t.shape[3:])[:, :L]

    qb = to_blocks(q)
    kb = to_band(to_blocks(k))
    vb = to_band(to_blocks(v))
    steps = n + np.arange(n)[:, None] - np.arange(2 * n)[None, :]
    in_window = (steps >= 0) & (steps <= n)
    after_start = (np.arange(nb)[:, None] > 0) | (np.arange(2 * n)[None, :] >= n)
    mask = in_window[None, None] & after_start[:, None, None, :]
    bias = jnp.transpose(t5_table[t5_bucket(np.clip(steps, 0, n) * dil)], (2, 0, 1)).astype(jnp.float32)
    s = jnp.einsum('brjqhd,brjchd->brjhqc', qb, kb) * (Dh ** -0.5) + bias
    s = jnp.where(mask, s, -jnp.inf)
    m = jnp.max(s, axis=-1)
    p = jnp.exp(s - m[..., None])
    den = jnp.sum(p, axis=-1)
    den_t = jnp.swapaxes(den, -1, -2)
    o = jnp.einsum('brjhqc,brjchd->brjqhd', p, vb) / den_t[..., None]
    return from_blocks(o), from_blocks(jnp.swapaxes(m, -1, -2)), from_blocks(den_t)


def dilated_sample(q, k_all, v_all, t5_table, win, dil):
    B, Q, H, Dh = q.shape
    W = k_all.shape[1] - Q
    n = win // dil
    steps = np.arange(n + 1)
    idx = W + np.arange(Q)[:, None] - dil * steps[None, :]
    valid = idx >= 0
    idx = np.maximum(idx, 0)
    kg = k_all[:, idx].astype(jnp.float32)
    vg = v_all[:, idx].astype(jnp.float32)
    bias = jnp.transpose(t5_table[t5_bucket(steps * dil)], (1, 0)).astype(jnp.float32)
    s = jnp.einsum('bqhd,bqchd->bhqc', q.astype(jnp.float32), kg) * (Dh ** -0.5) + bias[:, None, :]
    s = jnp.where(valid, s, -jnp.inf)
    m = jnp.max(s, axis=-1)
    p = jnp.exp(s - m[..., None])
    den = jnp.swapaxes(jnp.sum(p, axis=-1), 1, 2)
    o = jnp.einsum('bhqc,bqchd->bqhd', p, vg) / den[..., None]
    return o, jnp.swapaxes(m, 1, 2), den


def combine_groups(outs):
    o = jnp.stack([g[0] for g in outs])
    m = jnp.stack([g[1] for g in outs])
    den = jnp.stack([g[2] for g in outs])
    w = den * jnp.exp(m - jnp.max(m, axis=0, keepdims=True))
    return jnp.sum(w[..., None] * o, axis=0) / jnp.sum(w, axis=0)[..., None]


def rotate_pairs(x, pos):
    half = x.shape[-1] // 2
    ang = 1.0 / (10000.0 ** jnp.linspace(0.0, 1.0, half, dtype=jnp.float32))
    th = pos.astype(jnp.float32)[:, None] * ang[None, :]
    cos, sin = jnp.cos(th)[None, :, None, :], jnp.sin(th)[None, :, None, :]
    xp = x.reshape(*x.shape[:-1], half, 2)
    xe, xo = xp[..., 0], xp[..., 1]
    return jnp.stack([xe * cos - xo * sin, xo * cos + xe * sin], axis=-1).reshape(x.shape)


def retention_qk(q, k, pos):
    q = rotate_pairs(q.astype(jnp.float32), pos)
    k = rotate_pairs(k.astype(jnp.float32), pos) * (DK_B ** -0.5)
    return q, k


def retention_chunk(q, k, v, s0):
    L = q.shape[1]
    lg = jnp.log(1.0 - 2.0 ** (-5.0 - jnp.arange(H_B, dtype=jnp.float32)))
    i = jnp.arange(L, dtype=jnp.float32)
    diff = i[:, None] - i[None, :]
    dmask = jnp.where(diff >= 0, jnp.exp(jnp.maximum(diff, 0.0)[None] * lg[:, None, None]), 0.0)
    v = v.astype(jnp.float32)
    qk = jnp.einsum('blhd,bmhd->bhlm', q, k) * dmask
    o = (jnp.einsum('bhlm,bmhe->blhe', qk, v)
         + jnp.einsum('blhd,bhde->blhe', q, s0) * jnp.exp((i[:, None] + 1.0) * lg[None, :])[None, :, :, None])
    k_dec = k * jnp.exp((L - 1.0 - i)[:, None] * lg[None, :])[None, :, :, None]
    s1 = s0 * jnp.exp(L * lg)[None, :, None, None] + jnp.einsum('blhd,blhe->bhde', k_dec, v)
    return o, s1


def retention_prompt(q, k, v):
    B, L, H, _ = q.shape
    nc = L // RET_CHUNK
    to_chunks = lambda t: jnp.swapaxes(t.reshape(B, nc, RET_CHUNK, *t.shape[2:]), 0, 1)

    def step(s, qkv):
        o, s = retention_chunk(qkv[0], qkv[1], qkv[2], s)
        return s, o

    s0 = jnp.zeros((B, H_B, DK_B, DV_B), jnp.float32)
    s_end, o = lax.scan(step, s0, (to_chunks(q), to_chunks(k), to_chunks(v)))
    return jnp.swapaxes(o, 0, 1).reshape(B, L, H_B, DV_B), s_end


def merge_heads(oa, ob, gb, w_out, dtype):
    B, L = oa.shape[:2]
    obn = ob * lax.rsqrt(jnp.mean(ob * ob, axis=-1, keepdims=True) + RMS_EPS)
    ob_g = jax.nn.silu(gb.astype(jnp.float32)) * obn.reshape(B, L, RET_V_WIDTH)
    cat = jnp.concatenate([oa.reshape(B, L, A_WIDTH), ob_g], axis=-1).astype(dtype)
    return cat @ w_out


def even_prompt(h, w_in, w_out, t5_table):
    qa, ka, va, qb, kb, vb, gb = split_projection(h, w_in)
    oa = combine_groups([dilated_prompt(qa, ka, va, t5_table, w, d) for w, d in DIL_PATTERNS])
    qr, kr = retention_qk(qb, kb, jnp.arange(h.shape[1]))
    ob, s_end = retention_prompt(qr, kr, vb)
    y = merge_heads(oa, ob, gb, w_out, h.dtype)
    keep = min(WIN_MAX, h.shape[1])
    return y, ka[:, -keep:], va[:, -keep:], s_end


def even_sample(h, k_buf, v_buf, s_past, w_in, w_out, t5_table):
    qa, ka, va, qb, kb, vb, gb = split_projection(h, w_in)
    k_all = jnp.concatenate([k_buf.astype(ka.dtype), ka], axis=1)
    v_all = jnp.concatenate([v_buf.astype(va.dtype), va], axis=1)
    oa = combine_groups([dilated_sample(qa, k_all, v_all, t5_table, w, d) for w, d in DIL_PATTERNS])
    qr, kr = retention_qk(qb, kb, PAST_LEN + jnp.arange(h.shape[1]))
    ob, s_new = retention_chunk(qr, kr, vb, s_past.astype(jnp.float32))
    y = merge_heads(oa, ob, gb, w_out, h.dtype)
    return y, ka, va, s_new


def conv_module(h, buf, w_pw1, b_pw1, w_dw, b_dw, ln_g, ln_b, w_pw2, b_pw2):
    a, gate = jnp.split(h @ w_pw1 + b_pw1, 2, axis=-1)
    glu = a * jax.nn.sigmoid(gate)
    ext = jnp.concatenate([buf.astype(glu.dtype), glu], axis=1)
    c = lax.conv_general_dilated(ext, w_dw[:, None, :].astype(glu.dtype), (1,), 'VALID',
                                 dimension_numbers=('NWC', 'WIO', 'NWC'),
                                 feature_group_count=D_MODEL) + b_dw
    cf = c.astype(jnp.float32)
    mu = jnp.mean(cf, axis=-1, keepdims=True)
    var = jnp.mean(jnp.square(cf - mu), axis=-1, keepdims=True)
    nrm = (cf - mu) * lax.rsqrt(var + LN_EPS) * ln_g.astype(jnp.float32) + ln_b.astype(jnp.float32)
    y = jax.nn.silu(nrm).astype(h.dtype) @ w_pw2 + b_pw2
    return y, ext[:, -(CONV_W - 1):]


def setup_inputs(seed: int = 0) -> dict:
    key = jax.random.key(seed)
    ks = jax.random.split(key, 24)
    f32 = jnp.float32
    nrm = lambda k, shape, scale: jax.random.normal(k, shape, f32) * scale
    w_buf = min(WIN_MAX, PAST_LEN)
    return {
        'x_prompt': nrm(ks[0], (BATCH, SEQ, D_MODEL), 1.0),
        'x_sample': nrm(ks[1], (DEC_BATCH, DEC_SEQ, D_MODEL), 1.0),
        'cache_win_k': nrm(ks[2], (N_EVEN, DEC_BATCH, w_buf, H_A, HD_A), 1.0),
        'cache_win_v': nrm(ks[3], (N_EVEN, DEC_BATCH, w_buf, H_A, HD_A), 1.0),
        'state_ret': nrm(ks[4], (N_EVEN, DEC_BATCH, H_B, DK_B, DV_B), 0.5),
        'state_conv': nrm(ks[5], (N_ODD, DEC_BATCH, CONV_W - 1, D_MODEL), 0.5),
        't5_table': nrm(ks[6], (T5_BUCKETS, H_A), 0.2),
        'rms_g': 1.0 + nrm(ks[7], (DEPTH, 3, D_MODEL), 0.02),
        'final_g': 1.0 + nrm(ks[8], (D_MODEL,), 0.02),
        'w_ffn_gate': nrm(ks[9], (DEPTH, 2, D_MODEL, D_FF), D_MODEL ** -0.5),
        'w_ffn_up': nrm(ks[10], (DEPTH, 2, D_MODEL, D_FF), D_MODEL ** -0.5),
        'w_ffn_down': nrm(ks[11], (DEPTH, 2, D_FF, D_MODEL), D_FF ** -0.5),
        'w_mix_in': nrm(ks[12], (N_EVEN, D_MODEL, PROJ_COLS), D_MODEL ** -0.5),
        'w_mix_out': nrm(ks[13], (N_EVEN, MIX_WIDTH, D_MODEL), MIX_WIDTH ** -0.5),
        'w_pw1': nrm(ks[14], (N_ODD, D_MODEL, 2 * D_MODEL), D_MODEL ** -0.5),
        'b_pw1': nrm(ks[15], (N_ODD, 2 * D_MODEL), 0.02),
        'w_dw': nrm(ks[16], (N_ODD, CONV_W, D_MODEL), CONV_W ** -0.5),
        'b_dw': nrm(ks[17], (N_ODD, D_MODEL), 0.02),
        'ln_g': 1.0 + nrm(ks[18], (N_ODD, D_MODEL), 0.02),
        'ln_b': nrm(ks[19], (N_ODD, D_MODEL), 0.02),
        'w_pw2': nrm(ks[20], (N_ODD, D_MODEL, D_MODEL), D_MODEL ** -0.5),
        'b_pw2': nrm(ks[21], (N_ODD, D_MODEL), 0.02),
    }


def reference(x_prompt, x_sample, cache_win_k, cache_win_v, state_ret, state_conv, t5_table,
              rms_g, final_g, w_ffn_gate, w_ffn_up, w_ffn_down, w_mix_in, w_mix_out,
              w_pw1, b_pw1, w_dw, b_dw, ln_g, ln_b, w_pw2, b_pw2):
    xp, xs = x_prompt, x_sample
    kp_l, vp_l, sp_l, cp_l = [], [], [], []
    ks_l, vs_l, ss_l, cs_l = [], [], [], []
    for layer in range(DEPTH):
        g = rms_g[layer]
        ffn_a = (w_ffn_gate[layer, 0], w_ffn_up[layer, 0], w_ffn_down[layer, 0])
        ffn_b = (w_ffn_gate[layer, 1], w_ffn_up[layer, 1], w_ffn_down[layer, 1])
        xp = xp + 0.5 * swiglu(rmsnorm(xp, g[0]), *ffn_a)
        xs = xs + 0.5 * swiglu(rmsnorm(xs, g[0]), *ffn_a)
        hp, hs = rmsnorm(xp, g[1]), rmsnorm(xs, g[1])
        if layer % 2 == 0:
            e = layer // 2
            yp, kp, vp, sp = even_prompt(hp, w_mix_in[e], w_mix_out[e], t5_table)
            ys, ksn, vsn, ssn = even_sample(hs, cache_win_k[e], cache_win_v[e], state_ret[e],
                                            w_mix_in[e], w_mix_out[e], t5_table)
            kp_l.append(kp); vp_l.append(vp); sp_l.append(sp)
            ks_l.append(ksn); vs_l.append(vsn); ss_l.append(ssn)
        else:
            o = layer // 2
            conv_params = (w_pw1[o], b_pw1[o], w_dw[o], b_dw[o], ln_g[o], ln_b[o], w_pw2[o], b_pw2[o])
            zero_buf = jnp.zeros((hp.shape[0], CONV_W - 1, D_MODEL), hp.dtype)
            yp, cp = conv_module(hp, zero_buf, *conv_params)
            ys, csn = conv_module(hs, state_conv[o], *conv_params)
            cp_l.append(cp); cs_l.append(csn)
        xp = xp + yp
        xs = xs + ys
        xp = xp + 0.5 * swiglu(rmsnorm(xp, g[2]), *ffn_b)
        xs = xs + 0.5 * swiglu(rmsnorm(xs, g[2]), *ffn_b)
    y_prompt = rmsnorm(xp, final_g)
    y_sample = rmsnorm(xs, final_g)
    dt = x_prompt.dtype
    stk = lambda ts: jnp.stack(ts).astype(dt)
    return (y_prompt, y_sample, stk(kp_l), stk(vp_l), stk(sp_l), stk(cp_l),
            stk(ks_l), stk(vs_l), stk(ss_l), stk(cs_l))
```

```python
import functools
import math

import jax
import jax.numpy as jnp
import numpy as np
from jax import lax
from jax.experimental import pallas as pl
from jax.experimental.pallas import tpu as pltpu

D_MODEL = 1024
D_FF = 2816
H_A, HD_A = 8, 64
A_WIDTH = H_A * HD_A
DIL_PATTERNS = ((128, 1), (512, 4), (2048, 16))
H_B, DK_B, DV_B = 4, 64, 128
RET_QK_WIDTH = H_B * DK_B
RET_V_WIDTH = H_B * DV_B
RET_CHUNK = 128
PROJ_COLS = 3 * A_WIDTH + 2 * RET_QK_WIDTH + 2 * RET_V_WIDTH
CONV_W = 31
T5_BUCKETS = 32
T5_MAX_DIST = 2048
PAST_LEN = 2048
RMS_EPS = 1e-6
LN_EPS = 1e-5

LANES = 128
SUBLANES = 8
VMEM_LIMIT_BYTES = 56 * 1024 * 1024

NEG = -1e30
BAND_Q = 128
ATT_SPAN = 2048
ROW_TILE = 512


def _cparams(sem):
    return pltpu.CompilerParams(dimension_semantics=sem, vmem_limit_bytes=VMEM_LIMIT_BYTES)


def _const_spec(shape):
    nd = len(shape)
    return pl.BlockSpec(shape, lambda *_: (0,) * nd, pipeline_mode=pl.Buffered(1))


def _rms(x, g):
    y = x * lax.rsqrt(jnp.mean(x * x, axis=-1, keepdims=True) + RMS_EPS)
    return y * g


def _silu(x):
    return x / (1.0 + jnp.exp(-x))


def _bdot(a, b):
    return jnp.dot(a, b, preferred_element_type=jnp.float32)


def _bdot_nt(a, b):
    return lax.dot_general(a, b, (((1,), (1,)), ((), ())), preferred_element_type=jnp.float32)


def _ffn_body(*refs, pre, post, ff_chunk):
    it = iter(refs)
    x_ref = next(it)
    if pre:
        oa_ref, ob_ref, wo_ref = next(it), next(it), next(it)
    g_ref, wg_ref, wu_ref, wd_ref = next(it), next(it), next(it), next(it)
    if post:
        fg_ref = next(it)
    o_ref = next(it)
    h_scr = next(it)

    x = x_ref[...]
    if pre:
        x = x + _bdot(oa_ref[...], wo_ref[0:A_WIDTH, :]) + _bdot(ob_ref[...], wo_ref[A_WIDTH:, :])
    xn = _rms(x, g_ref[...]).astype(jnp.bfloat16)
    for c in range(D_FF // ff_chunk):
        sl = slice(c * ff_chunk, (c + 1) * ff_chunk)
        gate = _bdot(xn, wg_ref[:, sl])
        up = _bdot(xn, wu_ref[:, sl])
        h_scr[:, sl] = (_silu(gate) * up).astype(jnp.bfloat16)
    y = x + 0.5 * _bdot(h_scr[...], wd_ref[...])
    if post:
        y = _rms(y, fg_ref[...])
    o_ref[...] = y


def _ffn(x, g, wg, wu, wd, pre=None, post_g=None):
    n = x.shape[0]
    tm = ROW_TILE
    row = lambda w: pl.BlockSpec((tm, w), lambda i: (i, 0))
    args, specs = [x], [row(D_MODEL)]
    if pre is not None:
        oa, ob, wo = pre
        args += [oa, ob, wo]
        specs += [row(A_WIDTH), row(RET_V_WIDTH), _const_spec(wo.shape)]
    args += [g.reshape(1, D_MODEL), wg, wu, wd]
    specs += [_const_spec((1, D_MODEL)), _const_spec(wg.shape), _const_spec(wu.shape), _const_spec(wd.shape)]
    if post_g is not None:
        args.append(post_g.reshape(1, D_MODEL))
        specs.append(_const_spec((1, D_MODEL)))
    body = functools.partial(_ffn_body, pre=pre is not None, post=post_g is not None, ff_chunk=D_FF // 2)
    return pl.pallas_call(
        body,
        grid=(n // tm,),
        in_specs=specs,
        out_specs=row(D_MODEL),
        out_shape=jax.ShapeDtypeStruct((n, D_MODEL), jnp.float32),
        scratch_shapes=[pltpu.VMEM((tm, D_FF), jnp.bfloat16)],
        compiler_params=_cparams(("parallel",)),
        name="ffn",
    )(*args)


def _rotate(x, cos, sin_signed):
    w = x.shape[-1]
    lane = lax.broadcasted_iota(jnp.int32, x.shape, 1)
    partner = jnp.where((lane & 1) == 0, pltpu.roll(x, w - 1, 1), pltpu.roll(x, 1, 1))
    return x * cos + partner * sin_signed


def _proj_body(x_ref, g_ref, w_ref, cos_ref, sin_ref,
               qa_ref, ka_ref, va_ref, qr_ref, kr_ref, vb_ref, gb_ref):
    xn = _rms(x_ref[...], g_ref[...]).astype(jnp.bfloat16)
    a, qk, v = A_WIDTH, RET_QK_WIDTH, RET_V_WIDTH
    seg = lambda lo, width: _bdot(xn, w_ref[:, lo:lo + width])
    qa_ref[...] = seg(0, a) * (HD_A ** -0.5)
    ka_ref[...] = seg(a, a)
    va_ref[...] = seg(2 * a, a)
    cos, sin = cos_ref[...], sin_ref[...]
    qr_ref[...] = _rotate(seg(3 * a, qk), cos, sin).astype(qr_ref.dtype)
    kr_ref[...] = (_rotate(seg(3 * a + qk, qk), cos, sin) * (DK_B ** -0.5)).astype(kr_ref.dtype)
    vb_ref[...] = seg(3 * a + 2 * qk, v).astype(vb_ref.dtype)
    gb_ref[...] = seg(3 * a + 2 * qk + v, v)


def _proj(x, g, w_in, cos_tab, sin_tab, ret_dtype):
    n = x.shape[0]
    tm = ROW_TILE
    tab_blocks = cos_tab.shape[0] // tm
    row = lambda w: pl.BlockSpec((tm, w), lambda i: (i, 0))
    tab = pl.BlockSpec((tm, RET_QK_WIDTH), lambda i: (i % tab_blocks, 0))
    f32 = jnp.float32
    outs = [(A_WIDTH, f32), (A_WIDTH, f32), (A_WIDTH, f32), (RET_QK_WIDTH, ret_dtype),
            (RET_QK_WIDTH, ret_dtype), (RET_V_WIDTH, ret_dtype), (RET_V_WIDTH, f32)]
    return pl.pallas_call(
        _proj_body,
        grid=(n // tm,),
        in_specs=[row(D_MODEL), _const_spec((1, D_MODEL)), _const_spec(w_in.shape), tab, tab],
        out_specs=[row(w) for w, _ in outs],
        out_shape=[jax.ShapeDtypeStruct((n, w), dt) for w, dt in outs],
        compiler_params=_cparams(("parallel",)),
        name="mix_proj",
    )(x, g.reshape(1, D_MODEL), w_in, cos_tab, sin_tab)


def _attn_prompt_body(q_ref, kp_ref, kc_ref, vp_ref, vc_ref, bias_ref, o_ref,
                      kk, vv, m_a, m_b, l_a, l_b, acc):
    first = (pl.program_id(2) == 0).astype(jnp.int32)
    span_rows = ATT_SPAN
    kk[0:span_rows, :] = kp_ref[...]
    kk[span_rows:, :] = kc_ref[...]
    vv[0:span_rows, :] = vp_ref[...]
    vv[span_rows:, :] = vc_ref[...]
    m_a[...] = jnp.full(m_a.shape, NEG, jnp.float32)
    m_b[...] = jnp.full(m_b.shape, NEG, jnp.float32)
    l_a[...] = jnp.zeros(l_a.shape, jnp.float32)
    l_b[...] = jnp.zeros(l_b.shape, jnp.float32)
    acc[...] = jnp.zeros(acc.shape, jnp.float32)

    nq = BAND_Q
    low = lax.broadcasted_iota(jnp.int32, (nq, LANES), 1) < HD_A

    for p, (_, dil) in enumerate(DIL_PATTERNS):
        span = nq * dil
        shift = dil.bit_length() - 1

        def unit(u, carry, p=p, dil=dil, span=span, shift=shift):
            blk = u >> shift
            base = blk * span + (u & (dil - 1))
            rows = pl.ds(base, nq, stride=dil)
            band = pl.ds(span_rows + base - span, 2 * nq, stride=dil)
            q = q_ref[rows, :]
            kb = kk[band, :].astype(jnp.bfloat16)
            vb = vv[band, :].astype(jnp.bfloat16)
            zero = jnp.zeros_like(q)
            lhs = jnp.concatenate([jnp.where(low, q, zero), jnp.where(low, zero, q)], axis=0)
            flag = jnp.where(blk == 0, first, 0)
            s = _bdot_nt(lhs.astype(jnp.bfloat16), kb) + bias_ref[p, flag]
            m_prev = jnp.concatenate([m_a[rows, :], m_b[rows, :]], axis=0)
            l_prev = jnp.concatenate([l_a[rows, :], l_b[rows, :]], axis=0)
            m_next = jnp.maximum(m_prev, jnp.max(s, axis=1, keepdims=True))
            alpha = jnp.exp(m_prev - m_next)
            prob = jnp.exp(s - jnp.concatenate([m_next, m_next], axis=1))
            l_next = alpha * l_prev + jnp.sum(prob, axis=1, keepdims=True)
            pv = _bdot(prob.astype(jnp.bfloat16), vb)
            acc[rows, :] = (jnp.where(low, alpha[:nq], alpha[nq:]) * acc[rows, :]
                            + jnp.where(low, pv[:nq], pv[nq:]))
            m_a[rows, :] = m_next[:nq]
            m_b[rows, :] = m_next[nq:]
            l_a[rows, :] = l_next[:nq]
            l_b[rows, :] = l_next[nq:]
            return carry

        lax.fori_loop(0, span_rows // nq, unit, 0)

    low_all = lax.broadcasted_iota(jnp.int32, acc.shape, 1) < HD_A
    o_ref[...] = (acc[...] / jnp.where(low_all, l_a[...], l_b[...])).astype(o_ref.dtype)


def _prompt_bias(t5_table):
    n = BAND_Q
    steps = n + np.arange(n)[:, None] - np.arange(2 * n)[None, :]
    valid = (steps >= 0) & (steps <= n)
    per_pattern = []
    for _, dil in DIL_PATTERNS:
        bucket = _t5_bucket(np.clip(steps, 0, n) * dil)
        bias = jnp.transpose(t5_table[bucket], (2, 0, 1)).astype(jnp.float32)
        full = jnp.where(valid[None], bias, NEG)
        start = jnp.where((valid & (np.arange(2 * n)[None, :] >= n))[None], bias, NEG)
        per_pattern.append(jnp.stack([full, start], axis=0))
    b = jnp.stack(per_pattern, axis=0)
    b = b.reshape(len(DIL_PATTERNS), 2, H_A // 2, 2 * n, 2 * n)
    return jnp.transpose(b, (2, 0, 1, 3, 4))


def _attn_prompt(qa, ka, va, bias, batch, seq):
    nblk = seq // ATT_SPAN
    n = qa.shape[0]
    cur = pl.BlockSpec((ATT_SPAN, LANES), lambda hp, b, j: (b * nblk + j, hp))
    prev = pl.BlockSpec((ATT_SPAN, LANES), lambda hp, b, j: (b * nblk + jnp.maximum(j - 1, 0), hp))
    bias_spec = pl.BlockSpec((None,) + bias.shape[1:], lambda hp, b, j: (hp, 0, 0, 0, 0))
    f32 = jnp.float32
    return pl.pallas_call(
        _attn_prompt_body,
        grid=(H_A // 2, batch, nblk),
        in_specs=[cur, prev, cur, prev, cur, bias_spec],
        out_specs=cur,
        out_shape=jax.ShapeDtypeStruct((n, A_WIDTH), jnp.bfloat16),
        scratch_shapes=[pltpu.VMEM((2 * ATT_SPAN, LANES), f32), pltpu.VMEM((2 * ATT_SPAN, LANES), f32)]
        + [pltpu.VMEM((ATT_SPAN, LANES), f32)] * 5,
        compiler_params=_cparams(("parallel", "parallel", "parallel")),
        name="attn_prompt",
    )(qa, ka, ka, va, va, bias)


def _ret_tables(length):
    lg = jnp.log(1.0 - 2.0 ** (-5.0 - jnp.arange(H_B, dtype=jnp.float32)))
    i = jnp.arange(length, dtype=jnp.float32)
    diff = i[:, None] - i[None, :]
    dmask = jnp.where(diff >= 0, jnp.exp(jnp.maximum(diff, 0.0)[None] * lg[:, None, None]), 0.0)
    dq = jnp.exp((i[:, None] + 1.0) * lg[None, :])
    dq = jnp.broadcast_to(dq.T[:, :, None], (H_B, length, DV_B))
    kd = jnp.exp((length - 1.0 - i)[:, None] * lg[None, :])
    kdec = jnp.repeat(kd, DK_B, axis=1).reshape(length, H_B // 2, 2 * DK_B).transpose(1, 0, 2)
    sd = jnp.exp(length * lg)
    sdec = jnp.broadcast_to(jnp.repeat(sd, DK_B).reshape(H_B // 2, 2 * DK_B, 1), (H_B // 2, 2 * DK_B, DV_B))
    return dmask, dq, kdec, sdec


def _norm_gate(o, gate):
    on = o * lax.rsqrt(jnp.mean(o * o, axis=-1, keepdims=True) + RMS_EPS)
    return _silu(gate) * on


def _ret_prompt_body(q_ref, k_ref, v_ref, g_ref, dmask_ref, dq_ref, kdec_ref, sdec_ref,
                     o_ref, s_out_ref, state):
    step = pl.program_id(1)

    @pl.when(step == 0)
    def _():
        state[...] = jnp.zeros(state.shape, jnp.float32)

    c_len = RET_CHUNK
    low = lax.broadcasted_iota(jnp.int32, (c_len, LANES), 1) < DK_B
    top = lax.broadcasted_iota(jnp.int32, (2 * DK_B, DV_B), 0) < DK_B

    def chunk(c, carry):
        rows = pl.ds(pl.multiple_of(c * c_len, c_len), c_len)
        for pair in range(H_B // 2):
            lanes = slice(pair * LANES, (pair + 1) * LANES)
            q2 = q_ref[rows, lanes]
            k2 = k_ref[rows, lanes]
            s_old = state[pair]
            s_b16 = s_old.astype(jnp.bfloat16)
            kt = (k2.astype(jnp.float32) * kdec_ref[pair]).T.astype(jnp.bfloat16)
            zero = jnp.zeros_like(q2)
            upd = []
            for hh in range(2):
                h = 2 * pair + hh
                vl = slice(h * DV_B, (h + 1) * DV_B)
                qm = jnp.where(low, q2, zero) if hh == 0 else jnp.where(low, zero, q2)
                vh = v_ref[rows, vl]
                a = _bdot_nt(qm, k2) * dmask_ref[h]
                o = _bdot(a.astype(jnp.bfloat16), vh) + _bdot(qm, s_b16) * dq_ref[h]
                o_ref[rows, vl] = _norm_gate(o, g_ref[rows, vl]).astype(o_ref.dtype)
                upd.append(_bdot(kt, vh))
            state[pair] = s_old * sdec_ref[pair] + jnp.where(top, upd[0], upd[1])
        return carry

    lax.fori_loop(0, q_ref.shape[0] // c_len, chunk, 0)

    @pl.when(step == pl.num_programs(1) - 1)
    def _():
        for h in range(H_B):
            s_out_ref[h] = state[h // 2, (h % 2) * DK_B:(h % 2 + 1) * DK_B, :]


def _ret_prompt(qr, kr, vb, gb, batch, seq):
    rows = 1024
    nblk = seq // rows
    n = qr.shape[0]
    dmask, dq, kdec, sdec = _ret_tables(RET_CHUNK)
    row = lambda w: pl.BlockSpec((rows, w), lambda b, j: (b * nblk + j, 0))
    return pl.pallas_call(
        _ret_prompt_body,
        grid=(batch, nblk),
        in_specs=[row(RET_QK_WIDTH), row(RET_QK_WIDTH), row(RET_V_WIDTH), row(RET_V_WIDTH),
                  _const_spec(dmask.shape), _const_spec(dq.shape), _const_spec(kdec.shape),
                  _const_spec(sdec.shape)],
        out_specs=[row(RET_V_WIDTH),
                   pl.BlockSpec((None, H_B, DK_B, DV_B), lambda b, j: (b, 0, 0, 0))],
        out_shape=[jax.ShapeDtypeStruct((n, RET_V_WIDTH), jnp.bfloat16),
                   jax.ShapeDtypeStruct((batch, H_B, DK_B, DV_B), jnp.float32)],
        scratch_shapes=[pltpu.VMEM((H_B // 2, 2 * DK_B, DV_B), jnp.float32)],
        compiler_params=_cparams(("parallel", "arbitrary")),
        name="ret_prompt",
    )(qr, kr, vb, gb, dmask, dq, kdec, sdec)


def _sample_bias(t5_table, w_buf, q_len):
    cols = np.concatenate([np.arange(w_buf + q_len), -np.ones(LANES - q_len, np.int64)])
    dist = (w_buf + np.arange(q_len))[:, None] - cols[None, :]
    merged = None
    for win, dil in DIL_PATTERNS:
        ok = (cols[None, :] >= 0) & (dist >= 0) & (dist % dil == 0) & (dist // dil <= win // dil)
        bucket = _t5_bucket(np.clip(dist, 0, None))
        b = jnp.where(ok[None], jnp.transpose(t5_table[bucket], (2, 0, 1)).astype(jnp.float32), -jnp.inf)
        merged = b if merged is None else jnp.logaddexp(merged, b)
    merged = jnp.where(jnp.isfinite(merged), merged, NEG)
    return merged.reshape(H_A * q_len, cols.shape[0])


def _attn_sample_body(q_ref, kn_ref, vn_ref, kt_ref, vt_ref, bias_ref, o_ref):
    q_len = q_ref.shape[0]
    w_buf = kt_ref.shape[-1]
    bf16 = jnp.bfloat16
    rows = H_A * q_len
    q_rep = jnp.concatenate([q_ref[...]] * H_A, axis=0)
    row_head = jnp.right_shift(lax.broadcasted_iota(jnp.int32, (rows, A_WIDTH), 0), q_len.bit_length() - 1)
    lane_head = jnp.right_shift(lax.broadcasted_iota(jnp.int32, (rows, A_WIDTH), 1), HD_A.bit_length() - 1)
    own = row_head == lane_head
    lhs = jnp.where(own, q_rep, 0.0).astype(bf16)
    padz = jnp.zeros((LANES - q_len, A_WIDTH), jnp.float32)
    k_new = jnp.concatenate([kn_ref[...], padz], axis=0).astype(bf16)
    v_new = jnp.concatenate([vn_ref[...], padz], axis=0).astype(bf16)
    kt = kt_ref[...].reshape(A_WIDTH, w_buf).astype(bf16)
    vt = vt_ref[...].reshape(A_WIDTH, w_buf).astype(bf16)
    s_buf = _bdot(lhs, kt) + bias_ref[:, 0:w_buf]
    s_new = _bdot_nt(lhs, k_new) + bias_ref[:, w_buf:]
    m = jnp.maximum(jnp.max(s_buf, axis=1, keepdims=True), jnp.max(s_new, axis=1, keepdims=True))
    p_buf = jnp.exp(s_buf - m)
    p_new = jnp.exp(s_new - m)
    den = jnp.sum(p_buf, axis=1, keepdims=True) + jnp.sum(p_new, axis=1, keepdims=True)
    o_all = (_bdot_nt(p_buf.astype(bf16), vt) + _bdot(p_new.astype(bf16), v_new)) / den
    o_all = jnp.where(own, o_all, 0.0)
    out = o_all[0:q_len]
    for h in range(1, H_A):
        out = out + o_all[h * q_len:(h + 1) * q_len]
    o_ref[...] = out.astype(o_ref.dtype)


def _attn_sample(qa, ka, va, cache_kt, cache_vt, layer, bias, q_len):
    n = qa.shape[0]
    batch = n // q_len
    w_buf = cache_kt.shape[-1]
    new = pl.BlockSpec((q_len, A_WIDTH), lambda b: (b, 0))
    buf = pl.BlockSpec((None, None, H_A, HD_A, w_buf), lambda b: (layer, b, 0, 0, 0))
    return pl.pallas_call(
        _attn_sample_body,
        grid=(batch,),
        in_specs=[new, new, new, buf, buf, _const_spec(bias.shape)],
        out_specs=new,
        out_shape=jax.ShapeDtypeStruct((n, A_WIDTH), jnp.bfloat16),
        compiler_params=_cparams(("parallel",)),
        name="attn_sample",
    )(qa, ka, va, cache_kt, cache_vt, bias)


RET_GROUP = 16


def _ret_sample_body(q_ref, k_ref, v_ref, g_ref, s_ref, dmask_ref, dq_ref, kdec_ref, sdec_ref,
                     o_ref, s_out_ref, *, q_len):
    rows = q_ref.shape[0]
    low = lax.broadcasted_iota(jnp.int32, (rows, LANES), 1) < DK_B
    top = lax.broadcasted_iota(jnp.int32, (2 * DK_B, DV_B), 0) < DK_B
    col = lax.broadcasted_iota(jnp.int32, (2 * DK_B, rows), 1)
    for pair in range(H_B // 2):
        lanes = slice(pair * LANES, (pair + 1) * LANES)
        q2 = q_ref[:, lanes]
        k2 = k_ref[:, lanes]
        kt = (k2 * kdec_ref[pair]).T
        zero = jnp.zeros_like(q2)
        upd = [[], []]
        for hh in range(2):
            h = 2 * pair + hh
            vl = slice(h * DV_B, (h + 1) * DV_B)
            qm = jnp.where(low, q2, zero) if hh == 0 else jnp.where(low, zero, q2)
            vh = v_ref[:, vl]
            a = _bdot_nt(qm, k2) * dmask_ref[h]
            cross = jnp.concatenate(
                [_bdot(qm[r * q_len:(r + 1) * q_len], s_ref[r, pair]) for r in range(RET_GROUP)], axis=0)
            o = _bdot(a, vh) + cross * dq_ref[h]
            o_ref[:, vl] = _norm_gate(o, g_ref[:, vl]).astype(o_ref.dtype)
            for r in range(RET_GROUP):
                upd[hh].append(_bdot(jnp.where((col >= r * q_len) & (col < (r + 1) * q_len), kt, 0.0), vh))
        for r in range(RET_GROUP):
            s_out_ref[r, pair] = s_ref[r, pair] * sdec_ref[pair] + jnp.where(top, upd[0][r], upd[1][r])


def _ret_sample(qr, kr, vb, gb, state, layer, q_len):
    n = qr.shape[0]
    batch = n // q_len
    rows = RET_GROUP * q_len
    dmask1, dq1, kdec1, sdec = _ret_tables(q_len)
    eye = jnp.eye(RET_GROUP, dtype=jnp.float32)
    dmask = jnp.einsum("rs,hij->hrisj", eye, dmask1).reshape(H_B, rows, rows)
    dq = jnp.tile(dq1, (1, RET_GROUP, 1))
    kdec = jnp.tile(kdec1, (1, RET_GROUP, 1))
    row = lambda w: pl.BlockSpec((rows, w), lambda i: (i, 0))
    st_in = pl.BlockSpec((None, RET_GROUP, H_B // 2, 2 * DK_B, DV_B), lambda i: (layer, i, 0, 0, 0))
    st_out = pl.BlockSpec((RET_GROUP, H_B // 2, 2 * DK_B, DV_B), lambda i: (i, 0, 0, 0))
    return pl.pallas_call(
        functools.partial(_ret_sample_body, q_len=q_len),
        grid=(batch // RET_GROUP,),
        in_specs=[row(RET_QK_WIDTH), row(RET_QK_WIDTH), row(RET_V_WIDTH), row(RET_V_WIDTH), st_in,
                  _const_spec(dmask.shape), _const_spec(dq.shape), _const_spec(kdec.shape),
                  _const_spec(sdec.shape)],
        out_specs=[row(RET_V_WIDTH), st_out],
        out_shape=[jax.ShapeDtypeStruct((n, RET_V_WIDTH), jnp.bfloat16),
                   jax.ShapeDtypeStruct((batch, H_B // 2, 2 * DK_B, DV_B), jnp.float32)],
        compiler_params=_cparams(("parallel",)),
        name="ret_sample",
    )(qr, kr, vb, gb, state, dmask, dq, kdec, sdec)


CONV_PAD = 32


def _glu(x, g, w1_ref, b1_ref):
    xn = _rms(x, g).astype(jnp.bfloat16)
    a = _bdot(xn, w1_ref[:, 0:D_MODEL]) + b1_ref[:, 0:D_MODEL]
    gate = _bdot(xn, w1_ref[:, D_MODEL:]) + b1_ref[:, D_MODEL:]
    return a / (1.0 + jnp.exp(-gate))


def _conv_tail(x, c, lng_ref, lnb_ref, w2_ref, b2_ref):
    mu = jnp.mean(c, axis=-1, keepdims=True)
    d = c - mu
    var = jnp.mean(d * d, axis=-1, keepdims=True)
    nrm = d * lax.rsqrt(var + LN_EPS) * lng_ref[...] + lnb_ref[...]
    return x + _bdot(_silu(nrm).astype(jnp.bfloat16), w2_ref[...]) + b2_ref[...]


def _conv_prompt_body(x_ref, g_ref, w1_ref, b1_ref, wdw_ref, bdw_ref, lng_ref, lnb_ref, w2_ref, b2_ref,
                      o_ref, buf_ref, ext, cbuf):
    tm = x_ref.shape[0]
    hist = CONV_W - 1

    @pl.when(pl.program_id(1) == 0)
    def _():
        ext[0:CONV_PAD, :] = jnp.zeros((CONV_PAD, D_MODEL), jnp.float32)

    x = x_ref[...]
    ext[CONV_PAD:, :] = _glu(x, g_ref[...], w1_ref, b1_ref)

    blk, lane_blk, off = 32, 256, CONV_PAD - hist

    def rows_block(i, carry):
        r0 = pl.multiple_of(i * blk, blk)
        for lb in range(D_MODEL // lane_blk):
            lanes = slice(lb * lane_blk, (lb + 1) * lane_blk)
            win = ext[pl.ds(r0, blk + CONV_PAD), lanes]
            acc = jnp.broadcast_to(bdw_ref[:, lanes], (blk, lane_blk))
            for r in range(SUBLANES):
                taps = [k for k in range(CONV_W) if (k + off) % SUBLANES == r]
                rows = blk if r == 0 else blk + SUBLANES
                y = None
                for k in taps:
                    a0 = k + off - r
                    term = win[a0:a0 + rows] * wdw_ref[k:k + 1, lanes]
                    y = term if y is None else y + term
                acc = acc + y[r:r + blk]
            cbuf[pl.ds(r0, blk), lanes] = acc
        return carry
    lax.fori_loop(0, tm // blk, rows_block, 0)

    o_ref[...] = _conv_tail(x, cbuf[...], lng_ref, lnb_ref, w2_ref, b2_ref)
    buf_ref[...] = ext[tm + CONV_PAD - hist:, :]
    ext[0:CONV_PAD, :] = ext[tm:, :]


def _conv_prompt(x, g, w1, b1, wdw, bdw, lng, lnb, w2, b2, batch, seq):
    tm = ROW_TILE
    nblk = seq // tm
    n = x.shape[0]
    row = pl.BlockSpec((tm, D_MODEL), lambda b, j: (b * nblk + j, 0))
    vec = lambda a: a.reshape(1, -1)
    f32 = jnp.float32
    return pl.pallas_call(
        _conv_prompt_body,
        grid=(batch, nblk),
        in_specs=[row, _const_spec((1, D_MODEL)), _const_spec(w1.shape), _const_spec((1, 2 * D_MODEL)),
                  _const_spec(wdw.shape), _const_spec((1, D_MODEL)), _const_spec((1, D_MODEL)),
                  _const_spec((1, D_MODEL)), _const_spec(w2.shape), _const_spec((1, D_MODEL))],
        out_specs=[row, pl.BlockSpec((None, CONV_W - 1, D_MODEL), lambda b, j: (b, 0, 0))],
        out_shape=[jax.ShapeDtypeStruct((n, D_MODEL), f32),
                   jax.ShapeDtypeStruct((batch, CONV_W - 1, D_MODEL), f32)],
        scratch_shapes=[pltpu.VMEM((tm + CONV_PAD, D_MODEL), f32), pltpu.VMEM((tm, D_MODEL), f32)],
        compiler_params=_cparams(("parallel", "arbitrary")),
        name="conv_prompt",
    )(x, vec(g), w1, vec(b1), wdw, vec(bdw), vec(lng), vec(lnb), w2, vec(b2))


CONV_GROUP = 16


def _conv_sample_body(x_ref, g_ref, w1_ref, b1_ref, wdw_ref, bdw_ref, lng_ref, lnb_ref, w2_ref, b2_ref,
                      st_ref, o_ref, buf_ref, ext, cbuf, *, q_len):
    hist = CONV_W - 1
    x = x_ref[...]
    glu = _glu(x, g_ref[...], w1_ref, b1_ref)
    for r in range(CONV_GROUP):
        ext[r, 0:hist, :] = st_ref[r]
        ext[r, hist:hist + q_len, :] = glu[r * q_len:(r + 1) * q_len]
    for r in range(CONV_GROUP):
        acc = jnp.broadcast_to(bdw_ref[...], (q_len, D_MODEL))
        for k in range(CONV_W):
            acc = acc + ext[r, k:k + q_len, :] * wdw_ref[k:k + 1, :]
        cbuf[r * q_len:(r + 1) * q_len, :] = acc
        buf_ref[r] = ext[r, q_len:q_len + hist, :]
    o_ref[...] = _conv_tail(x, cbuf[...], lng_ref, lnb_ref, w2_ref, b2_ref)


def _conv_sample(x, g, w1, b1, wdw, bdw, lng, lnb, w2, b2, state, layer, q_len):
    n = x.shape[0]
    batch = n // q_len
    rows = CONV_GROUP * q_len
    hist = CONV_W - 1
    row = pl.BlockSpec((rows, D_MODEL), lambda i: (i, 0))
    vec = lambda a: a.reshape(1, -1)
    f32 = jnp.float32
    return pl.pallas_call(
        functools.partial(_conv_sample_body, q_len=q_len),
        grid=(batch // CONV_GROUP,),
        in_specs=[row, _const_spec((1, D_MODEL)), _const_spec(w1.shape), _const_spec((1, 2 * D_MODEL)),
                  _const_spec(wdw.shape), _const_spec((1, D_MODEL)), _const_spec((1, D_MODEL)),
                  _const_spec((1, D_MODEL)), _const_spec(w2.shape), _const_spec((1, D_MODEL)),
                  pl.BlockSpec((None, CONV_GROUP, hist, D_MODEL), lambda i: (layer, i, 0, 0))],
        out_specs=[row, pl.BlockSpec((CONV_GROUP, hist, D_MODEL), lambda i: (i, 0, 0))],
        out_shape=[jax.ShapeDtypeStruct((n, D_MODEL), f32),
                   jax.ShapeDtypeStruct((batch, hist, D_MODEL), f32)],
        scratch_shapes=[pltpu.VMEM((CONV_GROUP, hist + q_len + 2, D_MODEL), f32),
                        pltpu.VMEM((rows, D_MODEL), f32)],
        compiler_params=_cparams(("parallel",)),
        name="conv_sample",
    )(x, vec(g), w1, vec(b1), wdw, vec(bdw), vec(lng), vec(lnb), w2, vec(b2), state)


def _t5_bucket(dist):
    dist = np.asarray(dist).astype(np.int32)
    max_exact = T5_BUCKETS // 2
    log_ratio = np.log(np.maximum(dist, 1) / max_exact) / math.log(T5_MAX_DIST / max_exact)
    large = np.minimum(max_exact + (log_ratio * (T5_BUCKETS - max_exact)).astype(np.int32), T5_BUCKETS - 1)
    return np.where(dist < max_exact, dist, large).astype(np.int32)


def _rot_tables(pos):
    half = DK_B // 2
    ang = 1.0 / (10000.0 ** jnp.linspace(0.0, 1.0, half, dtype=jnp.float32))
    th = pos.astype(jnp.float32)[:, None] * ang[None, :]
    cos = jnp.repeat(jnp.cos(th), 2, axis=1)
    sin = jnp.repeat(jnp.sin(th), 2, axis=1) * jnp.tile(jnp.array([-1.0, 1.0], jnp.float32), half)[None, :]
    return jnp.tile(cos, (1, H_B)), jnp.tile(sin, (1, H_B))


def kernel(x_prompt, x_sample, cache_win_k, cache_win_v, state_ret, state_conv, t5_table, rms_g, final_g,
           w_ffn_gate, w_ffn_up, w_ffn_down, w_mix_in, w_mix_out, w_pw1, b_pw1, w_dw, b_dw, ln_g, ln_b,
           w_pw2, b_pw2):
    batch, seq, _ = x_prompt.shape
    dec_batch, dec_seq, _ = x_sample.shape
    depth = rms_g.shape[0]
    n_even = cache_win_k.shape[0]
    w_buf = cache_win_k.shape[2]
    keep = min(DIL_PATTERNS[-1][0], seq)
    bf16 = jnp.bfloat16

    xp = x_prompt.reshape(batch * seq, D_MODEL)
    xs = x_sample.reshape(dec_batch * dec_seq, D_MODEL)

    wg, wu, wd = w_ffn_gate.astype(bf16), w_ffn_up.astype(bf16), w_ffn_down.astype(bf16)
    w_in, w_out = w_mix_in.astype(bf16), w_mix_out.astype(bf16)
    w1, w2 = w_pw1.astype(bf16), w_pw2.astype(bf16)

    cos_p, sin_p = _rot_tables(jnp.arange(seq))
    cos_s, sin_s = _rot_tables(PAST_LEN + jnp.arange(ROW_TILE) % dec_seq)
    bias_p = _prompt_bias(t5_table)
    bias_s = _sample_bias(t5_table, w_buf, dec_seq)
    assert dec_seq & (dec_seq - 1) == 0 and dec_seq <= LANES
    ck = jnp.transpose(cache_win_k, (0, 1, 3, 4, 2))
    cv = jnp.transpose(cache_win_v, (0, 1, 3, 4, 2))
    st_ret = state_ret.reshape(n_even, dec_batch, H_B // 2, 2 * DK_B, DV_B)

    kp_l, vp_l, sp_l, cp_l, ks_l, vs_l, ss_l, cs_l = [], [], [], [], [], [], [], []
    for layer in range(depth):
        g = rms_g[layer]
        xp = _ffn(xp, g[0], wg[layer, 0], wu[layer, 0], wd[layer, 0])
        xs = _ffn(xs, g[0], wg[layer, 0], wu[layer, 0], wd[layer, 0])
        last = layer == depth - 1
        post = final_g if last else None
        ffn_b = (g[2], wg[layer, 1], wu[layer, 1], wd[layer, 1])
        if layer % 2 == 0:
            e = layer // 2
            qa, ka, va, qr, kr, vb, gb = _proj(xp, g[1], w_in[e], cos_p, sin_p, bf16)
            oa = _attn_prompt(qa, ka, va, bias_p, batch, seq)
            ob, s_end = _ret_prompt(qr, kr, vb, gb, batch, seq)
            xp = _ffn(xp, *ffn_b, pre=(oa, ob, w_out[e]), post_g=post)
            kp_l.append(ka.reshape(batch, seq, H_A, HD_A)[:, -keep:])
            vp_l.append(va.reshape(batch, seq, H_A, HD_A)[:, -keep:])
            sp_l.append(s_end)

            qa, ka, va, qr, kr, vb, gb = _proj(xs, g[1], w_in[e], cos_s, sin_s, jnp.float32)
            oa = _attn_sample(qa, ka, va, ck, cv, e, bias_s, dec_seq)
            ob, s_new = _ret_sample(qr, kr, vb, gb, st_ret, e, dec_seq)
            xs = _ffn(xs, *ffn_b, pre=(oa, ob, w_out[e]), post_g=post)
            ks_l.append(ka.reshape(dec_batch, dec_seq, H_A, HD_A))
            vs_l.append(va.reshape(dec_batch, dec_seq, H_A, HD_A))
            ss_l.append(s_new.reshape(dec_batch, H_B, DK_B, DV_B))
        else:
            o = layer // 2
            conv = (g[1], w1[o], b_pw1[o], w_dw[o], b_dw[o], ln_g[o], ln_b[o], w2[o], b_pw2[o])
            xp, cp = _conv_prompt(xp, *conv, batch, seq)
            xs, cs = _conv_sample(xs, *conv, state_conv, o, dec_seq)
            cp_l.append(cp)
            cs_l.append(cs)
            xp = _ffn(xp, *ffn_b, post_g=post)
            xs = _ffn(xs, *ffn_b, post_g=post)

    stk = jnp.stack
    return (xp.reshape(batch, seq, D_MODEL), xs.reshape(dec_batch, dec_seq, D_MODEL),
            stk(kp_l), stk(vp_l), stk(sp_l), stk(cp_l), stk(ks_l), stk(vs_l), stk(ss_l), stk(cs_l))
```

```python
import functools
import math

import jax
import jax.numpy as jnp
import numpy as np
from jax import lax
from jax.experimental import pallas as pl
from jax.experimental.pallas import tpu as pltpu

D_MODEL = 1024
D_FF = 2816
H_A, HD_A = 8, 64
A_WIDTH = H_A * HD_A
DIL_PATTERNS = ((128, 1), (512, 4), (2048, 16))
H_B, DK_B, DV_B = 4, 64, 128
RET_QK_WIDTH = H_B * DK_B
RET_V_WIDTH = H_B * DV_B
RET_CHUNK = 128
PROJ_COLS = 3 * A_WIDTH + 2 * RET_QK_WIDTH + 2 * RET_V_WIDTH
CONV_W = 31
T5_BUCKETS = 32
T5_MAX_DIST = 2048
PAST_LEN = 2048
RMS_EPS = 1e-6
LN_EPS = 1e-5

LANES = 128
SUBLANES = 8
VMEM_LIMIT_BYTES = 56 * 1024 * 1024

NEG = -1e30
BAND_Q = 128
ATT_SPAN = 2048
ATT_UNROLL = 8
ROW_TILE = 512


def _cparams(sem):
    return pltpu.CompilerParams(dimension_semantics=sem, vmem_limit_bytes=VMEM_LIMIT_BYTES)


def _const_spec(shape):
    nd = len(shape)
    return pl.BlockSpec(shape, lambda *_: (0,) * nd, pipeline_mode=pl.Buffered(1))


def _rms(x, g):
    y = x * lax.rsqrt(jnp.mean(x * x, axis=-1, keepdims=True) + RMS_EPS)
    return y * g


def _silu(x):
    return x / (1.0 + jnp.exp(-x))


def _bdot(a, b):
    return jnp.dot(a, b, preferred_element_type=jnp.float32)


def _bdot_nt(a, b):
    return lax.dot_general(a, b, (((1,), (1,)), ((), ())), preferred_element_type=jnp.float32)


def _ffn_body(*refs, pre, post, ff_chunk):
    it = iter(refs)
    x_ref = next(it)
    if pre:
        oa_ref, ob_ref, wo_ref = next(it), next(it), next(it)
    g_ref, wg_ref, wu_ref, wd_ref = next(it), next(it), next(it), next(it)
    if post:
        fg_ref = next(it)
    o_ref = next(it)
    h_scr = next(it)

    x = x_ref[...]
    if pre:
        x = x + _bdot(oa_ref[...], wo_ref[0:A_WIDTH, :]) + _bdot(ob_ref[...], wo_ref[A_WIDTH:, :])
    xn = _rms(x, g_ref[...]).astype(jnp.bfloat16)
    for c in range(D_FF // ff_chunk):
        sl = slice(c * ff_chunk, (c + 1) * ff_chunk)
        gate = _bdot(xn, wg_ref[:, sl])
        up = _bdot(xn, wu_ref[:, sl])
        h_scr[:, sl] = (_silu(gate) * up).astype(jnp.bfloat16)
    y = x + 0.5 * _bdot(h_scr[...], wd_ref[...])
    if post:
        y = _rms(y, fg_ref[...])
    o_ref[...] = y


def _ffn(x, g, wg, wu, wd, pre=None, post_g=None):
    n = x.shape[0]
    tm = ROW_TILE
    row = lambda w: pl.BlockSpec((tm, w), lambda i: (i, 0))
    args, specs = [x], [row(D_MODEL)]
    if pre is not None:
        oa, ob, wo = pre
        args += [oa, ob, wo]
        specs += [row(A_WIDTH), row(RET_V_WIDTH), _const_spec(wo.shape)]
    args += [g.reshape(1, D_MODEL), wg, wu, wd]
    specs += [_const_spec((1, D_MODEL)), _const_spec(wg.shape), _const_spec(wu.shape), _const_spec(wd.shape)]
    if post_g is not None:
        args.append(post_g.reshape(1, D_MODEL))
        specs.append(_const_spec((1, D_MODEL)))
    body = functools.partial(_ffn_body, pre=pre is not None, post=post_g is not None, ff_chunk=D_FF // 2)
    return pl.pallas_call(
        body,
        grid=(n // tm,),
        in_specs=specs,
        out_specs=row(D_MODEL),
        out_shape=jax.ShapeDtypeStruct((n, D_MODEL), jnp.float32),
        scratch_shapes=[pltpu.VMEM((tm, D_FF), jnp.bfloat16)],
        compiler_params=_cparams(("parallel",)),
        name="ffn",
    )(*args)


def _rotate(x, cos, sin_signed):
    w = x.shape[-1]
    lane = lax.broadcasted_iota(jnp.int32, x.shape, 1)
    partner = jnp.where((lane & 1) == 0, pltpu.roll(x, w - 1, 1), pltpu.roll(x, 1, 1))
    return x * cos + partner * sin_signed


def _proj_body(x_ref, g_ref, w_ref, cos_ref, sin_ref,
               qa_ref, ka_ref, va_ref, qr_ref, kr_ref, vb_ref, gb_ref):
    xn = _rms(x_ref[...], g_ref[...]).astype(jnp.bfloat16)
    a, qk, v = A_WIDTH, RET_QK_WIDTH, RET_V_WIDTH
    seg = lambda lo, width: _bdot(xn, w_ref[:, lo:lo + width])
    qa_ref[...] = seg(0, a) * (HD_A ** -0.5)
    ka_ref[...] = seg(a, a)
    va_ref[...] = seg(2 * a, a)
    cos, sin = cos_ref[...], sin_ref[...]
    qr_ref[...] = _rotate(seg(3 * a, qk), cos, sin).astype(qr_ref.dtype)
    kr_ref[...] = (_rotate(seg(3 * a + qk, qk), cos, sin) * (DK_B ** -0.5)).astype(kr_ref.dtype)
    vb_ref[...] = seg(3 * a + 2 * qk, v).astype(vb_ref.dtype)
    gb_ref[...] = seg(3 * a + 2 * qk + v, v)


def _proj(x, g, w_in, cos_tab, sin_tab, ret_dtype):
    n = x.shape[0]
    tm = ROW_TILE
    tab_blocks = cos_tab.shape[0] // tm
    row = lambda w: pl.BlockSpec((tm, w), lambda i: (i, 0))
    tab = pl.BlockSpec((tm, RET_QK_WIDTH), lambda i: (i % tab_blocks, 0))
    f32 = jnp.float32
    outs = [(A_WIDTH, f32), (A_WIDTH, f32), (A_WIDTH, f32), (RET_QK_WIDTH, ret_dtype),
            (RET_QK_WIDTH, ret_dtype), (RET_V_WIDTH, ret_dtype), (RET_V_WIDTH, f32)]
    return pl.pallas_call(
        _proj_body,
        grid=(n // tm,),
        in_specs=[row(D_MODEL), _const_spec((1, D_MODEL)), _const_spec(w_in.shape), tab, tab],
        out_specs=[row(w) for w, _ in outs],
        out_shape=[jax.ShapeDtypeStruct((n, w), dt) for w, dt in outs],
        compiler_params=_cparams(("parallel",)),
        name="mix_proj",
    )(x, g.reshape(1, D_MODEL), w_in, cos_tab, sin_tab)


def _attn_prompt_body(q_ref, kp_ref, kc_ref, vp_ref, vc_ref, bias_ref, o_ref,
                      kk, vv, m_a, m_b, l_a, l_b, acc):
    first = (pl.program_id(2) == 0).astype(jnp.int32)
    span_rows = ATT_SPAN
    kk[0:span_rows, :] = kp_ref[...]
    kk[span_rows:, :] = kc_ref[...]
    vv[0:span_rows, :] = vp_ref[...]
    vv[span_rows:, :] = vc_ref[...]
    nq = BAND_Q
    low = lax.broadcasted_iota(jnp.int32, (nq, LANES), 1) < HD_A
    n_units = span_rows // nq

    for p, (_, dil) in enumerate(DIL_PATTERNS):
        span = nq * dil
        shift = dil.bit_length() - 1
        init = p == 0

        def load(u, p=p, dil=dil, span=span, shift=shift, init=init):
            blk = u >> shift
            base = blk * span + (u & (dil - 1))
            rows = pl.ds(base, nq, stride=dil)
            band = pl.ds(span_rows + base - span, 2 * nq, stride=dil)
            flag = jnp.where(blk == 0, first, 0)
            vals = [q_ref[rows, :], kk[band, :], vv[band, :], bias_ref[p, flag]]
            if not init:
                vals += [m_a[rows, :], m_b[rows, :], l_a[rows, :], l_b[rows, :], acc[rows, :]]
            return rows, vals

        def compute(q, kb, vb, bias, *prev):
            zero = jnp.zeros_like(q)
            lhs = jnp.concatenate([jnp.where(low, q, zero), jnp.where(low, zero, q)], axis=0)
            s = _bdot_nt(lhs.astype(jnp.bfloat16), kb.astype(jnp.bfloat16)) + bias
            m_cur = jnp.max(s, axis=1, keepdims=True)
            if prev:
                m_pa, m_pb, l_pa, l_pb, acc_prev = prev
                m_prev = jnp.concatenate([m_pa, m_pb], axis=0)
                m_next = jnp.maximum(m_prev, m_cur)
                alpha = jnp.exp(m_prev - m_next)
            else:
                m_next = jnp.broadcast_to(m_cur, (2 * nq, LANES))
            prob = jnp.exp(s - jnp.concatenate([m_next, m_next], axis=1))
            l_next = jnp.sum(prob, axis=1, keepdims=True)
            pv = _bdot(prob.astype(jnp.bfloat16), vb.astype(jnp.bfloat16))
            acc_next = jnp.where(low, pv[:nq], pv[nq:])
            if prev:
                l_next = alpha * jnp.concatenate([l_pa, l_pb], axis=0) + l_next
                acc_next = jnp.where(low, alpha[:nq], alpha[nq:]) * acc_prev + acc_next
            else:
                l_next = jnp.broadcast_to(l_next, (2 * nq, LANES))
            return m_next, l_next, acc_next

        def group(g, carry, load=load, compute=compute):
            loaded = [load(g * ATT_UNROLL + j) for j in range(ATT_UNROLL)]
            results = [compute(*vals) for _, vals in loaded]
            for (rows, _), (m_next, l_next, acc_next) in zip(loaded, results):
                acc[rows, :] = acc_next
                m_a[rows, :] = m_next[:nq]
                m_b[rows, :] = m_next[nq:]
                l_a[rows, :] = l_next[:nq]
                l_b[rows, :] = l_next[nq:]
            return carry

        lax.fori_loop(0, n_units // ATT_UNROLL, group, 0)

    low_all = lax.broadcasted_iota(jnp.int32, acc.shape, 1) < HD_A
    o_ref[...] = (acc[...] / jnp.where(low_all, l_a[...], l_b[...])).astype(o_ref.dtype)


def _prompt_bias(t5_table):
    n = BAND_Q
    after_start = np.arange(2 * n)[None, None, :] >= n
    per_pattern = []
    for _, dil in DIL_PATTERNS:
        per_step = t5_table[_t5_bucket(np.arange(n + 1) * dil)].T.astype(jnp.float32)
        diag = jnp.concatenate([jnp.full((H_A, n - 1), NEG, jnp.float32), per_step[:, ::-1],
                                jnp.full((H_A, n), NEG, jnp.float32)], axis=1)
        skew = jnp.broadcast_to(diag[:, None, :], (H_A, n, 3 * n)).reshape(H_A, -1)
        skew = skew[:, :n * (3 * n - 1)].reshape(H_A, n, 3 * n - 1)
        full = skew[:, :, n - 1:3 * n - 1]
        start = jnp.where(after_start, full, NEG)
        per_pattern.append(jnp.stack([full, start], axis=0))
    b = jnp.stack(per_pattern, axis=0)
    b = b.reshape(len(DIL_PATTERNS), 2, H_A // 2, 2 * n, 2 * n)
    return jnp.transpose(b, (2, 0, 1, 3, 4))


def _attn_prompt(qa, ka, va, bias, batch, seq):
    nblk = seq // ATT_SPAN
    n = qa.shape[0]
    cur = pl.BlockSpec((ATT_SPAN, LANES), lambda hp, b, j: (b * nblk + j, hp))
    prev = pl.BlockSpec((ATT_SPAN, LANES), lambda hp, b, j: (b * nblk + jnp.maximum(j - 1, 0), hp))
    bias_spec = pl.BlockSpec((None,) + bias.shape[1:], lambda hp, b, j: (hp, 0, 0, 0, 0))
    f32 = jnp.float32
    return pl.pallas_call(
        _attn_prompt_body,
        grid=(H_A // 2, batch, nblk),
        in_specs=[cur, prev, cur, prev, cur, bias_spec],
        out_specs=cur,
        out_shape=jax.ShapeDtypeStruct((n, A_WIDTH), jnp.bfloat16),
        scratch_shapes=[pltpu.VMEM((2 * ATT_SPAN, LANES), f32), pltpu.VMEM((2 * ATT_SPAN, LANES), f32)]
        + [pltpu.VMEM((ATT_SPAN, LANES), f32)] * 5,
        compiler_params=_cparams(("parallel", "parallel", "parallel")),
        name="attn_prompt",
    )(qa, ka, ka, va, va, bias)


def _ret_tables(length):
    lg = jnp.log(1.0 - 2.0 ** (-5.0 - jnp.arange(H_B, dtype=jnp.float32)))
    i = jnp.arange(length, dtype=jnp.float32)
    diff = i[:, None] - i[None, :]
    dmask = jnp.where(diff >= 0, jnp.exp(jnp.maximum(diff, 0.0)[None] * lg[:, None, None]), 0.0)
    dq = jnp.exp((i[:, None] + 1.0) * lg[None, :])
    dq = jnp.broadcast_to(dq.T[:, :, None], (H_B, length, DV_B))
    kd = jnp.exp((length - 1.0 - i)[:, None] * lg[None, :])
    kdec = jnp.repeat(kd, DK_B, axis=1).reshape(length, H_B // 2, 2 * DK_B).transpose(1, 0, 2)
    sd = jnp.exp(length * lg)
    sdec = jnp.broadcast_to(jnp.repeat(sd, DK_B).reshape(H_B // 2, 2 * DK_B, 1), (H_B // 2, 2 * DK_B, DV_B))
    return dmask, dq, kdec, sdec


def _norm_gate(o, gate):
    on = o * lax.rsqrt(jnp.mean(o * o, axis=-1, keepdims=True) + RMS_EPS)
    return _silu(gate) * on


def _ret_prompt_body(q_ref, k_ref, v_ref, g_ref, dmask_ref, dq_ref, kdec_ref, sdec_ref,
                     o_ref, s_out_ref, state):
    step = pl.program_id(1)

    @pl.when(step == 0)
    def _():
        state[...] = jnp.zeros(state.shape, jnp.float32)

    c_len = RET_CHUNK
    low = lax.broadcasted_iota(jnp.int32, (c_len, LANES), 1) < DK_B
    top = lax.broadcasted_iota(jnp.int32, (2 * DK_B, DV_B), 0) < DK_B

    def chunk(c, carry):
        rows = pl.ds(pl.multiple_of(c * c_len, c_len), c_len)
        for pair in range(H_B // 2):
            lanes = slice(pair * LANES, (pair + 1) * LANES)
            q2 = q_ref[rows, lanes]
            k2 = k_ref[rows, lanes]
            s_old = state[pair]
            s_b16 = s_old.astype(jnp.bfloat16)
            kt = (k2.astype(jnp.float32) * kdec_ref[pair]).T.astype(jnp.bfloat16)
            zero = jnp.zeros_like(q2)
            upd = []
            for hh in range(2):
                h = 2 * pair + hh
                vl = slice(h * DV_B, (h + 1) * DV_B)
                qm = jnp.where(low, q2, zero) if hh == 0 else jnp.where(low, zero, q2)
                vh = v_ref[rows, vl]
                a = _bdot_nt(qm, k2) * dmask_ref[h]
                o = _bdot(a.astype(jnp.bfloat16), vh) + _bdot(qm, s_b16) * dq_ref[h]
                o_ref[rows, vl] = _norm_gate(o, g_ref[rows, vl]).astype(o_ref.dtype)
                upd.append(_bdot(kt, vh))
            state[pair] = s_old * sdec_ref[pair] + jnp.where(top, upd[0], upd[1])
        return carry

    lax.fori_loop(0, q_ref.shape[0] // c_len, chunk, 0)

    @pl.when(step == pl.num_programs(1) - 1)
    def _():
        for h in range(H_B):
            s_out_ref[h] = state[h // 2, (h % 2) * DK_B:(h % 2 + 1) * DK_B, :]


def _ret_prompt(qr, kr, vb, gb, batch, seq):
    rows = 1024
    nblk = seq // rows
    n = qr.shape[0]
    dmask, dq, kdec, sdec = _ret_tables(RET_CHUNK)
    row = lambda w: pl.BlockSpec((rows, w), lambda b, j: (b * nblk + j, 0))
    return pl.pallas_call(
        _ret_prompt_body,
        grid=(batch, nblk),
        in_specs=[row(RET_QK_WIDTH), row(RET_QK_WIDTH), row(RET_V_WIDTH), row(RET_V_WIDTH),
                  _const_spec(dmask.shape), _const_spec(dq.shape), _const_spec(kdec.shape),
                  _const_spec(sdec.shape)],
        out_specs=[row(RET_V_WIDTH),
                   pl.BlockSpec((None, H_B, DK_B, DV_B), lambda b, j: (b, 0, 0, 0))],
        out_shape=[jax.ShapeDtypeStruct((n, RET_V_WIDTH), jnp.bfloat16),
                   jax.ShapeDtypeStruct((batch, H_B, DK_B, DV_B), jnp.float32)],
        scratch_shapes=[pltpu.VMEM((H_B // 2, 2 * DK_B, DV_B), jnp.float32)],
        compiler_params=_cparams(("parallel", "arbitrary")),
        name="ret_prompt",
    )(qr, kr, vb, gb, dmask, dq, kdec, sdec)


def _sample_bias(t5_table, w_buf, q_len):
    dist = np.arange(w_buf + q_len)
    by_bucket = t5_table[_t5_bucket(dist)].T.astype(jnp.float32)
    merged = None
    for win, dil in DIL_PATTERNS:
        ok = (dist % dil == 0) & (dist // dil <= win // dil)
        b = jnp.where(ok[None], by_bucket, -jnp.inf)
        merged = b if merged is None else jnp.logaddexp(merged, b)
    merged = jnp.where(jnp.isfinite(merged), merged, NEG)
    rev = jnp.concatenate([jnp.full((H_A, q_len - 1), NEG, jnp.float32), merged], axis=1)[:, ::-1]
    per_query = jnp.stack([rev[:, q_len - 1 - i:q_len - 1 - i + w_buf + q_len] for i in range(q_len)], axis=1)
    pad = jnp.full((H_A, q_len, LANES - q_len), NEG, jnp.float32)
    return jnp.concatenate([per_query, pad], axis=2).reshape(H_A * q_len, w_buf + LANES)


def _attn_sample_body(q_ref, kn_ref, vn_ref, kt_ref, vt_ref, bias_ref, o_ref):
    q_len = q_ref.shape[0]
    w_buf = kt_ref.shape[-1]
    bf16 = jnp.bfloat16
    rows = H_A * q_len
    q_rep = jnp.concatenate([q_ref[...]] * H_A, axis=0)
    row_head = jnp.right_shift(lax.broadcasted_iota(jnp.int32, (rows, A_WIDTH), 0), q_len.bit_length() - 1)
    lane_head = jnp.right_shift(lax.broadcasted_iota(jnp.int32, (rows, A_WIDTH), 1), HD_A.bit_length() - 1)
    own = row_head == lane_head
    lhs = jnp.where(own, q_rep, 0.0).astype(bf16)
    padz = jnp.zeros((LANES - q_len, A_WIDTH), jnp.float32)
    k_new = jnp.concatenate([kn_ref[...], padz], axis=0).astype(bf16)
    v_new = jnp.concatenate([vn_ref[...], padz], axis=0).astype(bf16)
    kt = kt_ref[...].reshape(A_WIDTH, w_buf).astype(bf16)
    vt = vt_ref[...].reshape(A_WIDTH, w_buf).astype(bf16)
    s_buf = _bdot(lhs, kt) + bias_ref[:, 0:w_buf]
    s_new = _bdot_nt(lhs, k_new) + bias_ref[:, w_buf:]
    m = jnp.maximum(jnp.max(s_buf, axis=1, keepdims=True), jnp.max(s_new, axis=1, keepdims=True))
    p_buf = jnp.exp(s_buf - m)
    p_new = jnp.exp(s_new - m)
    den = jnp.sum(p_buf, axis=1, keepdims=True) + jnp.sum(p_new, axis=1, keepdims=True)
    o_all = (_bdot_nt(p_buf.astype(bf16), vt) + _bdot(p_new.astype(bf16), v_new)) / den
    o_all = jnp.where(own, o_all, 0.0)
    out = o_all[0:q_len]
    for h in range(1, H_A):
        out = out + o_all[h * q_len:(h + 1) * q_len]
    o_ref[...] = out.astype(o_ref.dtype)


def _attn_sample(qa, ka, va, cache_kt, cache_vt, layer, bias, q_len):
    n = qa.shape[0]
    batch = n // q_len
    w_buf = cache_kt.shape[-1]
    new = pl.BlockSpec((q_len, A_WIDTH), lambda b: (b, 0))
    buf = pl.BlockSpec((None, None, H_A, HD_A, w_buf), lambda b: (layer, b, 0, 0, 0))
    return pl.pallas_call(
        _attn_sample_body,
        grid=(batch,),
        in_specs=[new, new, new, buf, buf, _const_spec(bias.shape)],
        out_specs=new,
        out_shape=jax.ShapeDtypeStruct((n, A_WIDTH), jnp.bfloat16),
        compiler_params=_cparams(("parallel",)),
        name="attn_sample",
    )(qa, ka, va, cache_kt, cache_vt, bias)


RET_GROUP = 16


def _ret_sample_body(q_ref, k_ref, v_ref, g_ref, s_ref, dmask_ref, dq_ref, kdec_ref, sdec_ref,
                     o_ref, s_out_ref, *, q_len):
    rows = q_ref.shape[0]
    low = lax.broadcasted_iota(jnp.int32, (rows, LANES), 1) < DK_B
    top = lax.broadcasted_iota(jnp.int32, (2 * DK_B, DV_B), 0) < DK_B
    col = lax.broadcasted_iota(jnp.int32, (2 * DK_B, rows), 1)
    for pair in range(H_B // 2):
        lanes = slice(pair * LANES, (pair + 1) * LANES)
        q2 = q_ref[:, lanes]
        k2 = k_ref[:, lanes]
        kt = (k2 * kdec_ref[pair]).T
        zero = jnp.zeros_like(q2)
        upd = [[], []]
        for hh in range(2):
            h = 2 * pair + hh
            vl = slice(h * DV_B, (h + 1) * DV_B)
            qm = jnp.where(low, q2, zero) if hh == 0 else jnp.where(low, zero, q2)
            vh = v_ref[:, vl]
            a = _bdot_nt(qm, k2) * dmask_ref[h]
            cross = jnp.concatenate(
                [_bdot(qm[r * q_len:(r + 1) * q_len], s_ref[r, pair]) for r in range(RET_GROUP)], axis=0)
            o = _bdot(a, vh) + cross * dq_ref[h]
            o_ref[:, vl] = _norm_gate(o, g_ref[:, vl]).astype(o_ref.dtype)
            for r in range(RET_GROUP):
                upd[hh].append(_bdot(jnp.where((col >= r * q_len) & (col < (r + 1) * q_len), kt, 0.0), vh))
        for r in range(RET_GROUP):
            s_out_ref[r, pair] = s_ref[r, pair] * sdec_ref[pair] + jnp.where(top, upd[0][r], upd[1][r])


def _ret_sample(qr, kr, vb, gb, state, layer, q_len):
    n = qr.shape[0]
    batch = n // q_len
    rows = RET_GROUP * q_len
    dmask1, dq1, kdec1, sdec = _ret_tables(q_len)
    eye = jnp.eye(RET_GROUP, dtype=jnp.float32)
    dmask = jnp.einsum("rs,hij->hrisj", eye, dmask1).reshape(H_B, rows, rows)
    dq = jnp.tile(dq1, (1, RET_GROUP, 1))
    kdec = jnp.tile(kdec1, (1, RET_GROUP, 1))
    row = lambda w: pl.BlockSpec((rows, w), lambda i: (i, 0))
    st_in = pl.BlockSpec((None, RET_GROUP, H_B // 2, 2 * DK_B, DV_B), lambda i: (layer, i, 0, 0, 0))
    st_out = pl.BlockSpec((RET_GROUP, H_B // 2, 2 * DK_B, DV_B), lambda i: (i, 0, 0, 0))
    return pl.pallas_call(
        functools.partial(_ret_sample_body, q_len=q_len),
        grid=(batch // RET_GROUP,),
        in_specs=[row(RET_QK_WIDTH), row(RET_QK_WIDTH), row(RET_V_WIDTH), row(RET_V_WIDTH), st_in,
                  _const_spec(dmask.shape), _const_spec(dq.shape), _const_spec(kdec.shape),
                  _const_spec(sdec.shape)],
        out_specs=[row(RET_V_WIDTH), st_out],
        out_shape=[jax.ShapeDtypeStruct((n, RET_V_WIDTH), jnp.bfloat16),
                   jax.ShapeDtypeStruct((batch, H_B // 2, 2 * DK_B, DV_B), jnp.float32)],
        compiler_params=_cparams(("parallel",)),
        name="ret_sample",
    )(qr, kr, vb, gb, state, dmask, dq, kdec, sdec)


CONV_PAD = 32


def _glu(x, g, w1_ref, b1_ref):
    xn = _rms(x, g).astype(jnp.bfloat16)
    a = _bdot(xn, w1_ref[:, 0:D_MODEL]) + b1_ref[:, 0:D_MODEL]
    gate = _bdot(xn, w1_ref[:, D_MODEL:]) + b1_ref[:, D_MODEL:]
    return a / (1.0 + jnp.exp(-gate))


def _conv_tail(x, c, lng_ref, lnb_ref, w2_ref, b2_ref):
    mu = jnp.mean(c, axis=-1, keepdims=True)
    d = c - mu
    var = jnp.mean(d * d, axis=-1, keepdims=True)
    nrm = d * lax.rsqrt(var + LN_EPS) * lng_ref[...] + lnb_ref[...]
    return x + _bdot(_silu(nrm).astype(jnp.bfloat16), w2_ref[...]) + b2_ref[...]


def _conv_prompt_body(x_ref, g_ref, w1_ref, b1_ref, wdw_ref, bdw_ref, lng_ref, lnb_ref, w2_ref, b2_ref,
                      o_ref, buf_ref, ext, cbuf):
    tm = x_ref.shape[0]
    hist = CONV_W - 1

    @pl.when(pl.program_id(1) == 0)
    def _():
        ext[0:CONV_PAD, :] = jnp.zeros((CONV_PAD, D_MODEL), jnp.float32)

    x = x_ref[...]
    ext[CONV_PAD:, :] = _glu(x, g_ref[...], w1_ref, b1_ref)

    blk, lane_blk, off = 32, 256, CONV_PAD - hist

    def rows_block(i, carry):
        r0 = pl.multiple_of(i * blk, blk)
        for lb in range(D_MODEL // lane_blk):
            lanes = slice(lb * lane_blk, (lb + 1) * lane_blk)
            win = ext[pl.ds(r0, blk + CONV_PAD), lanes]
            acc = jnp.broadcast_to(bdw_ref[:, lanes], (blk, lane_blk))
            for r in range(SUBLANES):
                taps = [k for k in range(CONV_W) if (k + off) % SUBLANES == r]
                rows = blk if r == 0 else blk + SUBLANES
                y = None
                for k in taps:
                    a0 = k + off - r
                    tiles = win[a0:a0 + rows].reshape(rows // SUBLANES, SUBLANES, lane_blk)
                    term = (tiles * wdw_ref[k, :, lanes]).reshape(rows, lane_blk)
                    y = term if y is None else y + term
                acc = acc + y[r:r + blk]
            cbuf[pl.ds(r0, blk), lanes] = acc
        return carry
    lax.fori_loop(0, tm // blk, rows_block, 0)

    o_ref[...] = _conv_tail(x, cbuf[...], lng_ref, lnb_ref, w2_ref, b2_ref)
    buf_ref[...] = ext[tm + CONV_PAD - hist:, :]
    ext[0:CONV_PAD, :] = ext[tm:, :]


def _conv_prompt(x, g, w1, b1, wdw, bdw, lng, lnb, w2, b2, batch, seq):
    tm = ROW_TILE
    nblk = seq // tm
    n = x.shape[0]
    row = pl.BlockSpec((tm, D_MODEL), lambda b, j: (b * nblk + j, 0))
    vec = lambda a: a.reshape(1, -1)
    f32 = jnp.float32
    wdw = jnp.broadcast_to(wdw[:, None, :], (CONV_W, SUBLANES, D_MODEL))
    return pl.pallas_call(
        _conv_prompt_body,
        grid=(batch, nblk),
        in_specs=[row, _const_spec((1, D_MODEL)), _const_spec(w1.shape), _const_spec((1, 2 * D_MODEL)),
                  _const_spec(wdw.shape), _const_spec((1, D_MODEL)), _const_spec((1, D_MODEL)),
                  _const_spec((1, D_MODEL)), _const_spec(w2.shape), _const_spec((1, D_MODEL))],
        out_specs=[row, pl.BlockSpec((None, CONV_W - 1, D_MODEL), lambda b, j: (b, 0, 0))],
        out_shape=[jax.ShapeDtypeStruct((n, D_MODEL), f32),
                   jax.ShapeDtypeStruct((batch, CONV_W - 1, D_MODEL), f32)],
        scratch_shapes=[pltpu.VMEM((tm + CONV_PAD, D_MODEL), f32), pltpu.VMEM((tm, D_MODEL), f32)],
        compiler_params=_cparams(("parallel", "arbitrary")),
        name="conv_prompt",
    )(x, vec(g), w1, vec(b1), wdw, vec(bdw), vec(lng), vec(lnb), w2, vec(b2))


CONV_GROUP = 16


def _conv_sample_body(x_ref, g_ref, w1_ref, b1_ref, wdw_ref, bdw_ref, lng_ref, lnb_ref, w2_ref, b2_ref,
                      st_ref, o_ref, buf_ref, ext, cbuf, *, q_len):
    hist = CONV_W - 1
    x = x_ref[...]
    glu = _glu(x, g_ref[...], w1_ref, b1_ref)
    for r in range(CONV_GROUP):
        ext[r, 0:hist, :] = st_ref[r]
        ext[r, hist:hist + q_len, :] = glu[r * q_len:(r + 1) * q_len]
    for r in range(CONV_GROUP):
        acc = jnp.broadcast_to(bdw_ref[...], (q_len, D_MODEL))
        for k in range(CONV_W):
            acc = acc + ext[r, k:k + q_len, :] * wdw_ref[k:k + 1, :]
        cbuf[r * q_len:(r + 1) * q_len, :] = acc
        buf_ref[r] = ext[r, q_len:q_len + hist, :]
    o_ref[...] = _conv_tail(x, cbuf[...], lng_ref, lnb_ref, w2_ref, b2_ref)


def _conv_sample(x, g, w1, b1, wdw, bdw, lng, lnb, w2, b2, state, layer, q_len):
    n = x.shape[0]
    batch = n // q_len
    rows = CONV_GROUP * q_len
    hist = CONV_W - 1
    row = pl.BlockSpec((rows, D_MODEL), lambda i: (i, 0))
    vec = lambda a: a.reshape(1, -1)
    f32 = jnp.float32
    return pl.pallas_call(
        functools.partial(_conv_sample_body, q_len=q_len),
        grid=(batch // CONV_GROUP,),
        in_specs=[row, _const_spec((1, D_MODEL)), _const_spec(w1.shape), _const_spec((1, 2 * D_MODEL)),
                  _const_spec(wdw.shape), _const_spec((1, D_MODEL)), _const_spec((1, D_MODEL)),
                  _const_spec((1, D_MODEL)), _const_spec(w2.shape), _const_spec((1, D_MODEL)),
                  pl.BlockSpec((None, CONV_GROUP, hist, D_MODEL), lambda i: (layer, i, 0, 0))],
        out_specs=[row, pl.BlockSpec((CONV_GROUP, hist, D_MODEL), lambda i: (i, 0, 0))],
        out_shape=[jax.ShapeDtypeStruct((n, D_MODEL), f32),
                   jax.ShapeDtypeStruct((batch, hist, D_MODEL), f32)],
        scratch_shapes=[pltpu.VMEM((CONV_GROUP, hist + q_len + 2, D_MODEL), f32),
                        pltpu.VMEM((rows, D_MODEL), f32)],
        compiler_params=_cparams(("parallel",)),
        name="conv_sample",
    )(x, vec(g), w1, vec(b1), wdw, vec(bdw), vec(lng), vec(lnb), w2, vec(b2), state)


def _t5_bucket(dist):
    dist = np.asarray(dist).astype(np.int32)
    max_exact = T5_BUCKETS // 2
    log_ratio = np.log(np.maximum(dist, 1) / max_exact) / math.log(T5_MAX_DIST / max_exact)
    large = np.minimum(max_exact + (log_ratio * (T5_BUCKETS - max_exact)).astype(np.int32), T5_BUCKETS - 1)
    return np.where(dist < max_exact, dist, large).astype(np.int32)


def _rot_tables(pos):
    half = DK_B // 2
    ang = 1.0 / (10000.0 ** jnp.linspace(0.0, 1.0, half, dtype=jnp.float32))
    th = pos.astype(jnp.float32)[:, None] * ang[None, :]
    cos = jnp.repeat(jnp.cos(th), 2, axis=1)
    sin = jnp.repeat(jnp.sin(th), 2, axis=1) * jnp.tile(jnp.array([-1.0, 1.0], jnp.float32), half)[None, :]
    return jnp.tile(cos, (1, H_B)), jnp.tile(sin, (1, H_B))


def kernel(x_prompt, x_sample, cache_win_k, cache_win_v, state_ret, state_conv, t5_table, rms_g, final_g,
           w_ffn_gate, w_ffn_up, w_ffn_down, w_mix_in, w_mix_out, w_pw1, b_pw1, w_dw, b_dw, ln_g, ln_b,
           w_pw2, b_pw2):
    batch, seq, _ = x_prompt.shape
    dec_batch, dec_seq, _ = x_sample.shape
    depth = rms_g.shape[0]
    n_even = cache_win_k.shape[0]
    w_buf = cache_win_k.shape[2]
    keep = min(DIL_PATTERNS[-1][0], seq)
    bf16 = jnp.bfloat16

    xp = x_prompt.reshape(batch * seq, D_MODEL)
    xs = x_sample.reshape(dec_batch * dec_seq, D_MODEL)

    wg, wu, wd = w_ffn_gate.astype(bf16), w_ffn_up.astype(bf16), w_ffn_down.astype(bf16)
    w_in, w_out = w_mix_in.astype(bf16), w_mix_out.astype(bf16)
    w1, w2 = w_pw1.astype(bf16), w_pw2.astype(bf16)

    cos_p, sin_p = _rot_tables(jnp.arange(seq))
    cos_s, sin_s = _rot_tables(PAST_LEN + jnp.arange(ROW_TILE) % dec_seq)
    bias_p = _prompt_bias(t5_table)
    bias_s = _sample_bias(t5_table, w_buf, dec_seq)
    assert dec_seq & (dec_seq - 1) == 0 and dec_seq <= LANES
    ck = jnp.transpose(cache_win_k, (0, 1, 3, 4, 2))
    cv = jnp.transpose(cache_win_v, (0, 1, 3, 4, 2))
    st_ret = state_ret.reshape(n_even, dec_batch, H_B // 2, 2 * DK_B, DV_B)

    kp_l, vp_l, sp_l, cp_l, ks_l, vs_l, ss_l, cs_l = [], [], [], [], [], [], [], []
    for layer in range(depth):
        g = rms_g[layer]
        xp = _ffn(xp, g[0], wg[layer, 0], wu[layer, 0], wd[layer, 0])
        xs = _ffn(xs, g[0], wg[layer, 0], wu[layer, 0], wd[layer, 0])
        last = layer == depth - 1
        post = final_g if last else None
        ffn_b = (g[2], wg[layer, 1], wu[layer, 1], wd[layer, 1])
        if layer % 2 == 0:
            e = layer // 2
            qa, ka, va, qr, kr, vb, gb = _proj(xp, g[1], w_in[e], cos_p, sin_p, bf16)
            oa = _attn_prompt(qa, ka, va, bias_p, batch, seq)
            ob, s_end = _ret_prompt(qr, kr, vb, gb, batch, seq)
            xp = _ffn(xp, *ffn_b, pre=(oa, ob, w_out[e]), post_g=post)
            kp_l.append(ka.reshape(batch, seq, H_A, HD_A)[:, -keep:])
            vp_l.append(va.reshape(batch, seq, H_A, HD_A)[:, -keep:])
            sp_l.append(s_end)

            qa, ka, va, qr, kr, vb, gb = _proj(xs, g[1], w_in[e], cos_s, sin_s, jnp.float32)
            oa = _attn_sample(qa, ka, va, ck, cv, e, bias_s, dec_seq)
            ob, s_new = _ret_sample(qr, kr, vb, gb, st_ret, e, dec_seq)
            xs = _ffn(xs, *ffn_b, pre=(oa, ob, w_out[e]), post_g=post)
            ks_l.append(ka.reshape(dec_batch, dec_seq, H_A, HD_A))
            vs_l.append(va.reshape(dec_batch, dec_seq, H_A, HD_A))
            ss_l.append(s_new.reshape(dec_batch, H_B, DK_B, DV_B))
        else:
            o = layer // 2
            conv = (g[1], w1[o], b_pw1[o], w_dw[o], b_dw[o], ln_g[o], ln_b[o], w2[o], b_pw2[o])
            xp, cp = _conv_prompt(xp, *conv, batch, seq)
            xs, cs = _conv_sample(xs, *conv, state_conv, o, dec_seq)
            cp_l.append(cp)
            cs_l.append(cs)
            xp = _ffn(xp, *ffn_b, post_g=post)
            xs = _ffn(xs, *ffn_b, post_g=post)

    stk = jnp.stack
    return (xp.reshape(batch, seq, D_MODEL), xs.reshape(dec_batch, dec_seq, D_MODEL),
            stk(kp_l), stk(vp_l), stk(sp_l), stk(cp_l), stk(ks_l), stk(vs_l), stk(ss_l), stk(cs_l))
```

```python
import functools
import math

import jax
import jax.numpy as jnp
import numpy as np
from jax import lax
from jax.experimental import pallas as pl
from jax.experimental.pallas import tpu as pltpu

D_MODEL = 1024
D_FF = 2816
H_A, HD_A = 8, 64
A_WIDTH = H_A * HD_A
DIL_PATTERNS = ((128, 1), (512, 4), (2048, 16))
H_B, DK_B, DV_B = 4, 64, 128
RET_QK_WIDTH = H_B * DK_B
RET_V_WIDTH = H_B * DV_B
RET_CHUNK = 128
PROJ_COLS = 3 * A_WIDTH + 2 * RET_QK_WIDTH + 2 * RET_V_WIDTH
CONV_W = 31
T5_BUCKETS = 32
T5_MAX_DIST = 2048
PAST_LEN = 2048
RMS_EPS = 1e-6
LN_EPS = 1e-5

LANES = 128
SUBLANES = 8
VMEM_LIMIT_BYTES = 56 * 1024 * 1024

NEG = -1e30
BAND_Q = 128
ATT_SPAN = 2048
ATT_UNROLL = 8
ROW_TILE = 512


def _cparams(sem):
    return pltpu.CompilerParams(dimension_semantics=sem, vmem_limit_bytes=VMEM_LIMIT_BYTES)


def _const_spec(shape):
    nd = len(shape)
    return pl.BlockSpec(shape, lambda *_: (0,) * nd, pipeline_mode=pl.Buffered(1))


def _stack_spec(stacked, idx):
    tail = stacked.shape[len(idx):]
    return pl.BlockSpec((None,) * len(idx) + tail, lambda *_: tuple(idx) + (0,) * len(tail),
                        pipeline_mode=pl.Buffered(1))


def _rms(x, g):
    y = x * lax.rsqrt(jnp.mean(x * x, axis=-1, keepdims=True) + RMS_EPS)
    return y * g


def _silu(x):
    return x / (1.0 + jnp.exp(-x))


def _bdot(a, b):
    return jnp.dot(a, b, preferred_element_type=jnp.float32)


def _bdot_nt(a, b):
    return lax.dot_general(a, b, (((1,), (1,)), ((), ())), preferred_element_type=jnp.float32)


def _ffn_body(*refs, pre, post, ff_chunk):
    it = iter(refs)
    x_ref = next(it)
    if pre:
        oa_ref, ob_ref, wo_ref = next(it), next(it), next(it)
    g_ref, wg_ref, wu_ref, wd_ref = next(it), next(it), next(it), next(it)
    if post:
        fg_ref = next(it)
    o_ref = next(it)
    h_scr = next(it)

    x = x_ref[...]
    if pre:
        oa = oa_ref[...].astype(jnp.bfloat16)
        x = x + _bdot(oa, wo_ref[0:A_WIDTH, :]) + _bdot(ob_ref[...], wo_ref[A_WIDTH:, :])
    xn = _rms(x, g_ref[...]).astype(jnp.bfloat16)
    for c in range(D_FF // ff_chunk):
        sl = slice(c * ff_chunk, (c + 1) * ff_chunk)
        gate = _bdot(xn, wg_ref[:, sl])
        up = _bdot(xn, wu_ref[:, sl])
        h_scr[:, sl] = (_silu(gate) * up).astype(jnp.bfloat16)
    y = x + 0.5 * _bdot(h_scr[...], wd_ref[...])
    if post:
        y = _rms(y, fg_ref[...])
    o_ref[...] = y


def _ffn(x, g, wg, wu, wd, idx, pre=None, post_g=None):
    n = x.shape[0]
    tm = ROW_TILE
    row = lambda w: pl.BlockSpec((tm, w), lambda i: (i, 0))
    args, specs = [x], [row(D_MODEL)]
    if pre is not None:
        oa, ob, wo, wo_idx = pre
        args += [oa, ob, wo]
        specs += [row(A_WIDTH), row(RET_V_WIDTH), _stack_spec(wo, wo_idx)]
    args += [g.reshape(1, D_MODEL), wg, wu, wd]
    specs += [_const_spec((1, D_MODEL)), _stack_spec(wg, idx), _stack_spec(wu, idx), _stack_spec(wd, idx)]
    if post_g is not None:
        args.append(post_g.reshape(1, D_MODEL))
        specs.append(_const_spec((1, D_MODEL)))
    body = functools.partial(_ffn_body, pre=pre is not None, post=post_g is not None, ff_chunk=D_FF // 2)
    return pl.pallas_call(
        body,
        grid=(n // tm,),
        in_specs=specs,
        out_specs=row(D_MODEL),
        out_shape=jax.ShapeDtypeStruct((n, D_MODEL), jnp.float32),
        scratch_shapes=[pltpu.VMEM((tm, D_FF), jnp.bfloat16)],
        compiler_params=_cparams(("parallel",)),
        name="ffn",
    )(*args)


def _rotate(x, cos, sin_signed):
    w = x.shape[-1]
    lane = lax.broadcasted_iota(jnp.int32, x.shape, 1)
    partner = jnp.where((lane & 1) == 0, pltpu.roll(x, w - 1, 1), pltpu.roll(x, 1, 1))
    return x * cos + partner * sin_signed


def _proj_body(x_ref, g_ref, w_ref, cos_ref, sin_ref,
               qa_ref, ka_ref, va_ref, qr_ref, kr_ref, vb_ref, gb_ref):
    xn = _rms(x_ref[...], g_ref[...]).astype(jnp.bfloat16)
    a, qk, v = A_WIDTH, RET_QK_WIDTH, RET_V_WIDTH
    seg = lambda lo, width: _bdot(xn, w_ref[:, lo:lo + width])
    qa_ref[...] = seg(0, a) * (HD_A ** -0.5)
    ka_ref[...] = seg(a, a)
    va_ref[...] = seg(2 * a, a)
    cos, sin = cos_ref[...], sin_ref[...]
    qr_ref[...] = _rotate(seg(3 * a, qk), cos, sin).astype(qr_ref.dtype)
    kr_ref[...] = (_rotate(seg(3 * a + qk, qk), cos, sin) * (DK_B ** -0.5)).astype(kr_ref.dtype)
    vb_ref[...] = seg(3 * a + 2 * qk, v).astype(vb_ref.dtype)
    gb_ref[...] = seg(3 * a + 2 * qk + v, v)


def _proj(x, g, w_in, layer, cos_tab, sin_tab, ret_dtype):
    n = x.shape[0]
    tm = ROW_TILE
    tab_blocks = cos_tab.shape[0] // tm
    row = lambda w: pl.BlockSpec((tm, w), lambda i: (i, 0))
    tab = pl.BlockSpec((tm, RET_QK_WIDTH), lambda i: (i % tab_blocks, 0))
    f32 = jnp.float32
    outs = [(A_WIDTH, f32), (A_WIDTH, f32), (A_WIDTH, f32), (RET_QK_WIDTH, ret_dtype),
            (RET_QK_WIDTH, ret_dtype), (RET_V_WIDTH, ret_dtype), (RET_V_WIDTH, f32)]
    return pl.pallas_call(
        _proj_body,
        grid=(n // tm,),
        in_specs=[row(D_MODEL), _const_spec((1, D_MODEL)), _stack_spec(w_in, (layer,)), tab, tab],
        out_specs=[row(w) for w, _ in outs],
        out_shape=[jax.ShapeDtypeStruct((n, w), dt) for w, dt in outs],
        compiler_params=_cparams(("parallel",)),
        name="mix_proj",
    )(x, g.reshape(1, D_MODEL), w_in, cos_tab, sin_tab)


def _band_unit(q, kb, vb, bias, low, prev=None):
    nq = q.shape[0]
    zero = jnp.zeros_like(q)
    lhs = jnp.concatenate([jnp.where(low, q, zero), jnp.where(low, zero, q)], axis=0)
    s = _bdot_nt(lhs.astype(jnp.bfloat16), kb.astype(jnp.bfloat16)) + bias
    m_cur = jnp.max(s, axis=1, keepdims=True)
    if prev is not None:
        m_pa, m_pb, l_pa, l_pb, acc_prev = prev
        m_prev = jnp.concatenate([m_pa, m_pb], axis=0)
        m_next = jnp.maximum(m_prev, m_cur)
        alpha = jnp.exp(m_prev - m_next)
    else:
        m_next = jnp.broadcast_to(m_cur, (2 * nq, LANES))
    prob = jnp.exp(s - jnp.concatenate([m_next, m_next], axis=1))
    l_next = jnp.sum(prob, axis=1, keepdims=True)
    pv = _bdot(prob.astype(jnp.bfloat16), vb.astype(jnp.bfloat16))
    acc_next = jnp.where(low, pv[:nq], pv[nq:])
    if prev is not None:
        l_next = alpha * jnp.concatenate([l_pa, l_pb], axis=0) + l_next
        acc_next = jnp.where(low, alpha[:nq], alpha[nq:]) * acc_prev + acc_next
    else:
        l_next = jnp.broadcast_to(l_next, (2 * nq, LANES))
    return m_next, l_next, acc_next


NEAR_PATTERNS = DIL_PATTERNS[:2]


def _attn_near_body(q_ref, kp_ref, kc_ref, vp_ref, vc_ref, bias_ref, o_ref, lse_ref,
                    kk, vv, m_a, m_b, l_a, l_b, acc):
    first = (pl.program_id(2) == 0).astype(jnp.int32)
    span_rows = ATT_SPAN
    kk[0:span_rows, :] = kp_ref[...]
    kk[span_rows:, :] = kc_ref[...]
    vv[0:span_rows, :] = vp_ref[...]
    vv[span_rows:, :] = vc_ref[...]
    nq = BAND_Q
    low = lax.broadcasted_iota(jnp.int32, (nq, LANES), 1) < HD_A
    n_units = span_rows // nq

    for p, (_, dil) in enumerate(NEAR_PATTERNS):
        span = nq * dil
        shift = dil.bit_length() - 1
        init = p == 0

        def load(u, p=p, dil=dil, span=span, shift=shift, init=init):
            blk = u >> shift
            base = blk * span + (u & (dil - 1))
            rows = pl.ds(base, nq, stride=dil)
            band = pl.ds(span_rows + base - span, 2 * nq, stride=dil)
            flag = jnp.where(blk == 0, first, 0)
            vals = [q_ref[rows, :], kk[band, :], vv[band, :], bias_ref[p, flag]]
            prev = None if init else (m_a[rows, :], m_b[rows, :], l_a[rows, :], l_b[rows, :], acc[rows, :])
            return rows, vals, prev

        def group(g, carry, load=load):
            loaded = [load(g * ATT_UNROLL + j) for j in range(ATT_UNROLL)]
            results = [_band_unit(*vals, low, prev) for _, vals, prev in loaded]
            for (rows, _, _), (m_next, l_next, acc_next) in zip(loaded, results):
                acc[rows, :] = acc_next
                m_a[rows, :] = m_next[:nq]
                m_b[rows, :] = m_next[nq:]
                l_a[rows, :] = l_next[:nq]
                l_b[rows, :] = l_next[nq:]
            return carry

        lax.fori_loop(0, n_units // ATT_UNROLL, group, 0)

    low_all = lax.broadcasted_iota(jnp.int32, acc.shape, 1) < HD_A
    den = jnp.where(low_all, l_a[...], l_b[...])
    o_ref[...] = acc[...] / den
    lse_ref[...] = jnp.where(low_all, m_a[...], m_b[...]) + jnp.log(den)


FAR_DIL = DIL_PATTERNS[-1][1]
FAR_CLASSES = 2
FAR_STREAMS = 7


def _attn_far_body(q_hbm, k_hbm, v_hbm, near_hbm, lse_hbm, bias_ref, o_hbm,
                   qbuf, kbuf, vbuf, nearbuf, lsebuf, obuf, in_sem, out_sem, *, nblk):
    step = pl.program_id(0)
    n_steps = pl.num_programs(0)
    nq = BAND_Q
    per_span = FAR_DIL // FAR_CLASSES

    def locate(s):
        span = s // per_span
        first = (span % nblk) == 0
        return span, jnp.where(first, span, span - 1), (s % per_span) * FAR_CLASSES, first

    def in_copies(s, slot):
        span, prev, r0, _ = locate(s)
        copies = []
        for c in range(FAR_CLASSES):
            r = r0 + c
            pairs = [(q_hbm.at[span, :, r, :], qbuf.at[slot, c]),
                     (k_hbm.at[prev, :, r, :], kbuf.at[slot, c, pl.ds(0, nq), :]),
                     (k_hbm.at[span, :, r, :], kbuf.at[slot, c, pl.ds(nq, nq), :]),
                     (v_hbm.at[prev, :, r, :], vbuf.at[slot, c, pl.ds(0, nq), :]),
                     (v_hbm.at[span, :, r, :], vbuf.at[slot, c, pl.ds(nq, nq), :]),
                     (near_hbm.at[span, :, r, :], nearbuf.at[slot, c]),
                     (lse_hbm.at[span, :, r, :], lsebuf.at[slot, c])]
            copies += [pltpu.make_async_copy(src, dst, in_sem.at[slot, c, i])
                       for i, (src, dst) in enumerate(pairs)]
        return copies

    def out_copies(s, slot):
        span, _, r0, _ = locate(s)
        return [pltpu.make_async_copy(obuf.at[slot, c], o_hbm.at[span, :, r0 + c, :], out_sem.at[slot, c])
                for c in range(FAR_CLASSES)]

    slot = step % 2

    @pl.when(step == 0)
    def _():
        for cp in in_copies(0, 0):
            cp.start()

    @pl.when(step + 1 < n_steps)
    def _():
        for cp in in_copies(step + 1, 1 - slot):
            cp.start()

    for cp in in_copies(step, slot):
        cp.wait()

    @pl.when(step >= 2)
    def _():
        for cp in out_copies(step - 2, slot):
            cp.wait()

    first = locate(step)[3].astype(jnp.int32)
    low = lax.broadcasted_iota(jnp.int32, (nq, LANES), 1) < HD_A
    for c in range(FAR_CLASSES):
        for pair in range(H_A // 2):
            lanes = slice(pair * LANES, (pair + 1) * LANES)
            m_far, l_far, acc_far = _band_unit(qbuf[slot, c, :, lanes], kbuf[slot, c, :, lanes],
                                               vbuf[slot, c, :, lanes], bias_ref[pair, first], low)
            m_far = jnp.where(low, m_far[:nq], m_far[nq:])
            l_far = jnp.where(low, l_far[:nq], l_far[nq:])
            lse_near = lsebuf[slot, c, :, lanes]
            m_all = jnp.maximum(lse_near, m_far)
            w_near = jnp.exp(lse_near - m_all)
            w_far = jnp.exp(m_far - m_all)
            obuf[slot, c, :, lanes] = ((w_near * nearbuf[slot, c, :, lanes] + w_far * acc_far)
                                       / (w_near + w_far * l_far))

    for cp in out_copies(step, slot):
        cp.start()

    @pl.when(step == n_steps - 1)
    def _():
        for cp in out_copies(step, slot) + out_copies(step - 1, 1 - slot):
            cp.wait()


def _prompt_bias(t5_table):
    n = BAND_Q
    after_start = np.arange(2 * n)[None, None, :] >= n
    per_pattern = []
    for _, dil in DIL_PATTERNS:
        per_step = t5_table[_t5_bucket(np.arange(n + 1) * dil)].T.astype(jnp.float32)
        diag = jnp.concatenate([jnp.full((H_A, n - 1), NEG, jnp.float32), per_step[:, ::-1],
                                jnp.full((H_A, n), NEG, jnp.float32)], axis=1)
        skew = jnp.broadcast_to(diag[:, None, :], (H_A, n, 3 * n)).reshape(H_A, -1)
        skew = skew[:, :n * (3 * n - 1)].reshape(H_A, n, 3 * n - 1)
        full = skew[:, :, n - 1:3 * n - 1]
        start = jnp.where(after_start, full, NEG)
        per_pattern.append(jnp.stack([full, start], axis=0))
    b = jnp.stack(per_pattern, axis=0)
    b = b.reshape(len(DIL_PATTERNS), 2, H_A // 2, 2 * n, 2 * n)
    return jnp.transpose(b, (2, 0, 1, 3, 4))


def _attn_prompt(qa, ka, va, bias, batch, seq):
    assert ATT_SPAN == BAND_Q * FAR_DIL and FAR_DIL % FAR_CLASSES == 0
    nblk = seq // ATT_SPAN
    n = qa.shape[0]
    cur = pl.BlockSpec((ATT_SPAN, LANES), lambda hp, b, j: (b * nblk + j, hp))
    prev = pl.BlockSpec((ATT_SPAN, LANES), lambda hp, b, j: (b * nblk + jnp.maximum(j - 1, 0), hp))
    n_near = len(NEAR_PATTERNS)
    bias_near, bias_far = bias[:, :n_near], bias[:, n_near]
    bias_spec = pl.BlockSpec((None,) + bias_near.shape[1:], lambda hp, b, j: (hp, 0, 0, 0, 0))
    f32 = jnp.float32
    o_near, lse_near = pl.pallas_call(
        _attn_near_body,
        grid=(H_A // 2, batch, nblk),
        in_specs=[cur, prev, cur, prev, cur, bias_spec],
        out_specs=[cur, cur],
        out_shape=[jax.ShapeDtypeStruct((n, A_WIDTH), f32)] * 2,
        scratch_shapes=[pltpu.VMEM((2 * ATT_SPAN, LANES), f32), pltpu.VMEM((2 * ATT_SPAN, LANES), f32)]
        + [pltpu.VMEM((ATT_SPAN, LANES), f32)] * 5,
        compiler_params=_cparams(("parallel", "parallel", "parallel")),
        name="attn_near",
    )(qa, ka, ka, va, va, bias_near)

    spans = batch * nblk
    view = lambda a: a.reshape(spans, BAND_Q, FAR_DIL, A_WIDTH)
    n_steps = spans * FAR_DIL // FAR_CLASSES
    assert n_steps >= 2
    hbm = pl.BlockSpec(memory_space=pl.ANY)
    slots = lambda rows: pltpu.VMEM((2, FAR_CLASSES, rows, A_WIDTH), f32)
    out = pl.pallas_call(
        functools.partial(_attn_far_body, nblk=nblk),
        grid=(n_steps,),
        in_specs=[hbm] * 5 + [_const_spec(bias_far.shape)],
        out_specs=hbm,
        out_shape=jax.ShapeDtypeStruct((spans, BAND_Q, FAR_DIL, A_WIDTH), f32),
        scratch_shapes=[slots(BAND_Q), slots(2 * BAND_Q), slots(2 * BAND_Q), slots(BAND_Q), slots(BAND_Q),
                        slots(BAND_Q), pltpu.SemaphoreType.DMA((2, FAR_CLASSES, FAR_STREAMS)),
                        pltpu.SemaphoreType.DMA((2, FAR_CLASSES))],
        compiler_params=_cparams(("arbitrary",)),
        name="attn_far",
    )(view(qa), view(ka), view(va), view(o_near), view(lse_near), bias_far)
    return out.reshape(n, A_WIDTH)


def _ret_tables(length):
    lg = jnp.log(1.0 - 2.0 ** (-5.0 - jnp.arange(H_B, dtype=jnp.float32)))
    i = jnp.arange(length, dtype=jnp.float32)
    diff = i[:, None] - i[None, :]
    dmask = jnp.where(diff >= 0, jnp.exp(jnp.maximum(diff, 0.0)[None] * lg[:, None, None]), 0.0)
    dq = jnp.exp((i[:, None] + 1.0) * lg[None, :])
    dq = jnp.broadcast_to(dq.T[:, :, None], (H_B, length, DV_B))
    kd = jnp.exp((length - 1.0 - i)[:, None] * lg[None, :])
    kdec = jnp.repeat(kd, DK_B, axis=1).reshape(length, H_B // 2, 2 * DK_B).transpose(1, 0, 2)
    sd = jnp.exp(length * lg)
    sdec = jnp.broadcast_to(jnp.repeat(sd, DK_B).reshape(H_B // 2, 2 * DK_B, 1), (H_B // 2, 2 * DK_B, DV_B))
    return dmask, dq, kdec, sdec


def _norm_gate(o, gate):
    on = o * lax.rsqrt(jnp.mean(o * o, axis=-1, keepdims=True) + RMS_EPS)
    return _silu(gate) * on


def _ret_prompt_body(q_ref, k_ref, v_ref, g_ref, dmask_ref, dq_ref, kdec_ref, sdec_ref,
                     o_ref, s_out_ref, state):
    step = pl.program_id(1)

    @pl.when(step == 0)
    def _():
        state[...] = jnp.zeros(state.shape, jnp.float32)

    c_len = RET_CHUNK
    low = lax.broadcasted_iota(jnp.int32, (c_len, LANES), 1) < DK_B
    top = lax.broadcasted_iota(jnp.int32, (2 * DK_B, DV_B), 0) < DK_B

    def chunk(c, carry):
        rows = pl.ds(pl.multiple_of(c * c_len, c_len), c_len)
        for pair in range(H_B // 2):
            lanes = slice(pair * LANES, (pair + 1) * LANES)
            q2 = q_ref[rows, lanes]
            k2 = k_ref[rows, lanes]
            s_old = state[pair]
            s_b16 = s_old.astype(jnp.bfloat16)
            kt = (k2.astype(jnp.float32) * kdec_ref[pair]).T.astype(jnp.bfloat16)
            zero = jnp.zeros_like(q2)
            upd = []
            for hh in range(2):
                h = 2 * pair + hh
                vl = slice(h * DV_B, (h + 1) * DV_B)
                qm = jnp.where(low, q2, zero) if hh == 0 else jnp.where(low, zero, q2)
                vh = v_ref[rows, vl]
                a = _bdot_nt(qm, k2) * dmask_ref[h]
                o = _bdot(a.astype(jnp.bfloat16), vh) + _bdot(qm, s_b16) * dq_ref[h]
                o_ref[rows, vl] = _norm_gate(o, g_ref[rows, vl]).astype(o_ref.dtype)
                upd.append(_bdot(kt, vh))
            state[pair] = s_old * sdec_ref[pair] + jnp.where(top, upd[0], upd[1])
        return carry

    lax.fori_loop(0, q_ref.shape[0] // c_len, chunk, 0)

    @pl.when(step == pl.num_programs(1) - 1)
    def _():
        for h in range(H_B):
            s_out_ref[h] = state[h // 2, (h % 2) * DK_B:(h % 2 + 1) * DK_B, :]


def _ret_prompt(qr, kr, vb, gb, batch, seq):
    rows = 1024
    nblk = seq // rows
    n = qr.shape[0]
    dmask, dq, kdec, sdec = _ret_tables(RET_CHUNK)
    row = lambda w: pl.BlockSpec((rows, w), lambda b, j: (b * nblk + j, 0))
    return pl.pallas_call(
        _ret_prompt_body,
        grid=(batch, nblk),
        in_specs=[row(RET_QK_WIDTH), row(RET_QK_WIDTH), row(RET_V_WIDTH), row(RET_V_WIDTH),
                  _const_spec(dmask.shape), _const_spec(dq.shape), _const_spec(kdec.shape),
                  _const_spec(sdec.shape)],
        out_specs=[row(RET_V_WIDTH),
                   pl.BlockSpec((None, H_B, DK_B, DV_B), lambda b, j: (b, 0, 0, 0))],
        out_shape=[jax.ShapeDtypeStruct((n, RET_V_WIDTH), jnp.bfloat16),
                   jax.ShapeDtypeStruct((batch, H_B, DK_B, DV_B), jnp.float32)],
        scratch_shapes=[pltpu.VMEM((H_B // 2, 2 * DK_B, DV_B), jnp.float32)],
        compiler_params=_cparams(("parallel", "arbitrary")),
        name="ret_prompt",
    )(qr, kr, vb, gb, dmask, dq, kdec, sdec)


def _sample_bias(t5_table, w_buf, q_len):
    dist = np.arange(w_buf + q_len)
    by_bucket = t5_table[_t5_bucket(dist)].T.astype(jnp.float32)
    merged = None
    for win, dil in DIL_PATTERNS:
        ok = (dist % dil == 0) & (dist // dil <= win // dil)
        b = jnp.where(ok[None], by_bucket, -jnp.inf)
        merged = b if merged is None else jnp.logaddexp(merged, b)
    merged = jnp.where(jnp.isfinite(merged), merged, NEG)
    rev = jnp.concatenate([jnp.full((H_A, q_len - 1), NEG, jnp.float32), merged], axis=1)[:, ::-1]
    per_query = jnp.stack([rev[:, q_len - 1 - i:q_len - 1 - i + w_buf + q_len] for i in range(q_len)], axis=1)
    pad = jnp.full((H_A, q_len, LANES - q_len), NEG, jnp.float32)
    return jnp.concatenate([per_query, pad], axis=2).reshape(H_A * q_len, w_buf + LANES)


def _attn_sample_body(q_ref, kn_ref, vn_ref, kt_ref, vt_ref, bias_ref, o_ref):
    q_len = q_ref.shape[0]
    w_buf = kt_ref.shape[-1]
    bf16 = jnp.bfloat16
    rows = H_A * q_len
    q_rep = jnp.concatenate([q_ref[...]] * H_A, axis=0)
    row_head = jnp.right_shift(lax.broadcasted_iota(jnp.int32, (rows, A_WIDTH), 0), q_len.bit_length() - 1)
    lane_head = jnp.right_shift(lax.broadcasted_iota(jnp.int32, (rows, A_WIDTH), 1), HD_A.bit_length() - 1)
    own = row_head == lane_head
    lhs = jnp.where(own, q_rep, 0.0).astype(bf16)
    padz = jnp.zeros((LANES - q_len, A_WIDTH), jnp.float32)
    k_new = jnp.concatenate([kn_ref[...], padz], axis=0).astype(bf16)
    v_new = jnp.concatenate([vn_ref[...], padz], axis=0).astype(bf16)
    kt = kt_ref[...].reshape(A_WIDTH, w_buf).astype(bf16)
    vt = vt_ref[...].reshape(A_WIDTH, w_buf).astype(bf16)
    s_buf = _bdot(lhs, kt) + bias_ref[:, 0:w_buf]
    s_new = _bdot_nt(lhs, k_new) + bias_ref[:, w_buf:]
    m = jnp.maximum(jnp.max(s_buf, axis=1, keepdims=True), jnp.max(s_new, axis=1, keepdims=True))
    p_buf = jnp.exp(s_buf - m)
    p_new = jnp.exp(s_new - m)
    den = jnp.sum(p_buf, axis=1, keepdims=True) + jnp.sum(p_new, axis=1, keepdims=True)
    o_all = (_bdot_nt(p_buf.astype(bf16), vt) + _bdot(p_new.astype(bf16), v_new)) / den
    o_all = jnp.where(own, o_all, 0.0)
    out = o_all[0:q_len]
    for h in range(1, H_A):
        out = out + o_all[h * q_len:(h + 1) * q_len]
    o_ref[...] = out.astype(o_ref.dtype)


def _attn_sample(qa, ka, va, cache_kt, cache_vt, layer, bias, q_len):
    n = qa.shape[0]
    batch = n // q_len
    w_buf = cache_kt.shape[-1]
    new = pl.BlockSpec((q_len, A_WIDTH), lambda b: (b, 0))
    buf = pl.BlockSpec((None, None, H_A, HD_A, w_buf), lambda b: (layer, b, 0, 0, 0))
    return pl.pallas_call(
        _attn_sample_body,
        grid=(batch,),
        in_specs=[new, new, new, buf, buf, _const_spec(bias.shape)],
        out_specs=new,
        out_shape=jax.ShapeDtypeStruct((n, A_WIDTH), jnp.bfloat16),
        compiler_params=_cparams(("parallel",)),
        name="attn_sample",
    )(qa, ka, va, cache_kt, cache_vt, bias)


RET_GROUP = 16


def _ret_sample_body(q_ref, k_ref, v_ref, g_ref, s_ref, dmask_ref, dq_ref, kdec_ref, sdec_ref,
                     o_ref, s_out_ref, *, q_len):
    rows = q_ref.shape[0]
    low = lax.broadcasted_iota(jnp.int32, (rows, LANES), 1) < DK_B
    top = lax.broadcasted_iota(jnp.int32, (2 * DK_B, DV_B), 0) < DK_B
    col = lax.broadcasted_iota(jnp.int32, (2 * DK_B, rows), 1)
    for pair in range(H_B // 2):
        lanes = slice(pair * LANES, (pair + 1) * LANES)
        q2 = q_ref[:, lanes]
        k2 = k_ref[:, lanes]
        kt = (k2 * kdec_ref[pair]).T
        zero = jnp.zeros_like(q2)
        upd = [[], []]
        for hh in range(2):
            h = 2 * pair + hh
            vl = slice(h * DV_B, (h + 1) * DV_B)
            qm = jnp.where(low, q2, zero) if hh == 0 else jnp.where(low, zero, q2)
            vh = v_ref[:, vl]
            a = _bdot_nt(qm, k2) * dmask_ref[h]
            cross = jnp.concatenate(
                [_bdot(qm[r * q_len:(r + 1) * q_len], s_ref[r, pair]) for r in range(RET_GROUP)], axis=0)
            o = _bdot(a, vh) + cross * dq_ref[h]
            o_ref[:, vl] = _norm_gate(o, g_ref[:, vl]).astype(o_ref.dtype)
            for r in range(RET_GROUP):
                upd[hh].append(_bdot(jnp.where((col >= r * q_len) & (col < (r + 1) * q_len), kt, 0.0), vh))
        for r in range(RET_GROUP):
            s_out_ref[r, pair] = s_ref[r, pair] * sdec_ref[pair] + jnp.where(top, upd[0][r], upd[1][r])


def _ret_sample(qr, kr, vb, gb, state, layer, q_len):
    n = qr.shape[0]
    batch = n // q_len
    rows = RET_GROUP * q_len
    dmask1, dq1, kdec1, sdec = _ret_tables(q_len)
    eye = jnp.eye(RET_GROUP, dtype=jnp.float32)
    dmask = jnp.einsum("rs,hij->hrisj", eye, dmask1).reshape(H_B, rows, rows)
    dq = jnp.tile(dq1, (1, RET_GROUP, 1))
    kdec = jnp.tile(kdec1, (1, RET_GROUP, 1))
    row = lambda w: pl.BlockSpec((rows, w), lambda i: (i, 0))
    st_in = pl.BlockSpec((None, RET_GROUP, H_B // 2, 2 * DK_B, DV_B), lambda i: (layer, i, 0, 0, 0))
    st_out = pl.BlockSpec((RET_GROUP, H_B // 2, 2 * DK_B, DV_B), lambda i: (i, 0, 0, 0))
    return pl.pallas_call(
        functools.partial(_ret_sample_body, q_len=q_len),
        grid=(batch // RET_GROUP,),
        in_specs=[row(RET_QK_WIDTH), row(RET_QK_WIDTH), row(RET_V_WIDTH), row(RET_V_WIDTH), st_in,
                  _const_spec(dmask.shape), _const_spec(dq.shape), _const_spec(kdec.shape),
                  _const_spec(sdec.shape)],
        out_specs=[row(RET_V_WIDTH), st_out],
        out_shape=[jax.ShapeDtypeStruct((n, RET_V_WIDTH), jnp.bfloat16),
                   jax.ShapeDtypeStruct((batch, H_B // 2, 2 * DK_B, DV_B), jnp.float32)],
        compiler_params=_cparams(("parallel",)),
        name="ret_sample",
    )(qr, kr, vb, gb, state, dmask, dq, kdec, sdec)


CONV_PAD = 32


def _glu(x, g, w1_ref, b1_ref):
    xn = _rms(x, g).astype(jnp.bfloat16)
    a = _bdot(xn, w1_ref[:, 0:D_MODEL]) + b1_ref[:, 0:D_MODEL]
    gate = _bdot(xn, w1_ref[:, D_MODEL:]) + b1_ref[:, D_MODEL:]
    return a / (1.0 + jnp.exp(-gate))


def _conv_tail(x, c, lng_ref, lnb_ref, w2_ref, b2_ref):
    mu = jnp.mean(c, axis=-1, keepdims=True)
    d = c - mu
    var = jnp.mean(d * d, axis=-1, keepdims=True)
    nrm = d * lax.rsqrt(var + LN_EPS) * lng_ref[...] + lnb_ref[...]
    return x + _bdot(_silu(nrm).astype(jnp.bfloat16), w2_ref[...]) + b2_ref[...]


def _conv_prompt_body(x_ref, g_ref, w1_ref, b1_ref, wdw_ref, bdw_ref, lng_ref, lnb_ref, w2_ref, b2_ref,
                      o_ref, buf_ref, ext, cbuf):
    tm = x_ref.shape[0]
    hist = CONV_W - 1

    @pl.when(pl.program_id(1) == 0)
    def _():
        ext[0:CONV_PAD, :] = jnp.zeros((CONV_PAD, D_MODEL), jnp.float32)

    x = x_ref[...]
    ext[CONV_PAD:, :] = _glu(x, g_ref[...], w1_ref, b1_ref)

    blk, lane_blk, off = 32, 256, CONV_PAD - hist

    def rows_block(i, carry):
        r0 = pl.multiple_of(i * blk, blk)
        for lb in range(D_MODEL // lane_blk):
            lanes = slice(lb * lane_blk, (lb + 1) * lane_blk)
            win = ext[pl.ds(r0, blk + CONV_PAD), lanes]
            acc = jnp.broadcast_to(bdw_ref[:, lanes], (blk, lane_blk))
            for r in range(SUBLANES):
                taps = [k for k in range(CONV_W) if (k + off) % SUBLANES == r]
                rows = blk if r == 0 else blk + SUBLANES
                y = None
                for k in taps:
                    a0 = k + off - r
                    tiles = win[a0:a0 + rows].reshape(rows // SUBLANES, SUBLANES, lane_blk)
                    term = (tiles * wdw_ref[k, :, lanes]).reshape(rows, lane_blk)
                    y = term if y is None else y + term
                acc = acc + y[r:r + blk]
            cbuf[pl.ds(r0, blk), lanes] = acc
        return carry
    lax.fori_loop(0, tm // blk, rows_block, 0)

    o_ref[...] = _conv_tail(x, cbuf[...], lng_ref, lnb_ref, w2_ref, b2_ref)
    buf_ref[...] = ext[tm + CONV_PAD - hist:, :]
    ext[0:CONV_PAD, :] = ext[tm:, :]


def _conv_prompt(x, g, w1, b1, wdw, bdw, lng, lnb, w2, b2, layer, batch, seq):
    tm = ROW_TILE
    nblk = seq // tm
    n = x.shape[0]
    row = pl.BlockSpec((tm, D_MODEL), lambda b, j: (b * nblk + j, 0))
    vec = lambda a: a.reshape(1, -1)
    f32 = jnp.float32
    wdw = jnp.broadcast_to(wdw[:, None, :], (CONV_W, SUBLANES, D_MODEL))
    return pl.pallas_call(
        _conv_prompt_body,
        grid=(batch, nblk),
        in_specs=[row, _const_spec((1, D_MODEL)), _stack_spec(w1, (layer,)), _const_spec((1, 2 * D_MODEL)),
                  _const_spec(wdw.shape), _const_spec((1, D_MODEL)), _const_spec((1, D_MODEL)),
                  _const_spec((1, D_MODEL)), _stack_spec(w2, (layer,)), _const_spec((1, D_MODEL))],
        out_specs=[row, pl.BlockSpec((None, CONV_W - 1, D_MODEL), lambda b, j: (b, 0, 0))],
        out_shape=[jax.ShapeDtypeStruct((n, D_MODEL), f32),
                   jax.ShapeDtypeStruct((batch, CONV_W - 1, D_MODEL), f32)],
        scratch_shapes=[pltpu.VMEM((tm + CONV_PAD, D_MODEL), f32), pltpu.VMEM((tm, D_MODEL), f32)],
        compiler_params=_cparams(("parallel", "arbitrary")),
        name="conv_prompt",
    )(x, vec(g), w1, vec(b1), wdw, vec(bdw), vec(lng), vec(lnb), w2, vec(b2))


CONV_GROUP = 16


def _conv_sample_body(x_ref, g_ref, w1_ref, b1_ref, wdw_ref, bdw_ref, lng_ref, lnb_ref, w2_ref, b2_ref,
                      st_ref, o_ref, buf_ref, ext, cbuf, *, q_len):
    hist = CONV_W - 1
    x = x_ref[...]
    glu = _glu(x, g_ref[...], w1_ref, b1_ref)
    for r in range(CONV_GROUP):
        ext[r, 0:hist, :] = st_ref[r]
        ext[r, hist:hist + q_len, :] = glu[r * q_len:(r + 1) * q_len]
    for r in range(CONV_GROUP):
        acc = jnp.broadcast_to(bdw_ref[...], (q_len, D_MODEL))
        for k in range(CONV_W):
            acc = acc + ext[r, k:k + q_len, :] * wdw_ref[k:k + 1, :]
        cbuf[r * q_len:(r + 1) * q_len, :] = acc
        buf_ref[r] = ext[r, q_len:q_len + hist, :]
    o_ref[...] = _conv_tail(x, cbuf[...], lng_ref, lnb_ref, w2_ref, b2_ref)


def _conv_sample(x, g, w1, b1, wdw, bdw, lng, lnb, w2, b2, state, layer, q_len):
    n = x.shape[0]
    batch = n // q_len
    rows = CONV_GROUP * q_len
    hist = CONV_W - 1
    row = pl.BlockSpec((rows, D_MODEL), lambda i: (i, 0))
    vec = lambda a: a.reshape(1, -1)
    f32 = jnp.float32
    return pl.pallas_call(
        functools.partial(_conv_sample_body, q_len=q_len),
        grid=(batch // CONV_GROUP,),
        in_specs=[row, _const_spec((1, D_MODEL)), _stack_spec(w1, (layer,)), _const_spec((1, 2 * D_MODEL)),
                  _const_spec(wdw.shape), _const_spec((1, D_MODEL)), _const_spec((1, D_MODEL)),
                  _const_spec((1, D_MODEL)), _stack_spec(w2, (layer,)), _const_spec((1, D_MODEL)),
                  pl.BlockSpec((None, CONV_GROUP, hist, D_MODEL), lambda i: (layer, i, 0, 0))],
        out_specs=[row, pl.BlockSpec((CONV_GROUP, hist, D_MODEL), lambda i: (i, 0, 0))],
        out_shape=[jax.ShapeDtypeStruct((n, D_MODEL), f32),
                   jax.ShapeDtypeStruct((batch, hist, D_MODEL), f32)],
        scratch_shapes=[pltpu.VMEM((CONV_GROUP, hist + q_len + 2, D_MODEL), f32),
                        pltpu.VMEM((rows, D_MODEL), f32)],
        compiler_params=_cparams(("parallel",)),
        name="conv_sample",
    )(x, vec(g), w1, vec(b1), wdw, vec(bdw), vec(lng), vec(lnb), w2, vec(b2), state)


def _t5_bucket(dist):
    dist = np.asarray(dist).astype(np.int32)
    max_exact = T5_BUCKETS // 2
    log_ratio = np.log(np.maximum(dist, 1) / max_exact) / math.log(T5_MAX_DIST / max_exact)
    large = np.minimum(max_exact + (log_ratio * (T5_BUCKETS - max_exact)).astype(np.int32), T5_BUCKETS - 1)
    return np.where(dist < max_exact, dist, large).astype(np.int32)


def _rot_tables(pos):
    half = DK_B // 2
    ang = 1.0 / (10000.0 ** jnp.linspace(0.0, 1.0, half, dtype=jnp.float32))
    th = pos.astype(jnp.float32)[:, None] * ang[None, :]
    cos = jnp.repeat(jnp.cos(th), 2, axis=1)
    sin = jnp.repeat(jnp.sin(th), 2, axis=1) * jnp.tile(jnp.array([-1.0, 1.0], jnp.float32), half)[None, :]
    return jnp.tile(cos, (1, H_B)), jnp.tile(sin, (1, H_B))


def kernel(x_prompt, x_sample, cache_win_k, cache_win_v, state_ret, state_conv, t5_table, rms_g, final_g,
           w_ffn_gate, w_ffn_up, w_ffn_down, w_mix_in, w_mix_out, w_pw1, b_pw1, w_dw, b_dw, ln_g, ln_b,
           w_pw2, b_pw2):
    batch, seq, _ = x_prompt.shape
    dec_batch, dec_seq, _ = x_sample.shape
    depth = rms_g.shape[0]
    n_even = cache_win_k.shape[0]
    w_buf = cache_win_k.shape[2]
    keep = min(DIL_PATTERNS[-1][0], seq)
    bf16 = jnp.bfloat16

    xp = x_prompt.reshape(batch * seq, D_MODEL)
    xs = x_sample.reshape(dec_batch * dec_seq, D_MODEL)

    wg, wu, wd = w_ffn_gate.astype(bf16), w_ffn_up.astype(bf16), w_ffn_down.astype(bf16)
    w_in, w_out = w_mix_in.astype(bf16), w_mix_out.astype(bf16)
    w1, w2 = w_pw1.astype(bf16), w_pw2.astype(bf16)

    cos_p, sin_p = _rot_tables(jnp.arange(seq))
    cos_s, sin_s = _rot_tables(PAST_LEN + jnp.arange(ROW_TILE) % dec_seq)
    bias_p = _prompt_bias(t5_table)
    bias_s = _sample_bias(t5_table, w_buf, dec_seq)
    assert dec_seq & (dec_seq - 1) == 0 and dec_seq <= LANES
    ck = jnp.transpose(cache_win_k, (0, 1, 3, 4, 2))
    cv = jnp.transpose(cache_win_v, (0, 1, 3, 4, 2))
    st_ret = state_ret.reshape(n_even, dec_batch, H_B // 2, 2 * DK_B, DV_B)

    kp_l, vp_l, sp_l, cp_l, ks_l, vs_l, ss_l, cs_l = [], [], [], [], [], [], [], []
    for layer in range(depth):
        g = rms_g[layer]
        xp = _ffn(xp, g[0], wg, wu, wd, (layer, 0))
        xs = _ffn(xs, g[0], wg, wu, wd, (layer, 0))
        last = layer == depth - 1
        post = final_g if last else None
        ffn_b = (g[2], wg, wu, wd, (layer, 1))
        if layer % 2 == 0:
            e = layer // 2
            qa, ka, va, qr, kr, vb, gb = _proj(xp, g[1], w_in, e, cos_p, sin_p, bf16)
            oa = _attn_prompt(qa, ka, va, bias_p, batch, seq)
            ob, s_end = _ret_prompt(qr, kr, vb, gb, batch, seq)
            xp = _ffn(xp, *ffn_b, pre=(oa, ob, w_out, (e,)), post_g=post)
            kp_l.append(ka.reshape(batch, seq, A_WIDTH)[:, -keep:].reshape(batch, keep, H_A, HD_A))
            vp_l.append(va.reshape(batch, seq, A_WIDTH)[:, -keep:].reshape(batch, keep, H_A, HD_A))
            sp_l.append(s_end)

            qa, ka, va, qr, kr, vb, gb = _proj(xs, g[1], w_in, e, cos_s, sin_s, jnp.float32)
            oa = _attn_sample(qa, ka, va, ck, cv, e, bias_s, dec_seq)
            ob, s_new = _ret_sample(qr, kr, vb, gb, st_ret, e, dec_seq)
            xs = _ffn(xs, *ffn_b, pre=(oa, ob, w_out, (e,)), post_g=post)
            ks_l.append(ka.reshape(dec_batch, dec_seq, H_A, HD_A))
            vs_l.append(va.reshape(dec_batch, dec_seq, H_A, HD_A))
            ss_l.append(s_new.reshape(dec_batch, H_B, DK_B, DV_B))
        else:
            o = layer // 2
            conv = (g[1], w1, b_pw1[o], w_dw[o], b_dw[o], ln_g[o], ln_b[o], w2, b_pw2[o])
            xp, cp = _conv_prompt(xp, *conv, o, batch, seq)
            xs, cs = _conv_sample(xs, *conv, state_conv, o, dec_seq)
            cp_l.append(cp)
            cs_l.append(cs)
            xp = _ffn(xp, *ffn_b, post_g=post)
            xs = _ffn(xs, *ffn_b, post_g=post)

    stk = jnp.stack
    return (xp.reshape(batch, seq, D_MODEL), xs.reshape(dec_batch, dec_seq, D_MODEL),
            stk(kp_l), stk(vp_l), stk(sp_l), stk(cp_l), stk(ks_l), stk(vs_l), stk(ss_l), stk(cs_l))
```

```python
import functools
import math

import jax
import jax.numpy as jnp
import numpy as np
from jax import lax
from jax.experimental import pallas as pl
from jax.experimental.pallas import tpu as pltpu

D_MODEL = 1024
D_FF = 2816
H_A, HD_A = 8, 64
A_WIDTH = H_A * HD_A
DIL_PATTERNS = ((128, 1), (512, 4), (2048, 16))
H_B, DK_B, DV_B = 4, 64, 128
RET_QK_WIDTH = H_B * DK_B
RET_V_WIDTH = H_B * DV_B
RET_CHUNK = 128
PROJ_COLS = 3 * A_WIDTH + 2 * RET_QK_WIDTH + 2 * RET_V_WIDTH
CONV_W = 31
T5_BUCKETS = 32
T5_MAX_DIST = 2048
PAST_LEN = 2048
RMS_EPS = 1e-6
LN_EPS = 1e-5

LANES = 128
SUBLANES = 8
VMEM_LIMIT_BYTES = 56 * 1024 * 1024

NEG = -1e30
BAND_Q = 128
ATT_SPAN = 2048
RET_UNROLL = 4
ATT_UNROLL = 8
ROW_TILE = 512


def _cparams(sem):
    return pltpu.CompilerParams(dimension_semantics=sem, vmem_limit_bytes=VMEM_LIMIT_BYTES)


def _const_spec(shape):
    nd = len(shape)
    return pl.BlockSpec(shape, lambda *_: (0,) * nd, pipeline_mode=pl.Buffered(1))


def _stack_spec(stacked, idx):
    tail = stacked.shape[len(idx):]
    return pl.BlockSpec((None,) * len(idx) + tail, lambda *_: tuple(idx) + (0,) * len(tail),
                        pipeline_mode=pl.Buffered(1))


def _rms(x, g):
    y = x * lax.rsqrt(jnp.mean(x * x, axis=-1, keepdims=True) + RMS_EPS)
    return y * g


def _silu(x):
    return x / (1.0 + jnp.exp(-x))


def _bdot(a, b):
    return jnp.dot(a, b, preferred_element_type=jnp.float32)


def _bdot_nt(a, b):
    return lax.dot_general(a, b, (((1,), (1,)), ((), ())), preferred_element_type=jnp.float32)


def _ffn_body(*refs, pre, post, ff_chunk):
    it = iter(refs)
    x_ref = next(it)
    if pre:
        oa_ref, ob_ref, wo_ref = next(it), next(it), next(it)
    g_ref, wg_ref, wu_ref, wd_ref = next(it), next(it), next(it), next(it)
    if post:
        fg_ref = next(it)
    o_ref = next(it)
    h_scr = next(it)

    x = x_ref[...]
    if pre:
        oa = oa_ref[...].astype(jnp.bfloat16)
        x = x + _bdot(oa, wo_ref[0:A_WIDTH, :]) + _bdot(ob_ref[...], wo_ref[A_WIDTH:, :])
    xn = _rms(x, g_ref[...]).astype(jnp.bfloat16)
    for c in range(D_FF // ff_chunk):
        sl = slice(c * ff_chunk, (c + 1) * ff_chunk)
        gate = _bdot(xn, wg_ref[:, sl])
        up = _bdot(xn, wu_ref[:, sl])
        h_scr[:, sl] = (_silu(gate) * up).astype(jnp.bfloat16)
    y = x + 0.5 * _bdot(h_scr[...], wd_ref[...])
    if post:
        y = _rms(y, fg_ref[...])
    o_ref[...] = y


def _ffn(x, g, wg, wu, wd, idx, pre=None, post_g=None):
    n = x.shape[0]
    tm = ROW_TILE
    row = lambda w: pl.BlockSpec((tm, w), lambda i: (i, 0))
    args, specs = [x], [row(D_MODEL)]
    if pre is not None:
        oa, ob, wo, wo_idx = pre
        args += [oa, ob, wo]
        specs += [row(A_WIDTH), row(RET_V_WIDTH), _stack_spec(wo, wo_idx)]
    args += [g.reshape(1, D_MODEL), wg, wu, wd]
    specs += [_const_spec((1, D_MODEL)), _stack_spec(wg, idx), _stack_spec(wu, idx), _stack_spec(wd, idx)]
    if post_g is not None:
        args.append(post_g.reshape(1, D_MODEL))
        specs.append(_const_spec((1, D_MODEL)))
    body = functools.partial(_ffn_body, pre=pre is not None, post=post_g is not None, ff_chunk=D_FF // 2)
    return pl.pallas_call(
        body,
        grid=(n // tm,),
        in_specs=specs,
        out_specs=row(D_MODEL),
        out_shape=jax.ShapeDtypeStruct((n, D_MODEL), jnp.float32),
        scratch_shapes=[pltpu.VMEM((tm, D_FF), jnp.bfloat16)],
        compiler_params=_cparams(("parallel",)),
        name="ffn",
    )(*args)


def _rotate(x, cos, sin_signed):
    w = x.shape[-1]
    lane = lax.broadcasted_iota(jnp.int32, x.shape, 1)
    partner = jnp.where((lane & 1) == 0, pltpu.roll(x, w - 1, 1), pltpu.roll(x, 1, 1))
    return x * cos + partner * sin_signed


def _proj_body(x_ref, g_ref, w_ref, cos_ref, sin_ref,
               qa_ref, ka_ref, va_ref, qr_ref, kr_ref, vb_ref, gb_ref):
    xn = _rms(x_ref[...], g_ref[...]).astype(jnp.bfloat16)
    a, qk, v = A_WIDTH, RET_QK_WIDTH, RET_V_WIDTH
    seg = lambda lo, width: _bdot(xn, w_ref[:, lo:lo + width])
    qa_ref[...] = seg(0, a) * (HD_A ** -0.5)
    ka_ref[...] = seg(a, a)
    va_ref[...] = seg(2 * a, a)
    cos, sin = cos_ref[...], sin_ref[...]
    qr_ref[...] = _rotate(seg(3 * a, qk), cos, sin).astype(qr_ref.dtype)
    kr_ref[...] = (_rotate(seg(3 * a + qk, qk), cos, sin) * (DK_B ** -0.5)).astype(kr_ref.dtype)
    vb_ref[...] = seg(3 * a + 2 * qk, v).astype(vb_ref.dtype)
    gb_ref[...] = seg(3 * a + 2 * qk + v, v)


def _proj(x, g, w_in, layer, cos_tab, sin_tab, ret_dtype):
    n = x.shape[0]
    tm = ROW_TILE
    tab_blocks = cos_tab.shape[0] // tm
    row = lambda w: pl.BlockSpec((tm, w), lambda i: (i, 0))
    tab = pl.BlockSpec((tm, RET_QK_WIDTH), lambda i: (i % tab_blocks, 0))
    f32 = jnp.float32
    outs = [(A_WIDTH, f32), (A_WIDTH, f32), (A_WIDTH, f32), (RET_QK_WIDTH, ret_dtype),
            (RET_QK_WIDTH, ret_dtype), (RET_V_WIDTH, ret_dtype), (RET_V_WIDTH, f32)]
    return pl.pallas_call(
        _proj_body,
        grid=(n // tm,),
        in_specs=[row(D_MODEL), _const_spec((1, D_MODEL)), _stack_spec(w_in, (layer,)), tab, tab],
        out_specs=[row(w) for w, _ in outs],
        out_shape=[jax.ShapeDtypeStruct((n, w), dt) for w, dt in outs],
        compiler_params=_cparams(("parallel",)),
        name="mix_proj",
    )(x, g.reshape(1, D_MODEL), w_in, cos_tab, sin_tab)


def _band_unit(q, kb, vb, bias, low, prev=None):
    nq = q.shape[0]
    zero = jnp.zeros_like(q)
    lhs = jnp.concatenate([jnp.where(low, q, zero), jnp.where(low, zero, q)], axis=0)
    s = _bdot_nt(lhs.astype(jnp.bfloat16), kb.astype(jnp.bfloat16)) + bias
    m_cur = jnp.max(s, axis=1, keepdims=True)
    if prev is not None:
        m_pa, m_pb, l_pa, l_pb, acc_prev = prev
        m_prev = jnp.concatenate([m_pa, m_pb], axis=0)
        m_next = jnp.maximum(m_prev, m_cur)
        alpha = jnp.exp(m_prev - m_next)
    else:
        m_next = jnp.broadcast_to(m_cur, (2 * nq, LANES))
    prob = jnp.exp(s - jnp.concatenate([m_next, m_next], axis=1))
    l_next = jnp.sum(prob, axis=1, keepdims=True)
    pv = _bdot(prob.astype(jnp.bfloat16), vb.astype(jnp.bfloat16))
    acc_next = jnp.where(low, pv[:nq], pv[nq:])
    if prev is not None:
        l_next = alpha * jnp.concatenate([l_pa, l_pb], axis=0) + l_next
        acc_next = jnp.where(low, alpha[:nq], alpha[nq:]) * acc_prev + acc_next
    else:
        l_next = jnp.broadcast_to(l_next, (2 * nq, LANES))
    return m_next, l_next, acc_next


NEAR_PATTERNS = DIL_PATTERNS[:2]


def _attn_near_body(q_ref, kp_ref, kc_ref, vp_ref, vc_ref, bias_ref, o_ref, lse_ref,
                    kk, vv, m_a, m_b, l_a, l_b, acc):
    first = (pl.program_id(2) == 0).astype(jnp.int32)
    span_rows = ATT_SPAN
    kk[0:span_rows, :] = kp_ref[...]
    kk[span_rows:, :] = kc_ref[...]
    vv[0:span_rows, :] = vp_ref[...]
    vv[span_rows:, :] = vc_ref[...]
    nq = BAND_Q
    low = lax.broadcasted_iota(jnp.int32, (nq, LANES), 1) < HD_A
    n_units = span_rows // nq

    for p, (_, dil) in enumerate(NEAR_PATTERNS):
        span = nq * dil
        shift = dil.bit_length() - 1
        init = p == 0

        def load(u, p=p, dil=dil, span=span, shift=shift, init=init):
            blk = u >> shift
            base = blk * span + (u & (dil - 1))
            rows = pl.ds(base, nq, stride=dil)
            band = pl.ds(span_rows + base - span, 2 * nq, stride=dil)
            flag = jnp.where(blk == 0, first, 0)
            vals = [q_ref[rows, :], kk[band, :], vv[band, :], bias_ref[p, flag]]
            prev = None if init else (m_a[rows, :], m_b[rows, :], l_a[rows, :], l_b[rows, :], acc[rows, :])
            return rows, vals, prev

        def group(g, carry, load=load):
            loaded = [load(g * ATT_UNROLL + j) for j in range(ATT_UNROLL)]
            results = [_band_unit(*vals, low, prev) for _, vals, prev in loaded]
            for (rows, _, _), (m_next, l_next, acc_next) in zip(loaded, results):
                acc[rows, :] = acc_next
                m_a[rows, :] = m_next[:nq]
                m_b[rows, :] = m_next[nq:]
                l_a[rows, :] = l_next[:nq]
                l_b[rows, :] = l_next[nq:]
            return carry

        lax.fori_loop(0, n_units // ATT_UNROLL, group, 0)

    low_all = lax.broadcasted_iota(jnp.int32, acc.shape, 1) < HD_A
    den = jnp.where(low_all, l_a[...], l_b[...])
    o_ref[...] = acc[...] / den
    lse_ref[...] = jnp.where(low_all, m_a[...], m_b[...]) + jnp.log(den)


FAR_DIL = DIL_PATTERNS[-1][1]
FAR_CLASSES = 2
FAR_STREAMS = 7


def _attn_far_body(q_hbm, k_hbm, v_hbm, near_hbm, lse_hbm, bias_ref, o_hbm,
                   qbuf, kbuf, vbuf, nearbuf, lsebuf, obuf, in_sem, out_sem, *, nblk):
    step = pl.program_id(0)
    n_steps = pl.num_programs(0)
    nq = BAND_Q
    per_span = FAR_DIL // FAR_CLASSES

    def locate(s):
        span = s // per_span
        first = (span % nblk) == 0
        return span, jnp.where(first, span, span - 1), (s % per_span) * FAR_CLASSES, first

    def in_copies(s, slot):
        span, prev, r0, _ = locate(s)
        copies = []
        for c in range(FAR_CLASSES):
            r = r0 + c
            pairs = [(q_hbm.at[span, :, r, :], qbuf.at[slot, c]),
                     (k_hbm.at[prev, :, r, :], kbuf.at[slot, c, pl.ds(0, nq), :]),
                     (k_hbm.at[span, :, r, :], kbuf.at[slot, c, pl.ds(nq, nq), :]),
                     (v_hbm.at[prev, :, r, :], vbuf.at[slot, c, pl.ds(0, nq), :]),
                     (v_hbm.at[span, :, r, :], vbuf.at[slot, c, pl.ds(nq, nq), :]),
                     (near_hbm.at[span, :, r, :], nearbuf.at[slot, c]),
                     (lse_hbm.at[span, :, r, :], lsebuf.at[slot, c])]
            copies += [pltpu.make_async_copy(src, dst, in_sem.at[slot, c, i])
                       for i, (src, dst) in enumerate(pairs)]
        return copies

    def out_copies(s, slot):
        span, _, r0, _ = locate(s)
        return [pltpu.make_async_copy(obuf.at[slot, c], o_hbm.at[span, :, r0 + c, :], out_sem.at[slot, c])
                for c in range(FAR_CLASSES)]

    slot = step % 2

    @pl.when(step == 0)
    def _():
        for cp in in_copies(0, 0):
            cp.start()

    @pl.when(step + 1 < n_steps)
    def _():
        for cp in in_copies(step + 1, 1 - slot):
            cp.start()

    for cp in in_copies(step, slot):
        cp.wait()

    @pl.when(step >= 2)
    def _():
        for cp in out_copies(step - 2, slot):
            cp.wait()

    first = locate(step)[3].astype(jnp.int32)
    low = lax.broadcasted_iota(jnp.int32, (nq, LANES), 1) < HD_A
    for c in range(FAR_CLASSES):
        for pair in range(H_A // 2):
            lanes = slice(pair * LANES, (pair + 1) * LANES)
            m_far, l_far, acc_far = _band_unit(qbuf[slot, c, :, lanes], kbuf[slot, c, :, lanes],
                                               vbuf[slot, c, :, lanes], bias_ref[pair, first], low)
            m_far = jnp.where(low, m_far[:nq], m_far[nq:])
            l_far = jnp.where(low, l_far[:nq], l_far[nq:])
            lse_near = lsebuf[slot, c, :, lanes]
            m_all = jnp.maximum(lse_near, m_far)
            w_near = jnp.exp(lse_near - m_all)
            w_far = jnp.exp(m_far - m_all)
            obuf[slot, c, :, lanes] = ((w_near * nearbuf[slot, c, :, lanes] + w_far * acc_far)
                                       / (w_near + w_far * l_far))

    for cp in out_copies(step, slot):
        cp.start()

    @pl.when(step == n_steps - 1)
    def _():
        for cp in out_copies(step, slot) + out_copies(step - 1, 1 - slot):
            cp.wait()


def _prompt_bias(t5_table):
    n = BAND_Q
    after_start = np.arange(2 * n)[None, None, :] >= n
    per_pattern = []
    for _, dil in DIL_PATTERNS:
        per_step = t5_table[_t5_bucket(np.arange(n + 1) * dil)].T.astype(jnp.float32)
        diag = jnp.concatenate([jnp.full((H_A, n - 1), NEG, jnp.float32), per_step[:, ::-1],
                                jnp.full((H_A, n), NEG, jnp.float32)], axis=1)
        skew = jnp.broadcast_to(diag[:, None, :], (H_A, n, 3 * n)).reshape(H_A, -1)
        skew = skew[:, :n * (3 * n - 1)].reshape(H_A, n, 3 * n - 1)
        full = skew[:, :, n - 1:3 * n - 1]
        start = jnp.where(after_start, full, NEG)
        per_pattern.append(jnp.stack([full, start], axis=0))
    b = jnp.stack(per_pattern, axis=0)
    b = b.reshape(len(DIL_PATTERNS), 2, H_A // 2, 2 * n, 2 * n)
    return jnp.transpose(b, (2, 0, 1, 3, 4))


def _attn_prompt(qa, ka, va, bias, batch, seq):
    assert ATT_SPAN == BAND_Q * FAR_DIL and FAR_DIL % FAR_CLASSES == 0
    nblk = seq // ATT_SPAN
    n = qa.shape[0]
    cur = pl.BlockSpec((ATT_SPAN, LANES), lambda hp, b, j: (b * nblk + j, hp))
    prev = pl.BlockSpec((ATT_SPAN, LANES), lambda hp, b, j: (b * nblk + jnp.maximum(j - 1, 0), hp))
    n_near = len(NEAR_PATTERNS)
    bias_near, bias_far = bias[:, :n_near], bias[:, n_near]
    bias_spec = pl.BlockSpec((None,) + bias_near.shape[1:], lambda hp, b, j: (hp, 0, 0, 0, 0))
    f32 = jnp.float32
    o_near, lse_near = pl.pallas_call(
        _attn_near_body,
        grid=(H_A // 2, batch, nblk),
        in_specs=[cur, prev, cur, prev, cur, bias_spec],
        out_specs=[cur, cur],
        out_shape=[jax.ShapeDtypeStruct((n, A_WIDTH), f32)] * 2,
        scratch_shapes=[pltpu.VMEM((2 * ATT_SPAN, LANES), f32), pltpu.VMEM((2 * ATT_SPAN, LANES), f32)]
        + [pltpu.VMEM((ATT_SPAN, LANES), f32)] * 5,
        compiler_params=_cparams(("parallel", "parallel", "parallel")),
        name="attn_near",
    )(qa, ka, ka, va, va, bias_near)

    spans = batch * nblk
    view = lambda a: a.reshape(spans, BAND_Q, FAR_DIL, A_WIDTH)
    n_steps = spans * FAR_DIL // FAR_CLASSES
    assert n_steps >= 2
    hbm = pl.BlockSpec(memory_space=pl.ANY)
    slots = lambda rows: pltpu.VMEM((2, FAR_CLASSES, rows, A_WIDTH), f32)
    out = pl.pallas_call(
        functools.partial(_attn_far_body, nblk=nblk),
        grid=(n_steps,),
        in_specs=[hbm] * 5 + [_const_spec(bias_far.shape)],
        out_specs=hbm,
        out_shape=jax.ShapeDtypeStruct((spans, BAND_Q, FAR_DIL, A_WIDTH), f32),
        scratch_shapes=[slots(BAND_Q), slots(2 * BAND_Q), slots(2 * BAND_Q), slots(BAND_Q), slots(BAND_Q),
                        slots(BAND_Q), pltpu.SemaphoreType.DMA((2, FAR_CLASSES, FAR_STREAMS)),
                        pltpu.SemaphoreType.DMA((2, FAR_CLASSES))],
        compiler_params=_cparams(("arbitrary",)),
        name="attn_far",
    )(view(qa), view(ka), view(va), view(o_near), view(lse_near), bias_far)
    return out.reshape(n, A_WIDTH)


def _ret_tables(length):
    lg = jnp.log(1.0 - 2.0 ** (-5.0 - jnp.arange(H_B, dtype=jnp.float32)))
    i = jnp.arange(length, dtype=jnp.float32)
    diff = i[:, None] - i[None, :]
    dmask = jnp.where(diff >= 0, jnp.exp(jnp.maximum(diff, 0.0)[None] * lg[:, None, None]), 0.0)
    dq = jnp.exp((i[:, None] + 1.0) * lg[None, :])
    dq = jnp.broadcast_to(dq.T[:, :, None], (H_B, length, DV_B))
    kd = jnp.exp((length - 1.0 - i)[:, None] * lg[None, :])
    kdec = jnp.repeat(kd, DK_B, axis=1).reshape(length, H_B // 2, 2 * DK_B).transpose(1, 0, 2)
    sd = jnp.exp(length * lg)
    sdec = jnp.broadcast_to(jnp.repeat(sd, DK_B).reshape(H_B // 2, 2 * DK_B, 1), (H_B // 2, 2 * DK_B, DV_B))
    return dmask, dq, kdec, sdec


def _norm_gate(o, gate):
    on = o * lax.rsqrt(jnp.mean(o * o, axis=-1, keepdims=True) + RMS_EPS)
    return _silu(gate) * on


def _ret_prompt_body(q_ref, k_ref, v_ref, g_ref, dmask_ref, dq_ref, kdec_ref, sdec_ref,
                     o_ref, s_out_ref, state):
    step = pl.program_id(1)

    @pl.when(step == 0)
    def _():
        state[...] = jnp.zeros(state.shape, jnp.float32)

    c_len = RET_CHUNK
    low = lax.broadcasted_iota(jnp.int32, (c_len, LANES), 1) < DK_B
    top = lax.broadcasted_iota(jnp.int32, (2 * DK_B, DV_B), 0) < DK_B

    def chunk(c, s_pairs):
        rows = pl.ds(pl.multiple_of(c * c_len, c_len), c_len)
        s_next = []
        for pair in range(H_B // 2):
            lanes = slice(pair * LANES, (pair + 1) * LANES)
            q2 = q_ref[rows, lanes]
            k2 = k_ref[rows, lanes]
            s_old = s_pairs[pair]
            kt = (k2.astype(jnp.float32) * kdec_ref[pair]).T.astype(jnp.bfloat16)
            zero = jnp.zeros_like(q2)
            qm = jnp.concatenate([jnp.where(low, q2, zero), jnp.where(low, zero, q2)], axis=0)
            vl = slice(2 * pair * DV_B, 2 * (pair + 1) * DV_B)
            v2 = v_ref[rows, vl]
            vz = jnp.zeros((c_len, DV_B), v2.dtype)
            v_diag = jnp.concatenate([jnp.concatenate([v2[:, :DV_B], vz], axis=1),
                                      jnp.concatenate([vz, v2[:, DV_B:]], axis=1)], axis=0)
            a = (_bdot_nt(qm, k2) * dmask_ref[pair]).astype(jnp.bfloat16)
            intra = _bdot(jnp.concatenate([a[:c_len], a[c_len:]], axis=1), v_diag)
            cross = _bdot(qm, s_old.astype(jnp.bfloat16))
            o = intra + jnp.concatenate([cross[:c_len], cross[c_len:]], axis=1) * dq_ref[pair]
            gate = g_ref[rows, vl]
            o_ref[rows, vl] = jnp.concatenate(
                [_norm_gate(o[:, :DV_B], gate[:, :DV_B]), _norm_gate(o[:, DV_B:], gate[:, DV_B:])],
                axis=1).astype(o_ref.dtype)
            upd = _bdot(kt, v2)
            s_next.append(s_old * sdec_ref[pair] + jnp.where(top, upd[:, :DV_B], upd[:, DV_B:]))
        return s_next

    def group(gi, carry):
        s_pairs = [state[pair] for pair in range(H_B // 2)]
        for j in range(RET_UNROLL):
            s_pairs = chunk(gi * RET_UNROLL + j, s_pairs)
        for pair in range(H_B // 2):
            state[pair] = s_pairs[pair]
        return carry

    lax.fori_loop(0, q_ref.shape[0] // (c_len * RET_UNROLL), group, 0)

    @pl.when(step == pl.num_programs(1) - 1)
    def _():
        for h in range(H_B):
            s_out_ref[h] = state[h // 2, (h % 2) * DK_B:(h % 2 + 1) * DK_B, :]


def _ret_prompt(qr, kr, vb, gb, batch, seq):
    rows = 1024
    nblk = seq // rows
    n = qr.shape[0]
    dmask, dq, kdec, sdec = _ret_tables(RET_CHUNK)
    dmask = dmask.reshape(H_B // 2, 2 * RET_CHUNK, RET_CHUNK)
    dq = dq.reshape(H_B // 2, 2, RET_CHUNK, DV_B).transpose(0, 2, 1, 3).reshape(H_B // 2, RET_CHUNK, 2 * DV_B)
    row = lambda w: pl.BlockSpec((rows, w), lambda b, j: (b * nblk + j, 0))
    return pl.pallas_call(
        _ret_prompt_body,
        grid=(batch, nblk),
        in_specs=[row(RET_QK_WIDTH), row(RET_QK_WIDTH), row(RET_V_WIDTH), row(RET_V_WIDTH),
                  _const_spec(dmask.shape), _const_spec(dq.shape), _const_spec(kdec.shape),
                  _const_spec(sdec.shape)],
        out_specs=[row(RET_V_WIDTH),
                   pl.BlockSpec((None, H_B, DK_B, DV_B), lambda b, j: (b, 0, 0, 0))],
        out_shape=[jax.ShapeDtypeStruct((n, RET_V_WIDTH), jnp.bfloat16),
                   jax.ShapeDtypeStruct((batch, H_B, DK_B, DV_B), jnp.float32)],
        scratch_shapes=[pltpu.VMEM((H_B // 2, 2 * DK_B, DV_B), jnp.float32)],
        compiler_params=_cparams(("parallel", "arbitrary")),
        name="ret_prompt",
    )(qr, kr, vb, gb, dmask, dq, kdec, sdec)


def _sample_bias(t5_table, w_buf, q_len):
    dist = np.arange(w_buf + q_len)
    by_bucket = t5_table[_t5_bucket(dist)].T.astype(jnp.float32)
    merged = None
    for win, dil in DIL_PATTERNS:
        ok = (dist % dil == 0) & (dist // dil <= win // dil)
        b = jnp.where(ok[None], by_bucket, -jnp.inf)
        merged = b if merged is None else jnp.logaddexp(merged, b)
    merged = jnp.where(jnp.isfinite(merged), merged, NEG)
    rev = jnp.concatenate([jnp.full((H_A, q_len - 1), NEG, jnp.float32), merged], axis=1)[:, ::-1]
    per_query = jnp.stack([rev[:, q_len - 1 - i:q_len - 1 - i + w_buf + q_len] for i in range(q_len)], axis=1)
    pad = jnp.full((H_A, q_len, LANES - q_len), NEG, jnp.float32)
    return jnp.concatenate([per_query, pad], axis=2).reshape(H_A * q_len, w_buf + LANES)


def _attn_sample_body(q_ref, kn_ref, vn_ref, kt_ref, vt_ref, bias_ref, o_ref):
    q_len = q_ref.shape[0]
    w_buf = kt_ref.shape[-1]
    bf16 = jnp.bfloat16
    rows = H_A * q_len
    q_rep = jnp.concatenate([q_ref[...]] * H_A, axis=0)
    row_head = jnp.right_shift(lax.broadcasted_iota(jnp.int32, (rows, A_WIDTH), 0), q_len.bit_length() - 1)
    lane_head = jnp.right_shift(lax.broadcasted_iota(jnp.int32, (rows, A_WIDTH), 1), HD_A.bit_length() - 1)
    own = row_head == lane_head
    lhs = jnp.where(own, q_rep, 0.0).astype(bf16)
    padz = jnp.zeros((LANES - q_len, A_WIDTH), jnp.float32)
    k_new = jnp.concatenate([kn_ref[...], padz], axis=0).astype(bf16)
    v_new = jnp.concatenate([vn_ref[...], padz], axis=0).astype(bf16)
    kt = kt_ref[...].reshape(A_WIDTH, w_buf).astype(bf16)
    vt = vt_ref[...].reshape(A_WIDTH, w_buf).astype(bf16)
    s_buf = _bdot(lhs, kt) + bias_ref[:, 0:w_buf]
    s_new = _bdot_nt(lhs, k_new) + bias_ref[:, w_buf:]
    m = jnp.maximum(jnp.max(s_buf, axis=1, keepdims=True), jnp.max(s_new, axis=1, keepdims=True))
    p_buf = jnp.exp(s_buf - m)
    p_new = jnp.exp(s_new - m)
    den = jnp.sum(p_buf, axis=1, keepdims=True) + jnp.sum(p_new, axis=1, keepdims=True)
    o_all = (_bdot_nt(p_buf.astype(bf16), vt) + _bdot(p_new.astype(bf16), v_new)) / den
    o_all = jnp.where(own, o_all, 0.0)
    out = o_all[0:q_len]
    for h in range(1, H_A):
        out = out + o_all[h * q_len:(h + 1) * q_len]
    o_ref[...] = out.astype(o_ref.dtype)


def _attn_sample(qa, ka, va, cache_kt, cache_vt, layer, bias, q_len):
    n = qa.shape[0]
    batch = n // q_len
    w_buf = cache_kt.shape[-1]
    new = pl.BlockSpec((q_len, A_WIDTH), lambda b: (b, 0))
    buf = pl.BlockSpec((None, None, H_A, HD_A, w_buf), lambda b: (layer, b, 0, 0, 0))
    return pl.pallas_call(
        _attn_sample_body,
        grid=(batch,),
        in_specs=[new, new, new, buf, buf, _const_spec(bias.shape)],
        out_specs=new,
        out_shape=jax.ShapeDtypeStruct((n, A_WIDTH), jnp.bfloat16),
        compiler_params=_cparams(("parallel",)),
        name="attn_sample",
    )(qa, ka, va, cache_kt, cache_vt, bias)


RET_GROUP = 16


def _ret_sample_body(q_ref, k_ref, v_ref, g_ref, s_ref, dmask_ref, dq_ref, kdec_ref, sdec_ref,
                     o_ref, s_out_ref, *, q_len):
    rows = q_ref.shape[0]
    low = lax.broadcasted_iota(jnp.int32, (rows, LANES), 1) < DK_B
    top = lax.broadcasted_iota(jnp.int32, (2 * DK_B, DV_B), 0) < DK_B
    col = lax.broadcasted_iota(jnp.int32, (2 * DK_B, rows), 1)
    for pair in range(H_B // 2):
        lanes = slice(pair * LANES, (pair + 1) * LANES)
        q2 = q_ref[:, lanes]
        k2 = k_ref[:, lanes]
        kt = (k2 * kdec_ref[pair]).T
        zero = jnp.zeros_like(q2)
        upd = [[], []]
        for hh in range(2):
            h = 2 * pair + hh
            vl = slice(h * DV_B, (h + 1) * DV_B)
            qm = jnp.where(low, q2, zero) if hh == 0 else jnp.where(low, zero, q2)
            vh = v_ref[:, vl]
            a = _bdot_nt(qm, k2) * dmask_ref[h]
            cross = jnp.concatenate(
                [_bdot(qm[r * q_len:(r + 1) * q_len], s_ref[r, pair]) for r in range(RET_GROUP)], axis=0)
            o = _bdot(a, vh) + cross * dq_ref[h]
            o_ref[:, vl] = _norm_gate(o, g_ref[:, vl]).astype(o_ref.dtype)
            for r in range(RET_GROUP):
                upd[hh].append(_bdot(jnp.where((col >= r * q_len) & (col < (r + 1) * q_len), kt, 0.0), vh))
        for r in range(RET_GROUP):
            s_out_ref[r, pair] = s_ref[r, pair] * sdec_ref[pair] + jnp.where(top, upd[0][r], upd[1][r])


def _ret_sample(qr, kr, vb, gb, state, layer, q_len):
    n = qr.shape[0]
    batch = n // q_len
    rows = RET_GROUP * q_len
    dmask1, dq1, kdec1, sdec = _ret_tables(q_len)
    eye = jnp.eye(RET_GROUP, dtype=jnp.float32)
    dmask = jnp.einsum("rs,hij->hrisj", eye, dmask1).reshape(H_B, rows, rows)
    dq = jnp.tile(dq1, (1, RET_GROUP, 1))
    kdec = jnp.tile(kdec1, (1, RET_GROUP, 1))
    row = lambda w: pl.BlockSpec((rows, w), lambda i: (i, 0))
    st_in = pl.BlockSpec((None, RET_GROUP, H_B // 2, 2 * DK_B, DV_B), lambda i: (layer, i, 0, 0, 0))
    st_out = pl.BlockSpec((RET_GROUP, H_B // 2, 2 * DK_B, DV_B), lambda i: (i, 0, 0, 0))
    return pl.pallas_call(
        functools.partial(_ret_sample_body, q_len=q_len),
        grid=(batch // RET_GROUP,),
        in_specs=[row(RET_QK_WIDTH), row(RET_QK_WIDTH), row(RET_V_WIDTH), row(RET_V_WIDTH), st_in,
                  _const_spec(dmask.shape), _const_spec(dq.shape), _const_spec(kdec.shape),
                  _const_spec(sdec.shape)],
        out_specs=[row(RET_V_WIDTH), st_out],
        out_shape=[jax.ShapeDtypeStruct((n, RET_V_WIDTH), jnp.bfloat16),
                   jax.ShapeDtypeStruct((batch, H_B // 2, 2 * DK_B, DV_B), jnp.float32)],
        compiler_params=_cparams(("parallel",)),
        name="ret_sample",
    )(qr, kr, vb, gb, state, dmask, dq, kdec, sdec)


CONV_PAD = 32
CONV_SPLIT = 2


def _glu(x, g, w1_ref, b1_ref):
    xn = _rms(x, g).astype(jnp.bfloat16)
    a = _bdot(xn, w1_ref[:, 0:D_MODEL]) + b1_ref[:, 0:D_MODEL]
    gate = _bdot(xn, w1_ref[:, D_MODEL:]) + b1_ref[:, D_MODEL:]
    return a / (1.0 + jnp.exp(-gate))


def _conv_tail(x, c, lng_ref, lnb_ref, w2_ref, b2_ref):
    mu = jnp.mean(c, axis=-1, keepdims=True)
    d = c - mu
    var = jnp.mean(d * d, axis=-1, keepdims=True)
    nrm = d * lax.rsqrt(var + LN_EPS) * lng_ref[...] + lnb_ref[...]
    return x + _bdot(_silu(nrm).astype(jnp.bfloat16), w2_ref[...]) + b2_ref[...]


def _conv_prompt_body(x_ref, g_ref, w1_ref, b1_ref, wdw_ref, bdw_ref, lng_ref, lnb_ref, w2_ref, b2_ref,
                      o_ref, buf_ref, ext, cbuf):
    tm = x_ref.shape[0]
    hist = CONV_W - 1

    @pl.when(pl.program_id(1) == 0)
    def _():
        ext[0:CONV_PAD, :] = jnp.zeros((CONV_PAD, D_MODEL), jnp.float32)

    blk, lane_blk, off = 32, 256, CONV_PAD - hist

    def depthwise(lo, hi):
        for r0 in range(lo, hi, blk):
            for lb in range(D_MODEL // lane_blk):
                lanes = slice(lb * lane_blk, (lb + 1) * lane_blk)
                win = ext[r0:r0 + blk + CONV_PAD, lanes]
                acc = jnp.broadcast_to(bdw_ref[:, lanes], (blk, lane_blk))
                for r in range(SUBLANES):
                    taps = [k for k in range(CONV_W) if (k + off) % SUBLANES == r]
                    rows = blk if r == 0 else blk + SUBLANES
                    y = None
                    for k in taps:
                        a0 = k + off - r
                        tiles = win[a0:a0 + rows].reshape(rows // SUBLANES, SUBLANES, lane_blk)
                        term = (tiles * wdw_ref[k, :, lanes]).reshape(rows, lane_blk)
                        y = term if y is None else y + term
                    acc = acc + y[r:r + blk]
                cbuf[r0:r0 + blk, lanes] = acc

    part = tm // CONV_SPLIT
    g = g_ref[...]
    for h in range(CONV_SPLIT):
        rows = slice(h * part, (h + 1) * part)
        ext[CONV_PAD + h * part:CONV_PAD + (h + 1) * part, :] = _glu(x_ref[rows, :], g, w1_ref, b1_ref)
    for h in range(CONV_SPLIT):
        rows = slice(h * part, (h + 1) * part)
        depthwise(h * part, (h + 1) * part)
        o_ref[rows, :] = _conv_tail(x_ref[rows, :], cbuf[rows, :], lng_ref, lnb_ref, w2_ref, b2_ref)
    buf_ref[...] = ext[tm + CONV_PAD - hist:, :]
    ext[0:CONV_PAD, :] = ext[tm:, :]


def _conv_prompt(x, g, w1, b1, wdw, bdw, lng, lnb, w2, b2, layer, batch, seq):
    tm = ROW_TILE
    nblk = seq // tm
    n = x.shape[0]
    row = pl.BlockSpec((tm, D_MODEL), lambda b, j: (b * nblk + j, 0))
    vec = lambda a: a.reshape(1, -1)
    f32 = jnp.float32
    wdw = jnp.broadcast_to(wdw[:, None, :], (CONV_W, SUBLANES, D_MODEL))
    return pl.pallas_call(
        _conv_prompt_body,
        grid=(batch, nblk),
        in_specs=[row, _const_spec((1, D_MODEL)), _stack_spec(w1, (layer,)), _const_spec((1, 2 * D_MODEL)),
                  _const_spec(wdw.shape), _const_spec((1, D_MODEL)), _const_spec((1, D_MODEL)),
                  _const_spec((1, D_MODEL)), _stack_spec(w2, (layer,)), _const_spec((1, D_MODEL))],
        out_specs=[row, pl.BlockSpec((None, CONV_W - 1, D_MODEL), lambda b, j: (b, 0, 0))],
        out_shape=[jax.ShapeDtypeStruct((n, D_MODEL), f32),
                   jax.ShapeDtypeStruct((batch, CONV_W - 1, D_MODEL), f32)],
        scratch_shapes=[pltpu.VMEM((tm + CONV_PAD, D_MODEL), f32), pltpu.VMEM((tm, D_MODEL), f32)],
        compiler_params=_cparams(("parallel", "arbitrary")),
        name="conv_prompt",
    )(x, vec(g), w1, vec(b1), wdw, vec(bdw), vec(lng), vec(lnb), w2, vec(b2))


CONV_GROUP = 32


def _conv_sample_body(x_ref, g_ref, w1_ref, b1_ref, wdw_ref, bdw_ref, lng_ref, lnb_ref, w2_ref, b2_ref,
                      st_ref, o_ref, buf_ref, ext, cbuf, *, q_len):
    hist = CONV_W - 1
    x = x_ref[...]
    glu = _glu(x, g_ref[...], w1_ref, b1_ref)
    for r in range(CONV_GROUP):
        ext[r, 0:hist, :] = st_ref[r]
        ext[r, hist:hist + q_len, :] = glu[r * q_len:(r + 1) * q_len]
    for r in range(CONV_GROUP):
        acc = jnp.broadcast_to(bdw_ref[...], (q_len, D_MODEL))
        for k in range(CONV_W):
            acc = acc + ext[r, k:k + q_len, :] * wdw_ref[k:k + 1, :]
        cbuf[r * q_len:(r + 1) * q_len, :] = acc
        buf_ref[r] = ext[r, q_len:q_len + hist, :]
    o_ref[...] = _conv_tail(x, cbuf[...], lng_ref, lnb_ref, w2_ref, b2_ref)


def _conv_sample(x, g, w1, b1, wdw, bdw, lng, lnb, w2, b2, state, layer, q_len):
    n = x.shape[0]
    batch = n // q_len
    rows = CONV_GROUP * q_len
    hist = CONV_W - 1
    row = pl.BlockSpec((rows, D_MODEL), lambda i: (i, 0))
    vec = lambda a: a.reshape(1, -1)
    f32 = jnp.float32
    return pl.pallas_call(
        functools.partial(_conv_sample_body, q_len=q_len),
        grid=(batch // CONV_GROUP,),
        in_specs=[row, _const_spec((1, D_MODEL)), _stack_spec(w1, (layer,)), _const_spec((1, 2 * D_MODEL)),
                  _const_spec(wdw.shape), _const_spec((1, D_MODEL)), _const_spec((1, D_MODEL)),
                  _const_spec((1, D_MODEL)), _stack_spec(w2, (layer,)), _const_spec((1, D_MODEL)),
                  pl.BlockSpec((None, CONV_GROUP, hist, D_MODEL), lambda i: (layer, i, 0, 0))],
        out_specs=[row, pl.BlockSpec((CONV_GROUP, hist, D_MODEL), lambda i: (i, 0, 0))],
        out_shape=[jax.ShapeDtypeStruct((n, D_MODEL), f32),
                   jax.ShapeDtypeStruct((batch, hist, D_MODEL), f32)],
        scratch_shapes=[pltpu.VMEM((CONV_GROUP, hist + q_len + 2, D_MODEL), f32),
                        pltpu.VMEM((rows, D_MODEL), f32)],
        compiler_params=_cparams(("parallel",)),
        name="conv_sample",
    )(x, vec(g), w1, vec(b1), wdw, vec(bdw), vec(lng), vec(lnb), w2, vec(b2), state)


def _t5_bucket(dist):
    dist = np.asarray(dist).astype(np.int32)
    max_exact = T5_BUCKETS // 2
    log_ratio = np.log(np.maximum(dist, 1) / max_exact) / math.log(T5_MAX_DIST / max_exact)
    large = np.minimum(max_exact + (log_ratio * (T5_BUCKETS - max_exact)).astype(np.int32), T5_BUCKETS - 1)
    return np.where(dist < max_exact, dist, large).astype(np.int32)


def _rot_tables(pos):
    half = DK_B // 2
    ang = 1.0 / (10000.0 ** jnp.linspace(0.0, 1.0, half, dtype=jnp.float32))
    th = pos.astype(jnp.float32)[:, None] * ang[None, :]
    cos = jnp.repeat(jnp.cos(th), 2, axis=1)
    sin = jnp.repeat(jnp.sin(th), 2, axis=1) * jnp.tile(jnp.array([-1.0, 1.0], jnp.float32), half)[None, :]
    return jnp.tile(cos, (1, H_B)), jnp.tile(sin, (1, H_B))


def kernel(x_prompt, x_sample, cache_win_k, cache_win_v, state_ret, state_conv, t5_table, rms_g, final_g,
           w_ffn_gate, w_ffn_up, w_ffn_down, w_mix_in, w_mix_out, w_pw1, b_pw1, w_dw, b_dw, ln_g, ln_b,
           w_pw2, b_pw2):
    batch, seq, _ = x_prompt.shape
    dec_batch, dec_seq, _ = x_sample.shape
    depth = rms_g.shape[0]
    n_even = cache_win_k.shape[0]
    w_buf = cache_win_k.shape[2]
    keep = min(DIL_PATTERNS[-1][0], seq)
    bf16 = jnp.bfloat16

    xp = x_prompt.reshape(batch * seq, D_MODEL)
    xs = x_sample.reshape(dec_batch * dec_seq, D_MODEL)

    wg, wu, wd = w_ffn_gate.astype(bf16), w_ffn_up.astype(bf16), w_ffn_down.astype(bf16)
    w_in, w_out = w_mix_in.astype(bf16), w_mix_out.astype(bf16)
    w1, w2 = w_pw1.astype(bf16), w_pw2.astype(bf16)

    cos_p, sin_p = _rot_tables(jnp.arange(seq))
    cos_s, sin_s = _rot_tables(PAST_LEN + jnp.arange(ROW_TILE) % dec_seq)
    bias_p = _prompt_bias(t5_table)
    bias_s = _sample_bias(t5_table, w_buf, dec_seq)
    assert dec_seq & (dec_seq - 1) == 0 and dec_seq <= LANES
    ck = jnp.transpose(cache_win_k, (0, 1, 3, 4, 2))
    cv = jnp.transpose(cache_win_v, (0, 1, 3, 4, 2))
    st_ret = state_ret.reshape(n_even, dec_batch, H_B // 2, 2 * DK_B, DV_B)

    kp_l, vp_l, sp_l, cp_l, ks_l, vs_l, ss_l, cs_l = [], [], [], [], [], [], [], []
    for layer in range(depth):
        g = rms_g[layer]
        xp = _ffn(xp, g[0], wg, wu, wd, (layer, 0))
        xs = _ffn(xs, g[0], wg, wu, wd, (layer, 0))
        last = layer == depth - 1
        post = final_g if last else None
        ffn_b = (g[2], wg, wu, wd, (layer, 1))
        if layer % 2 == 0:
            e = layer // 2
            qa, ka, va, qr, kr, vb, gb = _proj(xp, g[1], w_in, e, cos_p, sin_p, bf16)
            oa = _attn_prompt(qa, ka, va, bias_p, batch, seq)
            ob, s_end = _ret_prompt(qr, kr, vb, gb, batch, seq)
            xp = _ffn(xp, *ffn_b, pre=(oa, ob, w_out, (e,)), post_g=post)
            kp_l.append(ka.reshape(batch, seq, A_WIDTH)[:, -keep:].reshape(batch, keep, H_A, HD_A))
            vp_l.append(va.reshape(batch, seq, A_WIDTH)[:, -keep:].reshape(batch, keep, H_A, HD_A))
            sp_l.append(s_end)

            qa, ka, va, qr, kr, vb, gb = _proj(xs, g[1], w_in, e, cos_s, sin_s, jnp.float32)
            oa = _attn_sample(qa, ka, va, ck, cv, e, bias_s, dec_seq)
            ob, s_new = _ret_sample(qr, kr, vb, gb, st_ret, e, dec_seq)
            xs = _ffn(xs, *ffn_b, pre=(oa, ob, w_out, (e,)), post_g=post)
            ks_l.append(ka.reshape(dec_batch, dec_seq, H_A, HD_A))
            vs_l.append(va.reshape(dec_batch, dec_seq, H_A, HD_A))
            ss_l.append(s_new.reshape(dec_batch, H_B, DK_B, DV_B))
        else:
            o = layer // 2
            conv = (g[1], w1, b_pw1[o], w_dw[o], b_dw[o], ln_g[o], ln_b[o], w2, b_pw2[o])
            xp, cp = _conv_prompt(xp, *conv, o, batch, seq)
            xs, cs = _conv_sample(xs, *conv, state_conv, o, dec_seq)
            cp_l.append(cp)
            cs_l.append(cs)
            xp = _ffn(xp, *ffn_b, post_g=post)
            xs = _ffn(xs, *ffn_b, post_g=post)

    stk = jnp.stack
    return (xp.reshape(batch, seq, D_MODEL), xs.reshape(dec_batch, dec_seq, D_MODEL),
            stk(kp_l), stk(vp_l), stk(sp_l), stk(cp_l), stk(ks_l), stk(vs_l), stk(ss_l), stk(cs_l))
```

```python
import functools
import math

import jax
import jax.numpy as jnp
import numpy as np
from jax import lax
from jax.experimental import pallas as pl
from jax.experimental.pallas import tpu as pltpu

D_MODEL = 1024
D_FF = 2816
H_A, HD_A = 8, 64
A_WIDTH = H_A * HD_A
DIL_PATTERNS = ((128, 1), (512, 4), (2048, 16))
H_B, DK_B, DV_B = 4, 64, 128
RET_QK_WIDTH = H_B * DK_B
RET_V_WIDTH = H_B * DV_B
RET_CHUNK = 128
PROJ_COLS = 3 * A_WIDTH + 2 * RET_QK_WIDTH + 2 * RET_V_WIDTH
CONV_W = 31
T5_BUCKETS = 32
T5_MAX_DIST = 2048
PAST_LEN = 2048
RMS_EPS = 1e-6
LN_EPS = 1e-5

LANES = 128
SUBLANES = 8
VMEM_LIMIT_BYTES = 56 * 1024 * 1024

NEG = -1e30
BAND_Q = 128
ATT_SPAN = 2048
RET_UNROLL = 4
ATT_UNROLL = 8
ROW_TILE = 512


def _cparams(sem):
    return pltpu.CompilerParams(dimension_semantics=sem, vmem_limit_bytes=VMEM_LIMIT_BYTES)


def _const_spec(shape):
    nd = len(shape)
    return pl.BlockSpec(shape, lambda *_: (0,) * nd, pipeline_mode=pl.Buffered(1))


def _stack_spec(stacked, idx):
    tail = stacked.shape[len(idx):]
    return pl.BlockSpec((None,) * len(idx) + tail, lambda *_: tuple(idx) + (0,) * len(tail),
                        pipeline_mode=pl.Buffered(1))


def _rms(x, g):
    y = x * lax.rsqrt(jnp.mean(x * x, axis=-1, keepdims=True) + RMS_EPS)
    return y * g


def _silu(x):
    return x / (1.0 + jnp.exp(-x))


def _bdot(a, b):
    return jnp.dot(a, b, preferred_element_type=jnp.float32)


def _bdot_nt(a, b):
    return lax.dot_general(a, b, (((1,), (1,)), ((), ())), preferred_element_type=jnp.float32)


def _ffn_body(*refs, pre, post, ff_chunk, rider_layer):
    it = iter(refs)
    x_ref = next(it)
    if pre:
        oa_ref, ob_ref, wo_ref = next(it), next(it), next(it)
    g_ref, wg_ref, wu_ref, wd_ref = next(it), next(it), next(it), next(it)
    if post:
        fg_ref = next(it)
    if rider_layer is not None:
        qs_ref, ks_ref, vs_ref, ck_hbm, cv_hbm, bias_ref = [next(it) for _ in range(6)]
    o_ref = next(it)
    if rider_layer is not None:
        oas_ref = next(it)
    h_scr = next(it)
    live = {}

    def phase_in():
        x = x_ref[...]
        if pre:
            oa = oa_ref[...].astype(jnp.bfloat16)
            x = x + _bdot(oa, wo_ref[0:A_WIDTH, :]) + _bdot(ob_ref[...], wo_ref[A_WIDTH:, :])
        live["x"] = x
        live["xn"] = _rms(x, g_ref[...]).astype(jnp.bfloat16)

    def phase_hidden(c):
        sl = slice(c * ff_chunk, (c + 1) * ff_chunk)
        gate = _bdot(live["xn"], wg_ref[:, sl])
        up = _bdot(live["xn"], wu_ref[:, sl])
        h_scr[:, sl] = (_silu(gate) * up).astype(jnp.bfloat16)

    def phase_out():
        y = live["x"] + 0.5 * _bdot(h_scr[...], wd_ref[...])
        if post:
            y = _rms(y, fg_ref[...])
        o_ref[...] = y

    phases = [phase_in] + [functools.partial(phase_hidden, c) for c in range(D_FF // ff_chunk)] + [phase_out]
    if rider_layer is None:
        for phase in phases:
            phase()
        return

    kbuf, vbuf, sem = next(it), next(it), next(it)
    per_step = len(phases)
    assert per_step % 2 == 0
    q_len = qs_ref.shape[0] // per_step
    step = pl.program_id(0)

    def fetch(req, slot):
        return [pltpu.make_async_copy(ck_hbm.at[rider_layer, req], kbuf.at[slot], sem.at[slot, 0]),
                pltpu.make_async_copy(cv_hbm.at[rider_layer, req], vbuf.at[slot], sem.at[slot, 1])]

    @pl.when(step == 0)
    def _():
        for cp in fetch(0, 0):
            cp.start()

    for r, phase in enumerate(phases):
        req, slot = step * per_step + r, r % 2
        for cp in fetch(req, slot):
            cp.wait()
        if r + 1 < per_step:
            for cp in fetch(req + 1, 1 - slot):
                cp.start()
        else:
            @pl.when(step + 1 < pl.num_programs(0))
            def _():
                for cp in fetch(req + 1, 1 - slot):
                    cp.start()
        rows = slice(r * q_len, (r + 1) * q_len)
        oas_ref[rows, :] = _sample_attention(qs_ref[rows, :], ks_ref[rows, :], vs_ref[rows, :],
                                             kbuf.at[slot], vbuf.at[slot], bias_ref)
        phase()


def _ffn(x, g, wg, wu, wd, idx, pre=None, post_g=None, rider=None):
    n = x.shape[0]
    tm = ROW_TILE
    n_steps = n // tm
    ff_chunk = D_FF // 2
    row = lambda w: pl.BlockSpec((tm, w), lambda i: (i, 0))
    args, specs = [x], [row(D_MODEL)]
    if pre is not None:
        oa, ob, wo, wo_idx = pre
        args += [oa, ob, wo]
        specs += [row(A_WIDTH), row(RET_V_WIDTH), _stack_spec(wo, wo_idx)]
    args += [g.reshape(1, D_MODEL), wg, wu, wd]
    specs += [_const_spec((1, D_MODEL)), _stack_spec(wg, idx), _stack_spec(wu, idx), _stack_spec(wd, idx)]
    if post_g is not None:
        args.append(post_g.reshape(1, D_MODEL))
        specs.append(_const_spec((1, D_MODEL)))
    out_specs, out_shape = row(D_MODEL), jax.ShapeDtypeStruct((n, D_MODEL), jnp.float32)
    scratch = [pltpu.VMEM((tm, D_FF), jnp.bfloat16)]
    semantics, rider_layer = ("parallel",), None
    if rider is not None:
        qs, ks, vs, cache_kt, cache_vt, rider_layer, bias_s, q_len = rider
        per_step = D_FF // ff_chunk + 2
        assert qs.shape[0] == n_steps * per_step * q_len
        new = pl.BlockSpec((per_step * q_len, A_WIDTH), lambda i: (i, 0))
        hbm = pl.BlockSpec(memory_space=pl.ANY)
        args += [qs, ks, vs, cache_kt, cache_vt, bias_s]
        specs += [new, new, new, hbm, hbm, _const_spec(bias_s.shape)]
        out_specs = [out_specs, new]
        out_shape = [out_shape, jax.ShapeDtypeStruct((qs.shape[0], A_WIDTH), jnp.float32)]
        slot = pltpu.VMEM((2,) + cache_kt.shape[2:], jnp.float32)
        scratch += [slot, slot, pltpu.SemaphoreType.DMA((2, 2))]
        semantics = ("arbitrary",)
    body = functools.partial(_ffn_body, pre=pre is not None, post=post_g is not None, ff_chunk=ff_chunk,
                             rider_layer=rider_layer)
    return pl.pallas_call(
        body,
        grid=(n_steps,),
        in_specs=specs,
        out_specs=out_specs,
        out_shape=out_shape,
        scratch_shapes=scratch,
        compiler_params=_cparams(semantics),
        name="ffn",
    )(*args)


def _rotate(x, cos, sin_signed):
    w = x.shape[-1]
    lane = lax.broadcasted_iota(jnp.int32, x.shape, 1)
    partner = jnp.where((lane & 1) == 0, pltpu.roll(x, w - 1, 1), pltpu.roll(x, 1, 1))
    return x * cos + partner * sin_signed


def _proj_body(x_ref, g_ref, w_ref, cos_ref, sin_ref,
               qa_ref, ka_ref, va_ref, qr_ref, kr_ref, vb_ref, gb_ref):
    xn = _rms(x_ref[...], g_ref[...]).astype(jnp.bfloat16)
    a, qk, v = A_WIDTH, RET_QK_WIDTH, RET_V_WIDTH
    seg = lambda lo, width: _bdot(xn, w_ref[:, lo:lo + width])
    qa_ref[...] = seg(0, a) * (HD_A ** -0.5)
    ka_ref[...] = seg(a, a)
    va_ref[...] = seg(2 * a, a)
    cos, sin = cos_ref[...], sin_ref[...]
    qr_ref[...] = _rotate(seg(3 * a, qk), cos, sin).astype(qr_ref.dtype)
    kr_ref[...] = (_rotate(seg(3 * a + qk, qk), cos, sin) * (DK_B ** -0.5)).astype(kr_ref.dtype)
    vb_ref[...] = seg(3 * a + 2 * qk, v).astype(vb_ref.dtype)
    gb_ref[...] = seg(3 * a + 2 * qk + v, v)


def _proj(x, g, w_in, layer, cos_tab, sin_tab, ret_dtype):
    n = x.shape[0]
    tm = ROW_TILE
    tab_blocks = cos_tab.shape[0] // tm
    row = lambda w: pl.BlockSpec((tm, w), lambda i: (i, 0))
    tab = pl.BlockSpec((tm, RET_QK_WIDTH), lambda i: (i % tab_blocks, 0))
    f32 = jnp.float32
    outs = [(A_WIDTH, f32), (A_WIDTH, f32), (A_WIDTH, f32), (RET_QK_WIDTH, ret_dtype),
            (RET_QK_WIDTH, ret_dtype), (RET_V_WIDTH, ret_dtype), (RET_V_WIDTH, f32)]
    return pl.pallas_call(
        _proj_body,
        grid=(n // tm,),
        in_specs=[row(D_MODEL), _const_spec((1, D_MODEL)), _stack_spec(w_in, (layer,)), tab, tab],
        out_specs=[row(w) for w, _ in outs],
        out_shape=[jax.ShapeDtypeStruct((n, w), dt) for w, dt in outs],
        compiler_params=_cparams(("parallel",)),
        name="mix_proj",
    )(x, g.reshape(1, D_MODEL), w_in, cos_tab, sin_tab)


def _band_unit(q, kb, vb, bias, low, prev=None):
    nq = q.shape[0]
    zero = jnp.zeros_like(q)
    lhs = jnp.concatenate([jnp.where(low, q, zero), jnp.where(low, zero, q)], axis=0)
    s = _bdot_nt(lhs.astype(jnp.bfloat16), kb.astype(jnp.bfloat16)) + bias
    m_cur = jnp.max(s, axis=1, keepdims=True)
    if prev is not None:
        m_pa, m_pb, l_pa, l_pb, acc_prev = prev
        m_prev = jnp.concatenate([m_pa, m_pb], axis=0)
        m_next = jnp.maximum(m_prev, m_cur)
        alpha = jnp.exp(m_prev - m_next)
    else:
        m_next = jnp.broadcast_to(m_cur, (2 * nq, LANES))
    prob = jnp.exp(s - jnp.concatenate([m_next, m_next], axis=1))
    l_next = jnp.sum(prob, axis=1, keepdims=True)
    pv = _bdot(prob.astype(jnp.bfloat16), vb.astype(jnp.bfloat16))
    acc_next = jnp.where(low, pv[:nq], pv[nq:])
    if prev is not None:
        l_next = alpha * jnp.concatenate([l_pa, l_pb], axis=0) + l_next
        acc_next = jnp.where(low, alpha[:nq], alpha[nq:]) * acc_prev + acc_next
    else:
        l_next = jnp.broadcast_to(l_next, (2 * nq, LANES))
    return m_next, l_next, acc_next


NEAR_PATTERNS = DIL_PATTERNS[:2]


def _attn_near_body(q_ref, kp_ref, kc_ref, vp_ref, vc_ref, bias_ref, o_ref, lse_ref,
                    kk, vv, m_a, m_b, l_a, l_b, acc):
    first = (pl.program_id(2) == 0).astype(jnp.int32)
    span_rows = ATT_SPAN
    kk[0:span_rows, :] = kp_ref[...]
    kk[span_rows:, :] = kc_ref[...]
    vv[0:span_rows, :] = vp_ref[...]
    vv[span_rows:, :] = vc_ref[...]
    nq = BAND_Q
    low = lax.broadcasted_iota(jnp.int32, (nq, LANES), 1) < HD_A
    n_units = span_rows // nq

    for p, (_, dil) in enumerate(NEAR_PATTERNS):
        span = nq * dil
        shift = dil.bit_length() - 1
        init = p == 0

        def load(u, p=p, dil=dil, span=span, shift=shift, init=init):
            blk = u >> shift
            base = blk * span + (u & (dil - 1))
            rows = pl.ds(base, nq, stride=dil)
            band = pl.ds(span_rows + base - span, 2 * nq, stride=dil)
            flag = jnp.where(blk == 0, first, 0)
            vals = [q_ref[rows, :], kk[band, :], vv[band, :], bias_ref[p, flag]]
            prev = None if init else (m_a[rows, :], m_b[rows, :], l_a[rows, :], l_b[rows, :], acc[rows, :])
            return rows, vals, prev

        def group(g, carry, load=load):
            loaded = [load(g * ATT_UNROLL + j) for j in range(ATT_UNROLL)]
            results = [_band_unit(*vals, low, prev) for _, vals, prev in loaded]
            for (rows, _, _), (m_next, l_next, acc_next) in zip(loaded, results):
                acc[rows, :] = acc_next
                m_a[rows, :] = m_next[:nq]
                m_b[rows, :] = m_next[nq:]
                l_a[rows, :] = l_next[:nq]
                l_b[rows, :] = l_next[nq:]
            return carry

        lax.fori_loop(0, n_units // ATT_UNROLL, group, 0)

    low_all = lax.broadcasted_iota(jnp.int32, acc.shape, 1) < HD_A
    den = jnp.where(low_all, l_a[...], l_b[...])
    o_ref[...] = acc[...] / den
    lse_ref[...] = jnp.where(low_all, m_a[...], m_b[...]) + jnp.log(den)


FAR_DIL = DIL_PATTERNS[-1][1]
FAR_CLASSES = 2
FAR_STREAMS = 7


def _attn_far_body(q_hbm, k_hbm, v_hbm, near_hbm, lse_hbm, bias_ref, o_hbm,
                   qbuf, kbuf, vbuf, nearbuf, lsebuf, obuf, in_sem, out_sem, *, nblk):
    step = pl.program_id(0)
    n_steps = pl.num_programs(0)
    nq = BAND_Q
    per_span = FAR_DIL // FAR_CLASSES

    def locate(s):
        span = s // per_span
        first = (span % nblk) == 0
        return span, jnp.where(first, span, span - 1), (s % per_span) * FAR_CLASSES, first

    def in_copies(s, slot):
        span, prev, r0, _ = locate(s)
        copies = []
        for c in range(FAR_CLASSES):
            r = r0 + c
            pairs = [(q_hbm.at[span, :, r, :], qbuf.at[slot, c]),
                     (k_hbm.at[prev, :, r, :], kbuf.at[slot, c, pl.ds(0, nq), :]),
                     (k_hbm.at[span, :, r, :], kbuf.at[slot, c, pl.ds(nq, nq), :]),
                     (v_hbm.at[prev, :, r, :], vbuf.at[slot, c, pl.ds(0, nq), :]),
                     (v_hbm.at[span, :, r, :], vbuf.at[slot, c, pl.ds(nq, nq), :]),
                     (near_hbm.at[span, :, r, :], nearbuf.at[slot, c]),
                     (lse_hbm.at[span, :, r, :], lsebuf.at[slot, c])]
            copies += [pltpu.make_async_copy(src, dst, in_sem.at[slot, c, i])
                       for i, (src, dst) in enumerate(pairs)]
        return copies

    def out_copies(s, slot):
        span, _, r0, _ = locate(s)
        return [pltpu.make_async_copy(obuf.at[slot, c], o_hbm.at[span, :, r0 + c, :], out_sem.at[slot, c])
                for c in range(FAR_CLASSES)]

    slot = step % 2

    @pl.when(step == 0)
    def _():
        for cp in in_copies(0, 0):
            cp.start()

    @pl.when(step + 1 < n_steps)
    def _():
        for cp in in_copies(step + 1, 1 - slot):
            cp.start()

    for cp in in_copies(step, slot):
        cp.wait()

    @pl.when(step >= 2)
    def _():
        for cp in out_copies(step - 2, slot):
            cp.wait()

    first = locate(step)[3].astype(jnp.int32)
    low = lax.broadcasted_iota(jnp.int32, (nq, LANES), 1) < HD_A
    for c in range(FAR_CLASSES):
        for pair in range(H_A // 2):
            lanes = slice(pair * LANES, (pair + 1) * LANES)
            m_far, l_far, acc_far = _band_unit(qbuf[slot, c, :, lanes], kbuf[slot, c, :, lanes],
                                               vbuf[slot, c, :, lanes], bias_ref[pair, first], low)
            m_far = jnp.where(low, m_far[:nq], m_far[nq:])
            l_far = jnp.where(low, l_far[:nq], l_far[nq:])
            lse_near = lsebuf[slot, c, :, lanes]
            m_all = jnp.maximum(lse_near, m_far)
            w_near = jnp.exp(lse_near - m_all)
            w_far = jnp.exp(m_far - m_all)
            obuf[slot, c, :, lanes] = ((w_near * nearbuf[slot, c, :, lanes] + w_far * acc_far)
                                       / (w_near + w_far * l_far))

    for cp in out_copies(step, slot):
        cp.start()

    @pl.when(step == n_steps - 1)
    def _():
        for cp in out_copies(step, slot) + out_copies(step - 1, 1 - slot):
            cp.wait()


def _prompt_bias(t5_table):
    n = BAND_Q
    after_start = np.arange(2 * n)[None, None, :] >= n
    per_pattern = []
    for _, dil in DIL_PATTERNS:
        per_step = t5_table[_t5_bucket(np.arange(n + 1) * dil)].T.astype(jnp.float32)
        diag = jnp.concatenate([jnp.full((H_A, n - 1), NEG, jnp.float32), per_step[:, ::-1],
                                jnp.full((H_A, n), NEG, jnp.float32)], axis=1)
        skew = jnp.broadcast_to(diag[:, None, :], (H_A, n, 3 * n)).reshape(H_A, -1)
        skew = skew[:, :n * (3 * n - 1)].reshape(H_A, n, 3 * n - 1)
        full = skew[:, :, n - 1:3 * n - 1]
        start = jnp.where(after_start, full, NEG)
        per_pattern.append(jnp.stack([full, start], axis=0))
    b = jnp.stack(per_pattern, axis=0)
    b = b.reshape(len(DIL_PATTERNS), 2, H_A // 2, 2 * n, 2 * n)
    return jnp.transpose(b, (2, 0, 1, 3, 4))


def _attn_prompt(qa, ka, va, bias, batch, seq):
    assert ATT_SPAN == BAND_Q * FAR_DIL and FAR_DIL % FAR_CLASSES == 0
    nblk = seq // ATT_SPAN
    n = qa.shape[0]
    cur = pl.BlockSpec((ATT_SPAN, LANES), lambda hp, b, j: (b * nblk + j, hp))
    prev = pl.BlockSpec((ATT_SPAN, LANES), lambda hp, b, j: (b * nblk + jnp.maximum(j - 1, 0), hp))
    n_near = len(NEAR_PATTERNS)
    bias_near, bias_far = bias[:, :n_near], bias[:, n_near]
    bias_spec = pl.BlockSpec((None,) + bias_near.shape[1:], lambda hp, b, j: (hp, 0, 0, 0, 0))
    f32 = jnp.float32
    o_near, lse_near = pl.pallas_call(
        _attn_near_body,
        grid=(H_A // 2, batch, nblk),
        in_specs=[cur, prev, cur, prev, cur, bias_spec],
        out_specs=[cur, cur],
        out_shape=[jax.ShapeDtypeStruct((n, A_WIDTH), f32)] * 2,
        scratch_shapes=[pltpu.VMEM((2 * ATT_SPAN, LANES), f32), pltpu.VMEM((2 * ATT_SPAN, LANES), f32)]
        + [pltpu.VMEM((ATT_SPAN, LANES), f32)] * 5,
        compiler_params=_cparams(("parallel", "parallel", "parallel")),
        name="attn_near",
    )(qa, ka, ka, va, va, bias_near)

    spans = batch * nblk
    view = lambda a: a.reshape(spans, BAND_Q, FAR_DIL, A_WIDTH)
    n_steps = spans * FAR_DIL // FAR_CLASSES
    assert n_steps >= 2
    hbm = pl.BlockSpec(memory_space=pl.ANY)
    slots = lambda rows: pltpu.VMEM((2, FAR_CLASSES, rows, A_WIDTH), f32)
    out = pl.pallas_call(
        functools.partial(_attn_far_body, nblk=nblk),
        grid=(n_steps,),
        in_specs=[hbm] * 5 + [_const_spec(bias_far.shape)],
        out_specs=hbm,
        out_shape=jax.ShapeDtypeStruct((spans, BAND_Q, FAR_DIL, A_WIDTH), f32),
        scratch_shapes=[slots(BAND_Q), slots(2 * BAND_Q), slots(2 * BAND_Q), slots(BAND_Q), slots(BAND_Q),
                        slots(BAND_Q), pltpu.SemaphoreType.DMA((2, FAR_CLASSES, FAR_STREAMS)),
                        pltpu.SemaphoreType.DMA((2, FAR_CLASSES))],
        compiler_params=_cparams(("arbitrary",)),
        name="attn_far",
    )(view(qa), view(ka), view(va), view(o_near), view(lse_near), bias_far)
    return out.reshape(n, A_WIDTH)


def _ret_tables(length):
    lg = jnp.log(1.0 - 2.0 ** (-5.0 - jnp.arange(H_B, dtype=jnp.float32)))
    i = jnp.arange(length, dtype=jnp.float32)
    diff = i[:, None] - i[None, :]
    dmask = jnp.where(diff >= 0, jnp.exp(jnp.maximum(diff, 0.0)[None] * lg[:, None, None]), 0.0)
    dq = jnp.exp((i[:, None] + 1.0) * lg[None, :])
    dq = jnp.broadcast_to(dq.T[:, :, None], (H_B, length, DV_B))
    kd = jnp.exp((length - 1.0 - i)[:, None] * lg[None, :])
    kdec = jnp.repeat(kd, DK_B, axis=1).reshape(length, H_B // 2, 2 * DK_B).transpose(1, 0, 2)
    sd = jnp.exp(length * lg)
    sdec = jnp.broadcast_to(jnp.repeat(sd, DK_B).reshape(H_B // 2, 2 * DK_B, 1), (H_B // 2, 2 * DK_B, DV_B))
    return dmask, dq, kdec, sdec


def _norm_gate(o, gate):
    on = o * lax.rsqrt(jnp.mean(o * o, axis=-1, keepdims=True) + RMS_EPS)
    return _silu(gate) * on


def _ret_prompt_body(q_ref, k_ref, v_ref, g_ref, dmask_ref, dq_ref, kdec_ref, sdec_ref,
                     o_ref, s_out_ref, state):
    step = pl.program_id(1)

    @pl.when(step == 0)
    def _():
        state[...] = jnp.zeros(state.shape, jnp.float32)

    c_len = RET_CHUNK
    low = lax.broadcasted_iota(jnp.int32, (c_len, LANES), 1) < DK_B
    top = lax.broadcasted_iota(jnp.int32, (2 * DK_B, DV_B), 0) < DK_B

    def chunk(c, s_pairs):
        rows = pl.ds(pl.multiple_of(c * c_len, c_len), c_len)
        s_next = []
        for pair in range(H_B // 2):
            lanes = slice(pair * LANES, (pair + 1) * LANES)
            q2 = q_ref[rows, lanes]
            k2 = k_ref[rows, lanes]
            s_old = s_pairs[pair]
            kt = (k2.astype(jnp.float32) * kdec_ref[pair]).T.astype(jnp.bfloat16)
            zero = jnp.zeros_like(q2)
            qm = jnp.concatenate([jnp.where(low, q2, zero), jnp.where(low, zero, q2)], axis=0)
            vl = slice(2 * pair * DV_B, 2 * (pair + 1) * DV_B)
            v2 = v_ref[rows, vl]
            vz = jnp.zeros((c_len, DV_B), v2.dtype)
            v_diag = jnp.concatenate([jnp.concatenate([v2[:, :DV_B], vz], axis=1),
                                      jnp.concatenate([vz, v2[:, DV_B:]], axis=1)], axis=0)
            a = (_bdot_nt(qm, k2) * dmask_ref[pair]).astype(jnp.bfloat16)
            intra = _bdot(jnp.concatenate([a[:c_len], a[c_len:]], axis=1), v_diag)
            cross = _bdot(qm, s_old.astype(jnp.bfloat16))
            o = intra + jnp.concatenate([cross[:c_len], cross[c_len:]], axis=1) * dq_ref[pair]
            gate = g_ref[rows, vl]
            o_ref[rows, vl] = jnp.concatenate(
                [_norm_gate(o[:, :DV_B], gate[:, :DV_B]), _norm_gate(o[:, DV_B:], gate[:, DV_B:])],
                axis=1).astype(o_ref.dtype)
            upd = _bdot(kt, v2)
            s_next.append(s_old * sdec_ref[pair] + jnp.where(top, upd[:, :DV_B], upd[:, DV_B:]))
        return s_next

    def group(gi, carry):
        s_pairs = [state[pair] for pair in range(H_B // 2)]
        for j in range(RET_UNROLL):
            s_pairs = chunk(gi * RET_UNROLL + j, s_pairs)
        for pair in range(H_B // 2):
            state[pair] = s_pairs[pair]
        return carry

    lax.fori_loop(0, q_ref.shape[0] // (c_len * RET_UNROLL), group, 0)

    @pl.when(step == pl.num_programs(1) - 1)
    def _():
        for h in range(H_B):
            s_out_ref[h] = state[h // 2, (h % 2) * DK_B:(h % 2 + 1) * DK_B, :]


def _ret_prompt(qr, kr, vb, gb, batch, seq):
    rows = 1024
    nblk = seq // rows
    n = qr.shape[0]
    dmask, dq, kdec, sdec = _ret_tables(RET_CHUNK)
    dmask = dmask.reshape(H_B // 2, 2 * RET_CHUNK, RET_CHUNK)
    dq = dq.reshape(H_B // 2, 2, RET_CHUNK, DV_B).transpose(0, 2, 1, 3).reshape(H_B // 2, RET_CHUNK, 2 * DV_B)
    row = lambda w: pl.BlockSpec((rows, w), lambda b, j: (b * nblk + j, 0))
    return pl.pallas_call(
        _ret_prompt_body,
        grid=(batch, nblk),
        in_specs=[row(RET_QK_WIDTH), row(RET_QK_WIDTH), row(RET_V_WIDTH), row(RET_V_WIDTH),
                  _const_spec(dmask.shape), _const_spec(dq.shape), _const_spec(kdec.shape),
                  _const_spec(sdec.shape)],
        out_specs=[row(RET_V_WIDTH),
                   pl.BlockSpec((None, H_B, DK_B, DV_B), lambda b, j: (b, 0, 0, 0))],
        out_shape=[jax.ShapeDtypeStruct((n, RET_V_WIDTH), jnp.bfloat16),
                   jax.ShapeDtypeStruct((batch, H_B, DK_B, DV_B), jnp.float32)],
        scratch_shapes=[pltpu.VMEM((H_B // 2, 2 * DK_B, DV_B), jnp.float32)],
        compiler_params=_cparams(("parallel", "arbitrary")),
        name="ret_prompt",
    )(qr, kr, vb, gb, dmask, dq, kdec, sdec)


def _sample_bias(t5_table, w_buf, q_len):
    dist = np.arange(w_buf + q_len)
    by_bucket = t5_table[_t5_bucket(dist)].T.astype(jnp.float32)
    merged = None
    for win, dil in DIL_PATTERNS:
        ok = (dist % dil == 0) & (dist // dil <= win // dil)
        b = jnp.where(ok[None], by_bucket, -jnp.inf)
        merged = b if merged is None else jnp.logaddexp(merged, b)
    merged = jnp.where(jnp.isfinite(merged), merged, NEG)
    rev = jnp.concatenate([jnp.full((H_A, q_len - 1), NEG, jnp.float32), merged], axis=1)[:, ::-1]
    per_query = jnp.stack([rev[:, q_len - 1 - i:q_len - 1 - i + w_buf + q_len] for i in range(q_len)], axis=1)
    pad = jnp.full((H_A, q_len, LANES - q_len), NEG, jnp.float32)
    return jnp.concatenate([per_query, pad], axis=2).reshape(H_A * q_len, w_buf + LANES)


def _sample_attention(q, k_new, v_new, kt_ref, vt_ref, bias_ref):
    q_len = q.shape[0]
    w_buf = kt_ref.shape[-1]
    bf16 = jnp.bfloat16
    rows = H_A * q_len
    q_rep = jnp.concatenate([q] * H_A, axis=0)
    row_head = jnp.right_shift(lax.broadcasted_iota(jnp.int32, (rows, A_WIDTH), 0), q_len.bit_length() - 1)
    lane_head = jnp.right_shift(lax.broadcasted_iota(jnp.int32, (rows, A_WIDTH), 1), HD_A.bit_length() - 1)
    own = row_head == lane_head
    lhs = jnp.where(own, q_rep, 0.0).astype(bf16)
    padz = jnp.zeros((LANES - q_len, A_WIDTH), jnp.float32)
    k_new = jnp.concatenate([k_new, padz], axis=0).astype(bf16)
    v_new = jnp.concatenate([v_new, padz], axis=0).astype(bf16)
    kt = kt_ref[...].reshape(A_WIDTH, w_buf).astype(bf16)
    vt = vt_ref[...].reshape(A_WIDTH, w_buf).astype(bf16)
    s_buf = _bdot(lhs, kt) + bias_ref[:, 0:w_buf]
    s_new = _bdot_nt(lhs, k_new) + bias_ref[:, w_buf:]
    m = jnp.maximum(jnp.max(s_buf, axis=1, keepdims=True), jnp.max(s_new, axis=1, keepdims=True))
    p_buf = jnp.exp(s_buf - m)
    p_new = jnp.exp(s_new - m)
    den = jnp.sum(p_buf, axis=1, keepdims=True) + jnp.sum(p_new, axis=1, keepdims=True)
    o_all = (_bdot_nt(p_buf.astype(bf16), vt) + _bdot(p_new.astype(bf16), v_new)) / den
    o_all = jnp.where(own, o_all, 0.0)
    out = o_all[0:q_len]
    for h in range(1, H_A):
        out = out + o_all[h * q_len:(h + 1) * q_len]
    return out


RET_GROUP = 16


def _ret_sample_body(q_ref, k_ref, v_ref, g_ref, s_ref, dmask_ref, dq_ref, kdec_ref, sdec_ref,
                     o_ref, s_out_ref, *, q_len):
    rows = q_ref.shape[0]
    low = lax.broadcasted_iota(jnp.int32, (rows, LANES), 1) < DK_B
    top = lax.broadcasted_iota(jnp.int32, (2 * DK_B, DV_B), 0) < DK_B
    col = lax.broadcasted_iota(jnp.int32, (2 * DK_B, rows), 1)
    for pair in range(H_B // 2):
        lanes = slice(pair * LANES, (pair + 1) * LANES)
        q2 = q_ref[:, lanes]
        k2 = k_ref[:, lanes]
        kt = (k2 * kdec_ref[pair]).T
        zero = jnp.zeros_like(q2)
        upd = [[], []]
        for hh in range(2):
            h = 2 * pair + hh
            vl = slice(h * DV_B, (h + 1) * DV_B)
            qm = jnp.where(low, q2, zero) if hh == 0 else jnp.where(low, zero, q2)
            vh = v_ref[:, vl]
            a = _bdot_nt(qm, k2) * dmask_ref[h]
            cross = jnp.concatenate(
                [_bdot(qm[r * q_len:(r + 1) * q_len], s_ref[r, pair]) for r in range(RET_GROUP)], axis=0)
            o = _bdot(a, vh) + cross * dq_ref[h]
            o_ref[:, vl] = _norm_gate(o, g_ref[:, vl]).astype(o_ref.dtype)
            for r in range(RET_GROUP):
                upd[hh].append(_bdot(jnp.where((col >= r * q_len) & (col < (r + 1) * q_len), kt, 0.0), vh))
        for r in range(RET_GROUP):
            s_out_ref[r, pair] = s_ref[r, pair] * sdec_ref[pair] + jnp.where(top, upd[0][r], upd[1][r])


def _ret_sample(qr, kr, vb, gb, state, layer, q_len):
    n = qr.shape[0]
    batch = n // q_len
    rows = RET_GROUP * q_len
    dmask1, dq1, kdec1, sdec = _ret_tables(q_len)
    eye = jnp.eye(RET_GROUP, dtype=jnp.float32)
    dmask = jnp.einsum("rs,hij->hrisj", eye, dmask1).reshape(H_B, rows, rows)
    dq = jnp.tile(dq1, (1, RET_GROUP, 1))
    kdec = jnp.tile(kdec1, (1, RET_GROUP, 1))
    row = lambda w: pl.BlockSpec((rows, w), lambda i: (i, 0))
    st_in = pl.BlockSpec((None, RET_GROUP, H_B // 2, 2 * DK_B, DV_B), lambda i: (layer, i, 0, 0, 0))
    st_out = pl.BlockSpec((RET_GROUP, H_B // 2, 2 * DK_B, DV_B), lambda i: (i, 0, 0, 0))
    return pl.pallas_call(
        functools.partial(_ret_sample_body, q_len=q_len),
        grid=(batch // RET_GROUP,),
        in_specs=[row(RET_QK_WIDTH), row(RET_QK_WIDTH), row(RET_V_WIDTH), row(RET_V_WIDTH), st_in,
                  _const_spec(dmask.shape), _const_spec(dq.shape), _const_spec(kdec.shape),
                  _const_spec(sdec.shape)],
        out_specs=[row(RET_V_WIDTH), st_out],
        out_shape=[jax.ShapeDtypeStruct((n, RET_V_WIDTH), jnp.bfloat16),
                   jax.ShapeDtypeStruct((batch, H_B // 2, 2 * DK_B, DV_B), jnp.float32)],
        compiler_params=_cparams(("parallel",)),
        name="ret_sample",
    )(qr, kr, vb, gb, state, dmask, dq, kdec, sdec)


CONV_PAD = 32
CONV_SPLIT = 2


def _glu(x, g, w1_ref, b1_ref):
    xn = _rms(x, g).astype(jnp.bfloat16)
    a = _bdot(xn, w1_ref[:, 0:D_MODEL]) + b1_ref[:, 0:D_MODEL]
    gate = _bdot(xn, w1_ref[:, D_MODEL:]) + b1_ref[:, D_MODEL:]
    return a / (1.0 + jnp.exp(-gate))


def _conv_tail(x, c, lng_ref, lnb_ref, w2_ref, b2_ref):
    mu = jnp.mean(c, axis=-1, keepdims=True)
    d = c - mu
    var = jnp.mean(d * d, axis=-1, keepdims=True)
    nrm = d * lax.rsqrt(var + LN_EPS) * lng_ref[...] + lnb_ref[...]
    return x + _bdot(_silu(nrm).astype(jnp.bfloat16), w2_ref[...]) + b2_ref[...]


def _conv_prompt_body(x_ref, g_ref, w1_ref, b1_ref, wdw_ref, bdw_ref, lng_ref, lnb_ref, w2_ref, b2_ref,
                      o_ref, buf_ref, ext, cbuf):
    tm = x_ref.shape[0]
    hist = CONV_W - 1

    @pl.when(pl.program_id(1) == 0)
    def _():
        ext[0:CONV_PAD, :] = jnp.zeros((CONV_PAD, D_MODEL), jnp.float32)

    blk, lane_blk, off = 32, 256, CONV_PAD - hist

    def depthwise(lo, hi):
        for r0 in range(lo, hi, blk):
            for lb in range(D_MODEL // lane_blk):
                lanes = slice(lb * lane_blk, (lb + 1) * lane_blk)
                win = ext[r0:r0 + blk + CONV_PAD, lanes]
                acc = jnp.broadcast_to(bdw_ref[:, lanes], (blk, lane_blk))
                for r in range(SUBLANES):
                    taps = [k for k in range(CONV_W) if (k + off) % SUBLANES == r]
                    rows = blk if r == 0 else blk + SUBLANES
                    y = None
                    for k in taps:
                        a0 = k + off - r
                        tiles = win[a0:a0 + rows].reshape(rows // SUBLANES, SUBLANES, lane_blk)
                        term = (tiles * wdw_ref[k, :, lanes]).reshape(rows, lane_blk)
                        y = term if y is None else y + term
                    acc = acc + y[r:r + blk]
                cbuf[r0:r0 + blk, lanes] = acc

    part = tm // CONV_SPLIT
    g = g_ref[...]
    for h in range(CONV_SPLIT):
        rows = slice(h * part, (h + 1) * part)
        ext[CONV_PAD + h * part:CONV_PAD + (h + 1) * part, :] = _glu(x_ref[rows, :], g, w1_ref, b1_ref)
    for h in range(CONV_SPLIT):
        rows = slice(h * part, (h + 1) * part)
        depthwise(h * part, (h + 1) * part)
        o_ref[rows, :] = _conv_tail(x_ref[rows, :], cbuf[rows, :], lng_ref, lnb_ref, w2_ref, b2_ref)
    buf_ref[...] = ext[tm + CONV_PAD - hist:, :]
    ext[0:CONV_PAD, :] = ext[tm:, :]


def _conv_prompt(x, g, w1, b1, wdw, bdw, lng, lnb, w2, b2, layer, batch, seq):
    tm = ROW_TILE
    nblk = seq // tm
    n = x.shape[0]
    row = pl.BlockSpec((tm, D_MODEL), lambda b, j: (b * nblk + j, 0))
    vec = lambda a: a.reshape(1, -1)
    f32 = jnp.float32
    wdw = jnp.broadcast_to(wdw[:, None, :], (CONV_W, SUBLANES, D_MODEL))
    return pl.pallas_call(
        _conv_prompt_body,
        grid=(batch, nblk),
        in_specs=[row, _const_spec((1, D_MODEL)), _stack_spec(w1, (layer,)), _const_spec((1, 2 * D_MODEL)),
                  _const_spec(wdw.shape), _const_spec((1, D_MODEL)), _const_spec((1, D_MODEL)),
                  _const_spec((1, D_MODEL)), _stack_spec(w2, (layer,)), _const_spec((1, D_MODEL))],
        out_specs=[row, pl.BlockSpec((None, CONV_W - 1, D_MODEL), lambda b, j: (b, 0, 0))],
        out_shape=[jax.ShapeDtypeStruct((n, D_MODEL), f32),
                   jax.ShapeDtypeStruct((batch, CONV_W - 1, D_MODEL), f32)],
        scratch_shapes=[pltpu.VMEM((tm + CONV_PAD, D_MODEL), f32), pltpu.VMEM((tm, D_MODEL), f32)],
        compiler_params=_cparams(("parallel", "arbitrary")),
        name="conv_prompt",
    )(x, vec(g), w1, vec(b1), wdw, vec(bdw), vec(lng), vec(lnb), w2, vec(b2))


CONV_GROUP = 32


def _conv_sample_body(x_ref, g_ref, w1_ref, b1_ref, wdw_ref, bdw_ref, lng_ref, lnb_ref, w2_ref, b2_ref,
                      st_ref, o_ref, buf_ref, ext, cbuf, *, q_len):
    hist = CONV_W - 1
    x = x_ref[...]
    glu = _glu(x, g_ref[...], w1_ref, b1_ref)
    for r in range(CONV_GROUP):
        ext[r, 0:hist, :] = st_ref[r]
        ext[r, hist:hist + q_len, :] = glu[r * q_len:(r + 1) * q_len]
    for r in range(CONV_GROUP):
        acc = jnp.broadcast_to(bdw_ref[...], (q_len, D_MODEL))
        for k in range(CONV_W):
            acc = acc + ext[r, k:k + q_len, :] * wdw_ref[k:k + 1, :]
        cbuf[r * q_len:(r + 1) * q_len, :] = acc
        buf_ref[r] = ext[r, q_len:q_len + hist, :]
    o_ref[...] = _conv_tail(x, cbuf[...], lng_ref, lnb_ref, w2_ref, b2_ref)


def _conv_sample(x, g, w1, b1, wdw, bdw, lng, lnb, w2, b2, state, layer, q_len):
    n = x.shape[0]
    batch = n // q_len
    rows = CONV_GROUP * q_len
    hist = CONV_W - 1
    row = pl.BlockSpec((rows, D_MODEL), lambda i: (i, 0))
    vec = lambda a: a.reshape(1, -1)
    f32 = jnp.float32
    return pl.pallas_call(
        functools.partial(_conv_sample_body, q_len=q_len),
        grid=(batch // CONV_GROUP,),
        in_specs=[row, _const_spec((1, D_MODEL)), _stack_spec(w1, (layer,)), _const_spec((1, 2 * D_MODEL)),
                  _const_spec(wdw.shape), _const_spec((1, D_MODEL)), _const_spec((1, D_MODEL)),
                  _const_spec((1, D_MODEL)), _stack_spec(w2, (layer,)), _const_spec((1, D_MODEL)),
                  pl.BlockSpec((None, CONV_GROUP, hist, D_MODEL), lambda i: (layer, i, 0, 0))],
        out_specs=[row, pl.BlockSpec((CONV_GROUP, hist, D_MODEL), lambda i: (i, 0, 0))],
        out_shape=[jax.ShapeDtypeStruct((n, D_MODEL), f32),
                   jax.ShapeDtypeStruct((batch, hist, D_MODEL), f32)],
        scratch_shapes=[pltpu.VMEM((CONV_GROUP, hist + q_len + 2, D_MODEL), f32),
                        pltpu.VMEM((rows, D_MODEL), f32)],
        compiler_params=_cparams(("parallel",)),
        name="conv_sample",
    )(x, vec(g), w1, vec(b1), wdw, vec(bdw), vec(lng), vec(lnb), w2, vec(b2), state)


def _t5_bucket(dist):
    dist = np.asarray(dist).astype(np.int32)
    max_exact = T5_BUCKETS // 2
    log_ratio = np.log(np.maximum(dist, 1) / max_exact) / math.log(T5_MAX_DIST / max_exact)
    large = np.minimum(max_exact + (log_ratio * (T5_BUCKETS - max_exact)).astype(np.int32), T5_BUCKETS - 1)
    return np.where(dist < max_exact, dist, large).astype(np.int32)


def _rot_tables(pos):
    half = DK_B // 2
    ang = 1.0 / (10000.0 ** jnp.linspace(0.0, 1.0, half, dtype=jnp.float32))
    th = pos.astype(jnp.float32)[:, None] * ang[None, :]
    cos = jnp.repeat(jnp.cos(th), 2, axis=1)
    sin = jnp.repeat(jnp.sin(th), 2, axis=1) * jnp.tile(jnp.array([-1.0, 1.0], jnp.float32), half)[None, :]
    return jnp.tile(cos, (1, H_B)), jnp.tile(sin, (1, H_B))


def kernel(x_prompt, x_sample, cache_win_k, cache_win_v, state_ret, state_conv, t5_table, rms_g, final_g,
           w_ffn_gate, w_ffn_up, w_ffn_down, w_mix_in, w_mix_out, w_pw1, b_pw1, w_dw, b_dw, ln_g, ln_b,
           w_pw2, b_pw2):
    batch, seq, _ = x_prompt.shape
    dec_batch, dec_seq, _ = x_sample.shape
    depth = rms_g.shape[0]
    n_even = cache_win_k.shape[0]
    w_buf = cache_win_k.shape[2]
    keep = min(DIL_PATTERNS[-1][0], seq)
    bf16 = jnp.bfloat16

    xp = x_prompt.reshape(batch * seq, D_MODEL)
    xs = x_sample.reshape(dec_batch * dec_seq, D_MODEL)

    wg, wu, wd = w_ffn_gate.astype(bf16), w_ffn_up.astype(bf16), w_ffn_down.astype(bf16)
    w_in, w_out = w_mix_in.astype(bf16), w_mix_out.astype(bf16)
    w1, w2 = w_pw1.astype(bf16), w_pw2.astype(bf16)

    cos_p, sin_p = _rot_tables(jnp.arange(seq))
    cos_s, sin_s = _rot_tables(PAST_LEN + jnp.arange(ROW_TILE) % dec_seq)
    bias_p = _prompt_bias(t5_table)
    bias_s = _sample_bias(t5_table, w_buf, dec_seq)
    assert dec_seq & (dec_seq - 1) == 0 and dec_seq <= LANES
    ck = jnp.transpose(cache_win_k, (0, 1, 3, 4, 2))
    cv = jnp.transpose(cache_win_v, (0, 1, 3, 4, 2))
    st_ret = state_ret.reshape(n_even, dec_batch, H_B // 2, 2 * DK_B, DV_B)

    kp_l, vp_l, sp_l, cp_l, ks_l, vs_l, ss_l, cs_l = [], [], [], [], [], [], [], []
    for layer in range(depth):
        g = rms_g[layer]
        xp = _ffn(xp, g[0], wg, wu, wd, (layer, 0))
        xs = _ffn(xs, g[0], wg, wu, wd, (layer, 0))
        last = layer == depth - 1
        post = final_g if last else None
        ffn_b = (g[2], wg, wu, wd, (layer, 1))
        if layer % 2 == 0:
            e = layer // 2
            qa, ka, va, qr, kr, vb, gb = _proj(xp, g[1], w_in, e, cos_p, sin_p, bf16)
            oa = _attn_prompt(qa, ka, va, bias_p, batch, seq)
            ob, s_end = _ret_prompt(qr, kr, vb, gb, batch, seq)
            kp_l.append(ka.reshape(batch, seq, A_WIDTH)[:, -keep:].reshape(batch, keep, H_A, HD_A))
            vp_l.append(va.reshape(batch, seq, A_WIDTH)[:, -keep:].reshape(batch, keep, H_A, HD_A))
            sp_l.append(s_end)

            qa, ka, va, qr, kr, vb, gb = _proj(xs, g[1], w_in, e, cos_s, sin_s, jnp.float32)
            xp, oa = _ffn(xp, *ffn_b, pre=(oa, ob, w_out, (e,)), post_g=post,
                          rider=(qa, ka, va, ck, cv, e, bias_s, dec_seq))
            ob, s_new = _ret_sample(qr, kr, vb, gb, st_ret, e, dec_seq)
            xs = _ffn(xs, *ffn_b, pre=(oa, ob, w_out, (e,)), post_g=post)
            ks_l.append(ka.reshape(dec_batch, dec_seq, H_A, HD_A))
            vs_l.append(va.reshape(dec_batch, dec_seq, H_A, HD_A))
            ss_l.append(s_new.reshape(dec_batch, H_B, DK_B, DV_B))
        else:
            o = layer // 2
            conv = (g[1], w1, b_pw1[o], w_dw[o], b_dw[o], ln_g[o], ln_b[o], w2, b_pw2[o])
            xp, cp = _conv_prompt(xp, *conv, o, batch, seq)
            xs, cs = _conv_sample(xs, *conv, state_conv, o, dec_seq)
            cp_l.append(cp)
            cs_l.append(cs)
            xp = _ffn(xp, *ffn_b, post_g=post)
            xs = _ffn(xs, *ffn_b, post_g=post)

    stk = jnp.stack
    return (xp.reshape(batch, seq, D_MODEL), xs.reshape(dec_batch, dec_seq, D_MODEL),
            stk(kp_l), stk(vp_l), stk(sp_l), stk(cp_l), stk(ks_l), stk(vs_l), stk(ss_l), stk(cs_l))
```

```python
import functools
import math

import jax
import jax.numpy as jnp
import numpy as np
from jax import lax
from jax.experimental import pallas as pl
from jax.experimental.pallas import tpu as pltpu

D_MODEL = 1024
D_FF = 2816
H_A, HD_A = 8, 64
A_WIDTH = H_A * HD_A
DIL_PATTERNS = ((128, 1), (512, 4), (2048, 16))
H_B, DK_B, DV_B = 4, 64, 128
RET_QK_WIDTH = H_B * DK_B
RET_V_WIDTH = H_B * DV_B
RET_CHUNK = 128
PROJ_COLS = 3 * A_WIDTH + 2 * RET_QK_WIDTH + 2 * RET_V_WIDTH
CONV_W = 31
T5_BUCKETS = 32
T5_MAX_DIST = 2048
PAST_LEN = 2048
RMS_EPS = 1e-6
LN_EPS = 1e-5

LANES = 128
SUBLANES = 8
VMEM_LIMIT_BYTES = 56 * 1024 * 1024

NEG = -1e30
BAND_Q = 128
ATT_SPAN = 2048
RIDER_PHASES = 4
RET_UNROLL = 4
ATT_UNROLL = 8
ROW_TILE = 512


def _cparams(sem):
    return pltpu.CompilerParams(dimension_semantics=sem, vmem_limit_bytes=VMEM_LIMIT_BYTES)


def _const_spec(shape):
    nd = len(shape)
    return pl.BlockSpec(shape, lambda *_: (0,) * nd, pipeline_mode=pl.Buffered(1))


def _stack_spec(stacked, idx):
    tail = stacked.shape[len(idx):]
    return pl.BlockSpec((None,) * len(idx) + tail, lambda *_: tuple(idx) + (0,) * len(tail),
                        pipeline_mode=pl.Buffered(1))


def _rms(x, g):
    y = x * lax.rsqrt(jnp.mean(x * x, axis=-1, keepdims=True) + RMS_EPS)
    return y * g


def _silu(x):
    return x / (1.0 + jnp.exp(-x))


def _bdot(a, b):
    return jnp.dot(a, b, preferred_element_type=jnp.float32)


def _bdot_nt(a, b):
    return lax.dot_general(a, b, (((1,), (1,)), ((), ())), preferred_element_type=jnp.float32)


def _ffn_body(*refs, pre, post, ff_chunk, rider_layer):
    it = iter(refs)
    x_ref = next(it)
    if pre:
        oa_ref, ob_ref, wo_ref = next(it), next(it), next(it)
    g_ref, wg_ref, wu_ref, wd_ref = next(it), next(it), next(it), next(it)
    if post:
        fg_ref = next(it)
    if rider_layer is not None:
        qs_ref, ks_ref, vs_ref, ck_hbm, cv_hbm, bias_ref = [next(it) for _ in range(6)]
    o_ref = next(it)
    if rider_layer is not None:
        oas_ref = next(it)
    h_scr = next(it)
    live = {}

    def phase_in():
        x = x_ref[...]
        if pre:
            oa = oa_ref[...].astype(jnp.bfloat16)
            x = x + _bdot(oa, wo_ref[0:A_WIDTH, :]) + _bdot(ob_ref[...], wo_ref[A_WIDTH:, :])
        live["x"] = x
        live["xn"] = _rms(x, g_ref[...]).astype(jnp.bfloat16)

    def hidden(lo, hi):
        def run():
            gate = _bdot(live["xn"], wg_ref[:, lo:hi])
            up = _bdot(live["xn"], wu_ref[:, lo:hi])
            h_scr[:, lo:hi] = (_silu(gate) * up).astype(jnp.bfloat16)
        return run

    def out(lo, hi):
        def run():
            y = live["x"][:, lo:hi] + 0.5 * _bdot(h_scr[...], wd_ref[:, lo:hi])
            if post:
                assert (lo, hi) == (0, D_MODEL)
                y = _rms(y, fg_ref[...])
            o_ref[:, lo:hi] = y
        return run

    if rider_layer is None:
        for op in [phase_in, hidden(0, ff_chunk), hidden(ff_chunk, D_FF), out(0, D_MODEL)]:
            op()
        return

    q1, q2, q3 = 5 * LANES, ff_chunk, ff_chunk + 5 * LANES
    half = D_MODEL // 2
    phases = [[phase_in, hidden(0, q1)], [hidden(q1, q2), hidden(q2, q3)],
              [hidden(q3, D_FF), out(0, half)], [out(half, D_MODEL)]]
    kbuf, vbuf, sem = next(it), next(it), next(it)
    per_step = len(phases)
    assert per_step == RIDER_PHASES and per_step % 2 == 0
    q_len = qs_ref.shape[0] // per_step
    step = pl.program_id(0)

    def fetch(req, slot):
        return [pltpu.make_async_copy(ck_hbm.at[rider_layer, req], kbuf.at[slot], sem.at[slot, 0]),
                pltpu.make_async_copy(cv_hbm.at[rider_layer, req], vbuf.at[slot], sem.at[slot, 1])]

    @pl.when(step == 0)
    def _():
        for cp in fetch(0, 0):
            cp.start()

    for r, phase in enumerate(phases):
        req, slot = step * per_step + r, r % 2
        for cp in fetch(req, slot):
            cp.wait()
        if r + 1 < per_step:
            for cp in fetch(req + 1, 1 - slot):
                cp.start()
        else:
            @pl.when(step + 1 < pl.num_programs(0))
            def _():
                for cp in fetch(req + 1, 1 - slot):
                    cp.start()
        rows = slice(r * q_len, (r + 1) * q_len)
        oas_ref[rows, :] = _sample_attention(qs_ref[rows, :], ks_ref[rows, :], vs_ref[rows, :],
                                             kbuf.at[slot], vbuf.at[slot], bias_ref)
        for op in phase:
            op()


def _ffn(x, g, wg, wu, wd, idx, pre=None, post_g=None, rider=None):
    n = x.shape[0]
    tm = ROW_TILE
    n_steps = n // tm
    ff_chunk = D_FF // 2
    row = lambda w: pl.BlockSpec((tm, w), lambda i: (i, 0))
    args, specs = [x], [row(D_MODEL)]
    if pre is not None:
        oa, ob, wo, wo_idx = pre
        args += [oa, ob, wo]
        specs += [row(A_WIDTH), row(RET_V_WIDTH), _stack_spec(wo, wo_idx)]
    args += [g.reshape(1, D_MODEL), wg, wu, wd]
    specs += [_const_spec((1, D_MODEL)), _stack_spec(wg, idx), _stack_spec(wu, idx), _stack_spec(wd, idx)]
    if post_g is not None:
        args.append(post_g.reshape(1, D_MODEL))
        specs.append(_const_spec((1, D_MODEL)))
    out_specs, out_shape = row(D_MODEL), jax.ShapeDtypeStruct((n, D_MODEL), jnp.float32)
    scratch = [pltpu.VMEM((tm, D_FF), jnp.bfloat16)]
    semantics, rider_layer = ("parallel",), None
    if rider is not None:
        qs, ks, vs, cache_kt, cache_vt, rider_layer, bias_s, q_len = rider
        per_step = RIDER_PHASES
        assert qs.shape[0] == n_steps * per_step * q_len
        new = pl.BlockSpec((per_step * q_len, A_WIDTH), lambda i: (i, 0))
        hbm = pl.BlockSpec(memory_space=pl.ANY)
        args += [qs, ks, vs, cache_kt, cache_vt, bias_s]
        specs += [new, new, new, hbm, hbm, _const_spec(bias_s.shape)]
        out_specs = [out_specs, new]
        out_shape = [out_shape, jax.ShapeDtypeStruct((qs.shape[0], A_WIDTH), jnp.float32)]
        slot = pltpu.VMEM((2,) + cache_kt.shape[2:], jnp.float32)
        scratch += [slot, slot, pltpu.SemaphoreType.DMA((2, 2))]
        semantics = ("arbitrary",)
    body = functools.partial(_ffn_body, pre=pre is not None, post=post_g is not None, ff_chunk=ff_chunk,
                             rider_layer=rider_layer)
    return pl.pallas_call(
        body,
        grid=(n_steps,),
        in_specs=specs,
        out_specs=out_specs,
        out_shape=out_shape,
        scratch_shapes=scratch,
        compiler_params=_cparams(semantics),
        name="ffn",
    )(*args)


def _rotate(x, cos, sin_signed):
    w = x.shape[-1]
    lane = lax.broadcasted_iota(jnp.int32, x.shape, 1)
    partner = jnp.where((lane & 1) == 0, pltpu.roll(x, w - 1, 1), pltpu.roll(x, 1, 1))
    return x * cos + partner * sin_signed


def _proj_body(x_ref, g_ref, w_ref, cos_ref, sin_ref,
               qa_ref, ka_ref, va_ref, qr_ref, kr_ref, vb_ref, gb_ref):
    xn = _rms(x_ref[...], g_ref[...]).astype(jnp.bfloat16)
    a, qk, v = A_WIDTH, RET_QK_WIDTH, RET_V_WIDTH
    seg = lambda lo, width: _bdot(xn, w_ref[:, lo:lo + width])
    qa_ref[...] = seg(0, a) * (HD_A ** -0.5)
    ka_ref[...] = seg(a, a)
    va_ref[...] = seg(2 * a, a)
    cos, sin = cos_ref[...], sin_ref[...]
    qr_ref[...] = _rotate(seg(3 * a, qk), cos, sin).astype(qr_ref.dtype)
    kr_ref[...] = (_rotate(seg(3 * a + qk, qk), cos, sin) * (DK_B ** -0.5)).astype(kr_ref.dtype)
    vb_ref[...] = seg(3 * a + 2 * qk, v).astype(vb_ref.dtype)
    gb_ref[...] = seg(3 * a + 2 * qk + v, v)


def _proj(x, g, w_in, layer, cos_tab, sin_tab, ret_dtype):
    n = x.shape[0]
    tm = ROW_TILE
    tab_blocks = cos_tab.shape[0] // tm
    row = lambda w: pl.BlockSpec((tm, w), lambda i: (i, 0))
    tab = pl.BlockSpec((tm, RET_QK_WIDTH), lambda i: (i % tab_blocks, 0))
    f32 = jnp.float32
    outs = [(A_WIDTH, f32), (A_WIDTH, f32), (A_WIDTH, f32), (RET_QK_WIDTH, ret_dtype),
            (RET_QK_WIDTH, ret_dtype), (RET_V_WIDTH, ret_dtype), (RET_V_WIDTH, f32)]
    return pl.pallas_call(
        _proj_body,
        grid=(n // tm,),
        in_specs=[row(D_MODEL), _const_spec((1, D_MODEL)), _stack_spec(w_in, (layer,)), tab, tab],
        out_specs=[row(w) for w, _ in outs],
        out_shape=[jax.ShapeDtypeStruct((n, w), dt) for w, dt in outs],
        compiler_params=_cparams(("parallel",)),
        name="mix_proj",
    )(x, g.reshape(1, D_MODEL), w_in, cos_tab, sin_tab)


def _band_unit(q, kb, vb, bias, low, prev=None):
    nq = q.shape[0]
    zero = jnp.zeros_like(q)
    lhs = jnp.concatenate([jnp.where(low, q, zero), jnp.where(low, zero, q)], axis=0)
    s = _bdot_nt(lhs.astype(jnp.bfloat16), kb.astype(jnp.bfloat16)) + bias
    m_cur = jnp.max(s, axis=1, keepdims=True)
    if prev is not None:
        m_pa, m_pb, l_pa, l_pb, acc_prev = prev
        m_prev = jnp.concatenate([m_pa, m_pb], axis=0)
        m_next = jnp.maximum(m_prev, m_cur)
        alpha = jnp.exp(m_prev - m_next)
    else:
        m_next = jnp.broadcast_to(m_cur, (2 * nq, LANES))
    prob = jnp.exp(s - jnp.concatenate([m_next, m_next], axis=1))
    l_next = jnp.sum(prob, axis=1, keepdims=True)
    pv = _bdot(prob.astype(jnp.bfloat16), vb.astype(jnp.bfloat16))
    acc_next = jnp.where(low, pv[:nq], pv[nq:])
    if prev is not None:
        l_next = alpha * jnp.concatenate([l_pa, l_pb], axis=0) + l_next
        acc_next = jnp.where(low, alpha[:nq], alpha[nq:]) * acc_prev + acc_next
    else:
        l_next = jnp.broadcast_to(l_next, (2 * nq, LANES))
    return m_next, l_next, acc_next


NEAR_PATTERNS = DIL_PATTERNS[:2]


def _attn_near_body(q_ref, kp_ref, kc_ref, vp_ref, vc_ref, bias_ref, o_ref, lse_ref,
                    kk, vv, m_a, m_b, l_a, l_b, acc):
    first = (pl.program_id(2) == 0).astype(jnp.int32)
    span_rows = ATT_SPAN
    kk[0:span_rows, :] = kp_ref[...]
    kk[span_rows:, :] = kc_ref[...]
    vv[0:span_rows, :] = vp_ref[...]
    vv[span_rows:, :] = vc_ref[...]
    nq = BAND_Q
    low = lax.broadcasted_iota(jnp.int32, (nq, LANES), 1) < HD_A
    n_units = span_rows // nq

    for p, (_, dil) in enumerate(NEAR_PATTERNS):
        span = nq * dil
        shift = dil.bit_length() - 1
        init = p == 0

        def load(u, p=p, dil=dil, span=span, shift=shift, init=init):
            blk = u >> shift
            base = blk * span + (u & (dil - 1))
            rows = pl.ds(base, nq, stride=dil)
            band = pl.ds(span_rows + base - span, 2 * nq, stride=dil)
            flag = jnp.where(blk == 0, first, 0)
            vals = [q_ref[rows, :], kk[band, :], vv[band, :], bias_ref[p, flag]]
            prev = None if init else (m_a[rows, :], m_b[rows, :], l_a[rows, :], l_b[rows, :], acc[rows, :])
            return rows, vals, prev

        def group(g, carry, load=load):
            loaded = [load(g * ATT_UNROLL + j) for j in range(ATT_UNROLL)]
            results = [_band_unit(*vals, low, prev) for _, vals, prev in loaded]
            for (rows, _, _), (m_next, l_next, acc_next) in zip(loaded, results):
                acc[rows, :] = acc_next
                m_a[rows, :] = m_next[:nq]
                m_b[rows, :] = m_next[nq:]
                l_a[rows, :] = l_next[:nq]
                l_b[rows, :] = l_next[nq:]
            return carry

        lax.fori_loop(0, n_units // ATT_UNROLL, group, 0)

    low_all = lax.broadcasted_iota(jnp.int32, acc.shape, 1) < HD_A
    den = jnp.where(low_all, l_a[...], l_b[...])
    o_ref[...] = acc[...] / den
    lse_ref[...] = jnp.where(low_all, m_a[...], m_b[...]) + jnp.log(den)


FAR_DIL = DIL_PATTERNS[-1][1]
FAR_CLASSES = 2
FAR_STREAMS = 7


def _attn_far_body(q_hbm, k_hbm, v_hbm, near_hbm, lse_hbm, bias_ref, o_hbm,
                   qbuf, kbuf, vbuf, nearbuf, lsebuf, obuf, in_sem, out_sem, *, nblk):
    step = pl.program_id(0)
    n_steps = pl.num_programs(0)
    nq = BAND_Q
    per_span = FAR_DIL // FAR_CLASSES

    def locate(s):
        span = s // per_span
        first = (span % nblk) == 0
        return span, jnp.where(first, span, span - 1), (s % per_span) * FAR_CLASSES, first

    def in_copies(s, slot):
        span, prev, r0, _ = locate(s)
        copies = []
        for c in range(FAR_CLASSES):
            r = r0 + c
            pairs = [(q_hbm.at[span, :, r, :], qbuf.at[slot, c]),
                     (k_hbm.at[prev, :, r, :], kbuf.at[slot, c, pl.ds(0, nq), :]),
                     (k_hbm.at[span, :, r, :], kbuf.at[slot, c, pl.ds(nq, nq), :]),
                     (v_hbm.at[prev, :, r, :], vbuf.at[slot, c, pl.ds(0, nq), :]),
                     (v_hbm.at[span, :, r, :], vbuf.at[slot, c, pl.ds(nq, nq), :]),
                     (near_hbm.at[span, :, r, :], nearbuf.at[slot, c]),
                     (lse_hbm.at[span, :, r, :], lsebuf.at[slot, c])]
            copies += [pltpu.make_async_copy(src, dst, in_sem.at[slot, c, i])
                       for i, (src, dst) in enumerate(pairs)]
        return copies

    def out_copies(s, slot):
        span, _, r0, _ = locate(s)
        return [pltpu.make_async_copy(obuf.at[slot, c], o_hbm.at[span, :, r0 + c, :], out_sem.at[slot, c])
                for c in range(FAR_CLASSES)]

    slot = step % 2

    @pl.when(step == 0)
    def _():
        for cp in in_copies(0, 0):
            cp.start()

    @pl.when(step + 1 < n_steps)
    def _():
        for cp in in_copies(step + 1, 1 - slot):
            cp.start()

    for cp in in_copies(step, slot):
        cp.wait()

    @pl.when(step >= 2)
    def _():
        for cp in out_copies(step - 2, slot):
            cp.wait()

    first = locate(step)[3].astype(jnp.int32)
    low = lax.broadcasted_iota(jnp.int32, (nq, LANES), 1) < HD_A
    for c in range(FAR_CLASSES):
        for pair in range(H_A // 2):
            lanes = slice(pair * LANES, (pair + 1) * LANES)
            m_far, l_far, acc_far = _band_unit(qbuf[slot, c, :, lanes], kbuf[slot, c, :, lanes],
                                               vbuf[slot, c, :, lanes], bias_ref[pair, first], low)
            m_far = jnp.where(low, m_far[:nq], m_far[nq:])
            l_far = jnp.where(low, l_far[:nq], l_far[nq:])
            lse_near = lsebuf[slot, c, :, lanes]
            m_all = jnp.maximum(lse_near, m_far)
            w_near = jnp.exp(lse_near - m_all)
            w_far = jnp.exp(m_far - m_all)
            obuf[slot, c, :, lanes] = ((w_near * nearbuf[slot, c, :, lanes] + w_far * acc_far)
                                       / (w_near + w_far * l_far))

    for cp in out_copies(step, slot):
        cp.start()

    @pl.when(step == n_steps - 1)
    def _():
        for cp in out_copies(step, slot) + out_copies(step - 1, 1 - slot):
            cp.wait()


def _prompt_bias(t5_table):
    n = BAND_Q
    after_start = np.arange(2 * n)[None, None, :] >= n
    per_pattern = []
    for _, dil in DIL_PATTERNS:
        per_step = t5_table[_t5_bucket(np.arange(n + 1) * dil)].T.astype(jnp.float32)
        diag = jnp.concatenate([jnp.full((H_A, n - 1), NEG, jnp.float32), per_step[:, ::-1],
                                jnp.full((H_A, n), NEG, jnp.float32)], axis=1)
        skew = jnp.broadcast_to(diag[:, None, :], (H_A, n, 3 * n)).reshape(H_A, -1)
        skew = skew[:, :n * (3 * n - 1)].reshape(H_A, n, 3 * n - 1)
        full = skew[:, :, n - 1:3 * n - 1]
        start = jnp.where(after_start, full, NEG)
        per_pattern.append(jnp.stack([full, start], axis=0))
    b = jnp.stack(per_pattern, axis=0)
    b = b.reshape(len(DIL_PATTERNS), 2, H_A // 2, 2 * n, 2 * n)
    return jnp.transpose(b, (2, 0, 1, 3, 4))


def _attn_prompt(qa, ka, va, bias, batch, seq):
    assert ATT_SPAN == BAND_Q * FAR_DIL and FAR_DIL % FAR_CLASSES == 0
    nblk = seq // ATT_SPAN
    n = qa.shape[0]
    cur = pl.BlockSpec((ATT_SPAN, LANES), lambda hp, b, j: (b * nblk + j, hp))
    prev = pl.BlockSpec((ATT_SPAN, LANES), lambda hp, b, j: (b * nblk + jnp.maximum(j - 1, 0), hp))
    n_near = len(NEAR_PATTERNS)
    bias_near, bias_far = bias[:, :n_near], bias[:, n_near]
    bias_spec = pl.BlockSpec((None,) + bias_near.shape[1:], lambda hp, b, j: (hp, 0, 0, 0, 0))
    f32 = jnp.float32
    o_near, lse_near = pl.pallas_call(
        _attn_near_body,
        grid=(H_A // 2, batch, nblk),
        in_specs=[cur, prev, cur, prev, cur, bias_spec],
        out_specs=[cur, cur],
        out_shape=[jax.ShapeDtypeStruct((n, A_WIDTH), f32)] * 2,
        scratch_shapes=[pltpu.VMEM((2 * ATT_SPAN, LANES), f32), pltpu.VMEM((2 * ATT_SPAN, LANES), f32)]
        + [pltpu.VMEM((ATT_SPAN, LANES), f32)] * 5,
        compiler_params=_cparams(("parallel", "parallel", "parallel")),
        name="attn_near",
    )(qa, ka, ka, va, va, bias_near)

    spans = batch * nblk
    view = lambda a: a.reshape(spans, BAND_Q, FAR_DIL, A_WIDTH)
    n_steps = spans * FAR_DIL // FAR_CLASSES
    assert n_steps >= 2
    hbm = pl.BlockSpec(memory_space=pl.ANY)
    slots = lambda rows: pltpu.VMEM((2, FAR_CLASSES, rows, A_WIDTH), f32)
    out = pl.pallas_call(
        functools.partial(_attn_far_body, nblk=nblk),
        grid=(n_steps,),
        in_specs=[hbm] * 5 + [_const_spec(bias_far.shape)],
        out_specs=hbm,
        out_shape=jax.ShapeDtypeStruct((spans, BAND_Q, FAR_DIL, A_WIDTH), f32),
        scratch_shapes=[slots(BAND_Q), slots(2 * BAND_Q), slots(2 * BAND_Q), slots(BAND_Q), slots(BAND_Q),
                        slots(BAND_Q), pltpu.SemaphoreType.DMA((2, FAR_CLASSES, FAR_STREAMS)),
                        pltpu.SemaphoreType.DMA((2, FAR_CLASSES))],
        compiler_params=_cparams(("arbitrary",)),
        name="attn_far",
    )(view(qa), view(ka), view(va), view(o_near), view(lse_near), bias_far)
    return out.reshape(n, A_WIDTH)


def _ret_tables(length):
    lg = jnp.log(1.0 - 2.0 ** (-5.0 - jnp.arange(H_B, dtype=jnp.float32)))
    i = jnp.arange(length, dtype=jnp.float32)
    diff = i[:, None] - i[None, :]
    dmask = jnp.where(diff >= 0, jnp.exp(jnp.maximum(diff, 0.0)[None] * lg[:, None, None]), 0.0)
    dq = jnp.exp((i[:, None] + 1.0) * lg[None, :])
    dq = jnp.broadcast_to(dq.T[:, :, None], (H_B, length, DV_B))
    kd = jnp.exp((length - 1.0 - i)[:, None] * lg[None, :])
    kdec = jnp.repeat(kd, DK_B, axis=1).reshape(length, H_B // 2, 2 * DK_B).transpose(1, 0, 2)
    sd = jnp.exp(length * lg)
    sdec = jnp.broadcast_to(jnp.repeat(sd, DK_B).reshape(H_B // 2, 2 * DK_B, 1), (H_B // 2, 2 * DK_B, DV_B))
    return dmask, dq, kdec, sdec


def _norm_gate(o, gate):
    on = o * lax.rsqrt(jnp.mean(o * o, axis=-1, keepdims=True) + RMS_EPS)
    return _silu(gate) * on


def _ret_prompt_body(q_ref, k_ref, v_ref, g_ref, dmask_ref, dq_ref, kdec_ref, sdec_ref,
                     o_ref, s_out_ref, state):
    step = pl.program_id(1)

    @pl.when(step == 0)
    def _():
        state[...] = jnp.zeros(state.shape, jnp.float32)

    c_len = RET_CHUNK
    low = lax.broadcasted_iota(jnp.int32, (c_len, LANES), 1) < DK_B
    top = lax.broadcasted_iota(jnp.int32, (2 * DK_B, DV_B), 0) < DK_B

    def chunk(c, s_pairs):
        rows = pl.ds(pl.multiple_of(c * c_len, c_len), c_len)
        s_next = []
        for pair in range(H_B // 2):
            lanes = slice(pair * LANES, (pair + 1) * LANES)
            q2 = q_ref[rows, lanes]
            k2 = k_ref[rows, lanes]
            s_old = s_pairs[pair]
            kt = (k2.astype(jnp.float32) * kdec_ref[pair]).T.astype(jnp.bfloat16)
            zero = jnp.zeros_like(q2)
            qm = jnp.concatenate([jnp.where(low, q2, zero), jnp.where(low, zero, q2)], axis=0)
            vl = slice(2 * pair * DV_B, 2 * (pair + 1) * DV_B)
            v2 = v_ref[rows, vl]
            vz = jnp.zeros((c_len, DV_B), v2.dtype)
            v_diag = jnp.concatenate([jnp.concatenate([v2[:, :DV_B], vz], axis=1),
                                      jnp.concatenate([vz, v2[:, DV_B:]], axis=1)], axis=0)
            a = (_bdot_nt(qm, k2) * dmask_ref[pair]).astype(jnp.bfloat16)
            intra = _bdot(jnp.concatenate([a[:c_len], a[c_len:]], axis=1), v_diag)
            cross = _bdot(qm, s_old.astype(jnp.bfloat16))
            o = intra + jnp.concatenate([cross[:c_len], cross[c_len:]], axis=1) * dq_ref[pair]
            gate = g_ref[rows, vl]
            o_ref[rows, vl] = jnp.concatenate(
                [_norm_gate(o[:, :DV_B], gate[:, :DV_B]), _norm_gate(o[:, DV_B:], gate[:, DV_B:])],
                axis=1).astype(o_ref.dtype)
            upd = _bdot(kt, v2)
            s_next.append(s_old * sdec_ref[pair] + jnp.where(top, upd[:, :DV_B], upd[:, DV_B:]))
        return s_next

    def group(gi, carry):
        s_pairs = [state[pair] for pair in range(H_B // 2)]
        for j in range(RET_UNROLL):
            s_pairs = chunk(gi * RET_UNROLL + j, s_pairs)
        for pair in range(H_B // 2):
            state[pair] = s_pairs[pair]
        return carry

    lax.fori_loop(0, q_ref.shape[0] // (c_len * RET_UNROLL), group, 0)

    @pl.when(step == pl.num_programs(1) - 1)
    def _():
        for h in range(H_B):
            s_out_ref[h] = state[h // 2, (h % 2) * DK_B:(h % 2 + 1) * DK_B, :]


def _ret_prompt(qr, kr, vb, gb, batch, seq):
    rows = 1024
    nblk = seq // rows
    n = qr.shape[0]
    dmask, dq, kdec, sdec = _ret_tables(RET_CHUNK)
    dmask = dmask.reshape(H_B // 2, 2 * RET_CHUNK, RET_CHUNK)
    dq = dq.reshape(H_B // 2, 2, RET_CHUNK, DV_B).transpose(0, 2, 1, 3).reshape(H_B // 2, RET_CHUNK, 2 * DV_B)
    row = lambda w: pl.BlockSpec((rows, w), lambda b, j: (b * nblk + j, 0))
    return pl.pallas_call(
        _ret_prompt_body,
        grid=(batch, nblk),
        in_specs=[row(RET_QK_WIDTH), row(RET_QK_WIDTH), row(RET_V_WIDTH), row(RET_V_WIDTH),
                  _const_spec(dmask.shape), _const_spec(dq.shape), _const_spec(kdec.shape),
                  _const_spec(sdec.shape)],
        out_specs=[row(RET_V_WIDTH),
                   pl.BlockSpec((None, H_B, DK_B, DV_B), lambda b, j: (b, 0, 0, 0))],
        out_shape=[jax.ShapeDtypeStruct((n, RET_V_WIDTH), jnp.bfloat16),
                   jax.ShapeDtypeStruct((batch, H_B, DK_B, DV_B), jnp.float32)],
        scratch_shapes=[pltpu.VMEM((H_B // 2, 2 * DK_B, DV_B), jnp.float32)],
        compiler_params=_cparams(("parallel", "arbitrary")),
        name="ret_prompt",
    )(qr, kr, vb, gb, dmask, dq, kdec, sdec)


def _sample_bias(t5_table, w_buf, q_len):
    dist = np.arange(w_buf + q_len)
    by_bucket = t5_table[_t5_bucket(dist)].T.astype(jnp.float32)
    merged = None
    for win, dil in DIL_PATTERNS:
        ok = (dist % dil == 0) & (dist // dil <= win // dil)
        b = jnp.where(ok[None], by_bucket, -jnp.inf)
        merged = b if merged is None else jnp.logaddexp(merged, b)
    merged = jnp.where(jnp.isfinite(merged), merged, NEG)
    rev = jnp.concatenate([jnp.full((H_A, q_len - 1), NEG, jnp.float32), merged], axis=1)[:, ::-1]
    per_query = jnp.stack([rev[:, q_len - 1 - i:q_len - 1 - i + w_buf + q_len] for i in range(q_len)], axis=1)
    pad = jnp.full((H_A, q_len, LANES - q_len), NEG, jnp.float32)
    return jnp.concatenate([per_query, pad], axis=2).reshape(H_A * q_len, w_buf + LANES)


def _sample_attention(q, k_new, v_new, kt_ref, vt_ref, bias_ref):
    q_len = q.shape[0]
    w_buf = kt_ref.shape[-1]
    bf16 = jnp.bfloat16
    rows = H_A * q_len
    q_rep = jnp.concatenate([q] * H_A, axis=0)
    row_head = jnp.right_shift(lax.broadcasted_iota(jnp.int32, (rows, A_WIDTH), 0), q_len.bit_length() - 1)
    lane_head = jnp.right_shift(lax.broadcasted_iota(jnp.int32, (rows, A_WIDTH), 1), HD_A.bit_length() - 1)
    own = row_head == lane_head
    lhs = jnp.where(own, q_rep, 0.0).astype(bf16)
    padz = jnp.zeros((LANES - q_len, A_WIDTH), jnp.float32)
    k_new = jnp.concatenate([k_new, padz], axis=0).astype(bf16)
    v_new = jnp.concatenate([v_new, padz], axis=0).astype(bf16)
    kt = kt_ref[...].reshape(A_WIDTH, w_buf).astype(bf16)
    vt = vt_ref[...].reshape(A_WIDTH, w_buf).astype(bf16)
    s_buf = _bdot(lhs, kt) + bias_ref[:, 0:w_buf]
    s_new = _bdot_nt(lhs, k_new) + bias_ref[:, w_buf:]
    m = jnp.maximum(jnp.max(s_buf, axis=1, keepdims=True), jnp.max(s_new, axis=1, keepdims=True))
    p_buf = jnp.exp(s_buf - m)
    p_new = jnp.exp(s_new - m)
    den = jnp.sum(p_buf, axis=1, keepdims=True) + jnp.sum(p_new, axis=1, keepdims=True)
    o_all = (_bdot_nt(p_buf.astype(bf16), vt) + _bdot(p_new.astype(bf16), v_new)) / den
    o_all = jnp.where(own, o_all, 0.0)
    out = o_all[0:q_len]
    for h in range(1, H_A):
        out = out + o_all[h * q_len:(h + 1) * q_len]
    return out


RET_GROUP = 16


def _ret_sample_body(q_ref, k_ref, v_ref, g_ref, s_ref, dmask_ref, dq_ref, kdec_ref, sdec_ref,
                     o_ref, s_out_ref, *, q_len):
    rows = q_ref.shape[0]
    low = lax.broadcasted_iota(jnp.int32, (rows, LANES), 1) < DK_B
    top = lax.broadcasted_iota(jnp.int32, (2 * DK_B, DV_B), 0) < DK_B
    col = lax.broadcasted_iota(jnp.int32, (2 * DK_B, rows), 1)
    for pair in range(H_B // 2):
        lanes = slice(pair * LANES, (pair + 1) * LANES)
        q2 = q_ref[:, lanes]
        k2 = k_ref[:, lanes]
        kt = (k2 * kdec_ref[pair]).T
        zero = jnp.zeros_like(q2)
        upd = [[], []]
        for hh in range(2):
            h = 2 * pair + hh
            vl = slice(h * DV_B, (h + 1) * DV_B)
            qm = jnp.where(low, q2, zero) if hh == 0 else jnp.where(low, zero, q2)
            vh = v_ref[:, vl]
            a = _bdot_nt(qm, k2) * dmask_ref[h]
            cross = jnp.concatenate(
                [_bdot(qm[r * q_len:(r + 1) * q_len], s_ref[r, pair]) for r in range(RET_GROUP)], axis=0)
            o = _bdot(a, vh) + cross * dq_ref[h]
            o_ref[:, vl] = _norm_gate(o, g_ref[:, vl]).astype(o_ref.dtype)
            for r in range(RET_GROUP):
                upd[hh].append(_bdot(jnp.where((col >= r * q_len) & (col < (r + 1) * q_len), kt, 0.0), vh))
        for r in range(RET_GROUP):
            s_out_ref[r, pair] = s_ref[r, pair] * sdec_ref[pair] + jnp.where(top, upd[0][r], upd[1][r])


def _ret_sample(qr, kr, vb, gb, state, layer, q_len):
    n = qr.shape[0]
    batch = n // q_len
    rows = RET_GROUP * q_len
    dmask1, dq1, kdec1, sdec = _ret_tables(q_len)
    eye = jnp.eye(RET_GROUP, dtype=jnp.float32)
    dmask = jnp.einsum("rs,hij->hrisj", eye, dmask1).reshape(H_B, rows, rows)
    dq = jnp.tile(dq1, (1, RET_GROUP, 1))
    kdec = jnp.tile(kdec1, (1, RET_GROUP, 1))
    row = lambda w: pl.BlockSpec((rows, w), lambda i: (i, 0))
    st_in = pl.BlockSpec((None, RET_GROUP, H_B // 2, 2 * DK_B, DV_B), lambda i: (layer, i, 0, 0, 0))
    st_out = pl.BlockSpec((RET_GROUP, H_B // 2, 2 * DK_B, DV_B), lambda i: (i, 0, 0, 0))
    return pl.pallas_call(
        functools.partial(_ret_sample_body, q_len=q_len),
        grid=(batch // RET_GROUP,),
        in_specs=[row(RET_QK_WIDTH), row(RET_QK_WIDTH), row(RET_V_WIDTH), row(RET_V_WIDTH), st_in,
                  _const_spec(dmask.shape), _const_spec(dq.shape), _const_spec(kdec.shape),
                  _const_spec(sdec.shape)],
        out_specs=[row(RET_V_WIDTH), st_out],
        out_shape=[jax.ShapeDtypeStruct((n, RET_V_WIDTH), jnp.bfloat16),
                   jax.ShapeDtypeStruct((batch, H_B // 2, 2 * DK_B, DV_B), jnp.float32)],
        compiler_params=_cparams(("parallel",)),
        name="ret_sample",
    )(qr, kr, vb, gb, state, dmask, dq, kdec, sdec)


CONV_PAD = 32
CONV_BLK = 32
CONV_SPLIT = 2


def _glu(x, g, w1_ref, b1_ref):
    xn = _rms(x, g).astype(jnp.bfloat16)
    a = _bdot(xn, w1_ref[:, 0:D_MODEL]) + b1_ref[:, 0:D_MODEL]
    gate = _bdot(xn, w1_ref[:, D_MODEL:]) + b1_ref[:, D_MODEL:]
    return a / (1.0 + jnp.exp(-gate))


def _conv_tail(x, c, lng_ref, lnb_ref, w2_ref, b2_ref):
    mu = jnp.mean(c, axis=-1, keepdims=True)
    d = c - mu
    var = jnp.mean(d * d, axis=-1, keepdims=True)
    nrm = d * lax.rsqrt(var + LN_EPS) * lng_ref[...] + lnb_ref[...]
    return x + _bdot(_silu(nrm).astype(jnp.bfloat16), w2_ref[...]) + b2_ref[...]


def _conv_prompt_body(x_ref, g_ref, w1_ref, b1_ref, wdw_ref, bdw_ref, lng_ref, lnb_ref, w2_ref, b2_ref,
                      shift_ref, o_ref, buf_ref, ext, cbuf):
    tm = x_ref.shape[0]
    hist = CONV_W - 1

    @pl.when(pl.program_id(1) == 0)
    def _():
        ext[0:CONV_PAD, :] = jnp.zeros((CONV_PAD, D_MODEL), jnp.float32)

    blk, lane_blk, off = CONV_BLK, 256, CONV_PAD - hist

    def depthwise(lo, hi):
        for r0 in range(lo, hi, blk):
            for lb in range(D_MODEL // lane_blk):
                lanes = slice(lb * lane_blk, (lb + 1) * lane_blk)
                win = ext[r0:r0 + blk + CONV_PAD, lanes]
                acc = jnp.broadcast_to(bdw_ref[:, lanes], (blk, lane_blk))
                shifted = []
                for r in range(SUBLANES):
                    taps = [k for k in range(CONV_W) if (k + off) % SUBLANES == r]
                    rows = blk if r == 0 else blk + SUBLANES
                    y = None
                    for k in taps:
                        a0 = k + off - r
                        tiles = win[a0:a0 + rows].reshape(rows // SUBLANES, SUBLANES, lane_blk)
                        term = (tiles * wdw_ref[k, :, lanes]).reshape(rows, lane_blk)
                        y = term if y is None else y + term
                    if r == 0:
                        acc = acc + y
                    else:
                        shifted.append(y)
                shifted.append(jnp.zeros((SUBLANES, lane_blk), jnp.float32))
                stacked = jnp.concatenate(shifted, axis=0).astype(jnp.bfloat16)
                cbuf[r0:r0 + blk, lanes] = acc + _bdot(shift_ref[...], stacked)

    part = tm // CONV_SPLIT
    g = g_ref[...]
    for h in range(CONV_SPLIT):
        rows = slice(h * part, (h + 1) * part)
        ext[CONV_PAD + h * part:CONV_PAD + (h + 1) * part, :] = _glu(x_ref[rows, :], g, w1_ref, b1_ref)
    for h in range(CONV_SPLIT):
        rows = slice(h * part, (h + 1) * part)
        depthwise(h * part, (h + 1) * part)
        o_ref[rows, :] = _conv_tail(x_ref[rows, :], cbuf[rows, :], lng_ref, lnb_ref, w2_ref, b2_ref)
    buf_ref[...] = ext[tm + CONV_PAD - hist:, :]
    ext[0:CONV_PAD, :] = ext[tm:, :]


def _conv_prompt(x, g, w1, b1, wdw, bdw, lng, lnb, w2, b2, layer, batch, seq):
    tm = ROW_TILE
    nblk = seq // tm
    n = x.shape[0]
    row = pl.BlockSpec((tm, D_MODEL), lambda b, j: (b * nblk + j, 0))
    vec = lambda a: a.reshape(1, -1)
    f32 = jnp.float32
    wdw = jnp.broadcast_to(wdw[:, None, :], (CONV_W, SUBLANES, D_MODEL))
    part = CONV_BLK + SUBLANES
    shift = np.zeros((CONV_BLK, SUBLANES * part), np.float32)
    for r in range(1, SUBLANES):
        shift[np.arange(CONV_BLK), (r - 1) * part + np.arange(CONV_BLK) + r] = 1.0
    shift = jnp.asarray(shift[:, :(SUBLANES - 1) * part + SUBLANES], jnp.bfloat16)
    return pl.pallas_call(
        _conv_prompt_body,
        grid=(batch, nblk),
        in_specs=[row, _const_spec((1, D_MODEL)), _stack_spec(w1, (layer,)), _const_spec((1, 2 * D_MODEL)),
                  _const_spec(wdw.shape), _const_spec((1, D_MODEL)), _const_spec((1, D_MODEL)),
                  _const_spec((1, D_MODEL)), _stack_spec(w2, (layer,)), _const_spec((1, D_MODEL)),
                  _const_spec(shift.shape)],
        out_specs=[row, pl.BlockSpec((None, CONV_W - 1, D_MODEL), lambda b, j: (b, 0, 0))],
        out_shape=[jax.ShapeDtypeStruct((n, D_MODEL), f32),
                   jax.ShapeDtypeStruct((batch, CONV_W - 1, D_MODEL), f32)],
        scratch_shapes=[pltpu.VMEM((tm + CONV_PAD, D_MODEL), f32), pltpu.VMEM((tm, D_MODEL), f32)],
        compiler_params=_cparams(("parallel", "arbitrary")),
        name="conv_prompt",
    )(x, vec(g), w1, vec(b1), wdw, vec(bdw), vec(lng), vec(lnb), w2, vec(b2), shift)


CONV_GROUP = 32


def _conv_sample_body(x_ref, g_ref, w1_ref, b1_ref, wdw_ref, bdw_ref, lng_ref, lnb_ref, w2_ref, b2_ref,
                      st_ref, o_ref, buf_ref, ext, cbuf, *, q_len):
    hist = CONV_W - 1
    x = x_ref[...]
    glu = _glu(x, g_ref[...], w1_ref, b1_ref)
    for r in range(CONV_GROUP):
        ext[r, 0:hist, :] = st_ref[r]
        ext[r, hist:hist + q_len, :] = glu[r * q_len:(r + 1) * q_len]
    for r in range(CONV_GROUP):
        acc = jnp.broadcast_to(bdw_ref[...], (q_len, D_MODEL))
        for k in range(CONV_W):
            acc = acc + ext[r, k:k + q_len, :] * wdw_ref[k:k + 1, :]
        cbuf[r * q_len:(r + 1) * q_len, :] = acc
        buf_ref[r] = ext[r, q_len:q_len + hist, :]
    o_ref[...] = _conv_tail(x, cbuf[...], lng_ref, lnb_ref, w2_ref, b2_ref)


def _conv_sample(x, g, w1, b1, wdw, bdw, lng, lnb, w2, b2, state, layer, q_len):
    n = x.shape[0]
    batch = n // q_len
    rows = CONV_GROUP * q_len
    hist = CONV_W - 1
    row = pl.BlockSpec((rows, D_MODEL), lambda i: (i, 0))
    vec = lambda a: a.reshape(1, -1)
    f32 = jnp.float32
    return pl.pallas_call(
        functools.partial(_conv_sample_body, q_len=q_len),
        grid=(batch // CONV_GROUP,),
        in_specs=[row, _const_spec((1, D_MODEL)), _stack_spec(w1, (layer,)), _const_spec((1, 2 * D_MODEL)),
                  _const_spec(wdw.shape), _const_spec((1, D_MODEL)), _const_spec((1, D_MODEL)),
                  _const_spec((1, D_MODEL)), _stack_spec(w2, (layer,)), _const_spec((1, D_MODEL)),
                  pl.BlockSpec((None, CONV_GROUP, hist, D_MODEL), lambda i: (layer, i, 0, 0))],
        out_specs=[row, pl.BlockSpec((CONV_GROUP, hist, D_MODEL), lambda i: (i, 0, 0))],
        out_shape=[jax.ShapeDtypeStruct((n, D_MODEL), f32),
                   jax.ShapeDtypeStruct((batch, hist, D_MODEL), f32)],
        scratch_shapes=[pltpu.VMEM((CONV_GROUP, hist + q_len + 2, D_MODEL), f32),
                        pltpu.VMEM((rows, D_MODEL), f32)],
        compiler_params=_cparams(("parallel",)),
        name="conv_sample",
    )(x, vec(g), w1, vec(b1), wdw, vec(bdw), vec(lng), vec(lnb), w2, vec(b2), state)


def _t5_bucket(dist):
    dist = np.asarray(dist).astype(np.int32)
    max_exact = T5_BUCKETS // 2
    log_ratio = np.log(np.maximum(dist, 1) / max_exact) / math.log(T5_MAX_DIST / max_exact)
    large = np.minimum(max_exact + (log_ratio * (T5_BUCKETS - max_exact)).astype(np.int32), T5_BUCKETS - 1)
    return np.where(dist < max_exact, dist, large).astype(np.int32)


def _rot_tables(pos):
    half = DK_B // 2
    ang = 1.0 / (10000.0 ** jnp.linspace(0.0, 1.0, half, dtype=jnp.float32))
    th = pos.astype(jnp.float32)[:, None] * ang[None, :]
    cos = jnp.repeat(jnp.cos(th), 2, axis=1)
    sin = jnp.repeat(jnp.sin(th), 2, axis=1) * jnp.tile(jnp.array([-1.0, 1.0], jnp.float32), half)[None, :]
    return jnp.tile(cos, (1, H_B)), jnp.tile(sin, (1, H_B))


def kernel(x_prompt, x_sample, cache_win_k, cache_win_v, state_ret, state_conv, t5_table, rms_g, final_g,
           w_ffn_gate, w_ffn_up, w_ffn_down, w_mix_in, w_mix_out, w_pw1, b_pw1, w_dw, b_dw, ln_g, ln_b,
           w_pw2, b_pw2):
    batch, seq, _ = x_prompt.shape
    dec_batch, dec_seq, _ = x_sample.shape
    depth = rms_g.shape[0]
    n_even = cache_win_k.shape[0]
    w_buf = cache_win_k.shape[2]
    keep = min(DIL_PATTERNS[-1][0], seq)
    bf16 = jnp.bfloat16

    xp = x_prompt.reshape(batch * seq, D_MODEL)
    xs = x_sample.reshape(dec_batch * dec_seq, D_MODEL)

    wg, wu, wd = w_ffn_gate.astype(bf16), w_ffn_up.astype(bf16), w_ffn_down.astype(bf16)
    w_in, w_out = w_mix_in.astype(bf16), w_mix_out.astype(bf16)
    w1, w2 = w_pw1.astype(bf16), w_pw2.astype(bf16)

    cos_p, sin_p = _rot_tables(jnp.arange(seq))
    cos_s, sin_s = _rot_tables(PAST_LEN + jnp.arange(ROW_TILE) % dec_seq)
    bias_p = _prompt_bias(t5_table)
    bias_s = _sample_bias(t5_table, w_buf, dec_seq)
    assert dec_seq & (dec_seq - 1) == 0 and dec_seq <= LANES
    ck = jnp.transpose(cache_win_k, (0, 1, 3, 4, 2))
    cv = jnp.transpose(cache_win_v, (0, 1, 3, 4, 2))
    st_ret = state_ret.reshape(n_even, dec_batch, H_B // 2, 2 * DK_B, DV_B)

    kp_l, vp_l, sp_l, cp_l, ks_l, vs_l, ss_l, cs_l = [], [], [], [], [], [], [], []
    for layer in range(depth):
        g = rms_g[layer]
        xp = _ffn(xp, g[0], wg, wu, wd, (layer, 0))
        xs = _ffn(xs, g[0], wg, wu, wd, (layer, 0))
        last = layer == depth - 1
        post = final_g if last else None
        ffn_b = (g[2], wg, wu, wd, (layer, 1))
        if layer % 2 == 0:
            e = layer // 2
            qa, ka, va, qr, kr, vb, gb = _proj(xp, g[1], w_in, e, cos_p, sin_p, bf16)
            oa = _attn_prompt(qa, ka, va, bias_p, batch, seq)
            ob, s_end = _ret_prompt(qr, kr, vb, gb, batch, seq)
            kp_l.append(ka.reshape(batch, seq, A_WIDTH)[:, -keep:].reshape(batch, keep, H_A, HD_A))
            vp_l.append(va.reshape(batch, seq, A_WIDTH)[:, -keep:].reshape(batch, keep, H_A, HD_A))
            sp_l.append(s_end)

            qa, ka, va, qr, kr, vb, gb = _proj(xs, g[1], w_in, e, cos_s, sin_s, jnp.float32)
            xp, oa = _ffn(xp, *ffn_b, pre=(oa, ob, w_out, (e,)), post_g=post,
                          rider=(qa, ka, va, ck, cv, e, bias_s, dec_seq))
            ob, s_new = _ret_sample(qr, kr, vb, gb, st_ret, e, dec_seq)
            xs = _ffn(xs, *ffn_b, pre=(oa, ob, w_out, (e,)), post_g=post)
            ks_l.append(ka.reshape(dec_batch, dec_seq, H_A, HD_A))
            vs_l.append(va.reshape(dec_batch, dec_seq, H_A, HD_A))
            ss_l.append(s_new.reshape(dec_batch, H_B, DK_B, DV_B))
        else:
            o = layer // 2
            conv = (g[1], w1, b_pw1[o], w_dw[o], b_dw[o], ln_g[o], ln_b[o], w2, b_pw2[o])
            xp, cp = _conv_prompt(xp, *conv, o, batch, seq)
            xs, cs = _conv_sample(xs, *conv, state_conv, o, dec_seq)
            cp_l.append(cp)
            cs_l.append(cs)
            xp = _ffn(xp, *ffn_b, post_g=post)
            xs = _ffn(xs, *ffn_b, post_g=post)

    stk = jnp.stack
    return (xp.reshape(batch, seq, D_MODEL), xs.reshape(dec_batch, dec_seq, D_MODEL),
            stk(kp_l), stk(vp_l), stk(sp_l), stk(cp_l), stk(ks_l), stk(vs_l), stk(ss_l), stk(cs_l))
```

```python
import functools
import math

import jax
import jax.numpy as jnp
import numpy as np
from jax import lax
from jax.experimental import pallas as pl
from jax.experimental.pallas import tpu as pltpu

D_MODEL = 1024
D_FF = 2816
H_A, HD_A = 8, 64
A_WIDTH = H_A * HD_A
DIL_PATTERNS = ((128, 1), (512, 4), (2048, 16))
H_B, DK_B, DV_B = 4, 64, 128
RET_QK_WIDTH = H_B * DK_B
RET_V_WIDTH = H_B * DV_B
RET_CHUNK = 128
PROJ_COLS = 3 * A_WIDTH + 2 * RET_QK_WIDTH + 2 * RET_V_WIDTH
CONV_W = 31
T5_BUCKETS = 32
T5_MAX_DIST = 2048
PAST_LEN = 2048
RMS_EPS = 1e-6
LN_EPS = 1e-5

LANES = 128
SUBLANES = 8
VMEM_LIMIT_BYTES = 56 * 1024 * 1024

NEG = -1e30
BAND_Q = 128
ATT_SPAN = 2048
RIDER_PHASES = 4
RET_UNROLL = 4
ATT_UNROLL = 8
ROW_TILE = 512
FFN_TILE = 1024


def _cparams(sem):
    return pltpu.CompilerParams(dimension_semantics=sem, vmem_limit_bytes=VMEM_LIMIT_BYTES)


def _const_spec(shape):
    nd = len(shape)
    return pl.BlockSpec(shape, lambda *_: (0,) * nd, pipeline_mode=pl.Buffered(1))


def _stack_spec(stacked, idx):
    tail = stacked.shape[len(idx):]
    return pl.BlockSpec((None,) * len(idx) + tail, lambda *_: tuple(idx) + (0,) * len(tail),
                        pipeline_mode=pl.Buffered(1))


def _rms(x, g):
    y = x * lax.rsqrt(jnp.mean(x * x, axis=-1, keepdims=True) + RMS_EPS)
    return y * g


def _silu(x):
    return x / (1.0 + jnp.exp(-x))


def _bdot(a, b):
    return jnp.dot(a, b, preferred_element_type=jnp.float32)


def _bdot_nt(a, b):
    return lax.dot_general(a, b, (((1,), (1,)), ((), ())), preferred_element_type=jnp.float32)


def _ffn_body(*refs, pre, post, ff_chunk, rider_layer):
    it = iter(refs)
    x_ref = next(it)
    if pre:
        oa_ref, ob_ref, wo_ref = next(it), next(it), next(it)
    g_ref, wg_ref, wu_ref, wd_ref = next(it), next(it), next(it), next(it)
    if post:
        fg_ref = next(it)
    if rider_layer is not None:
        qs_ref, ks_ref, vs_ref, ck_hbm, cv_hbm, bias_ref = [next(it) for _ in range(6)]
    o_ref = next(it)
    if rider_layer is not None:
        oas_ref = next(it)
    h_scr = next(it)
    live = {}

    def phase_in():
        x = x_ref[...]
        if pre:
            oa = oa_ref[...].astype(jnp.bfloat16)
            x = x + _bdot(oa, wo_ref[0:A_WIDTH, :]) + _bdot(ob_ref[...], wo_ref[A_WIDTH:, :])
        live["x"] = x
        live["xn"] = _rms(x, g_ref[...]).astype(jnp.bfloat16)

    def hidden(lo, hi):
        def run():
            gate = _bdot(live["xn"], wg_ref[:, lo:hi])
            up = _bdot(live["xn"], wu_ref[:, lo:hi])
            h_scr[:, lo:hi] = (_silu(gate) * up).astype(jnp.bfloat16)
        return run

    def out(lo, hi):
        def run():
            y = live["x"][:, lo:hi] + 0.5 * _bdot(h_scr[...], wd_ref[:, lo:hi])
            if post:
                assert (lo, hi) == (0, D_MODEL)
                y = _rms(y, fg_ref[...])
            o_ref[:, lo:hi] = y
        return run

    if rider_layer is None:
        for op in [phase_in, hidden(0, ff_chunk), hidden(ff_chunk, D_FF), out(0, D_MODEL)]:
            op()
        return

    q1, q2, q3 = 5 * LANES, ff_chunk, ff_chunk + 5 * LANES
    half = D_MODEL // 2
    phases = [[phase_in, hidden(0, q1)], [hidden(q1, q2), hidden(q2, q3)],
              [hidden(q3, D_FF), out(0, half)], [out(half, D_MODEL)]]
    kbuf, vbuf, sem = next(it), next(it), next(it)
    per_step = len(phases)
    assert per_step == RIDER_PHASES and per_step % 2 == 0
    q_len = qs_ref.shape[0] // per_step
    step = pl.program_id(0)

    def fetch(req, slot):
        return [pltpu.make_async_copy(ck_hbm.at[rider_layer, req], kbuf.at[slot], sem.at[slot, 0]),
                pltpu.make_async_copy(cv_hbm.at[rider_layer, req], vbuf.at[slot], sem.at[slot, 1])]

    @pl.when(step == 0)
    def _():
        for cp in fetch(0, 0):
            cp.start()

    own = _own_head_mask(q_len)
    for r, phase in enumerate(phases):
        req, slot = step * per_step + r, r % 2
        for cp in fetch(req, slot):
            cp.wait()
        if r + 1 < per_step:
            for cp in fetch(req + 1, 1 - slot):
                cp.start()
        else:
            @pl.when(step + 1 < pl.num_programs(0))
            def _():
                for cp in fetch(req + 1, 1 - slot):
                    cp.start()
        rows = slice(r * q_len, (r + 1) * q_len)
        s_buf, s_new = _sample_scores(qs_ref[rows, :], ks_ref[rows, :], kbuf.at[slot], bias_ref, own)
        phase[0]()
        oas_ref[rows, :] = _sample_output(s_buf, s_new, vs_ref[rows, :], vbuf.at[slot], own)
        for op in phase[1:]:
            op()


def _ffn(x, g, wg, wu, wd, idx, pre=None, post_g=None, rider=None):
    n = x.shape[0]
    plain = pre is None and rider is None
    tm = FFN_TILE if plain and n >= 2 * FFN_TILE else ROW_TILE
    n_steps = n // tm
    ff_chunk = D_FF // 2
    row = lambda w: pl.BlockSpec((tm, w), lambda i: (i, 0))
    args, specs = [x], [row(D_MODEL)]
    if pre is not None:
        oa, ob, wo, wo_idx = pre
        args += [oa, ob, wo]
        specs += [row(A_WIDTH), row(RET_V_WIDTH), _stack_spec(wo, wo_idx)]
    args += [g.reshape(1, D_MODEL), wg, wu, wd]
    specs += [_const_spec((1, D_MODEL)), _stack_spec(wg, idx), _stack_spec(wu, idx), _stack_spec(wd, idx)]
    if post_g is not None:
        args.append(post_g.reshape(1, D_MODEL))
        specs.append(_const_spec((1, D_MODEL)))
    out_specs, out_shape = row(D_MODEL), jax.ShapeDtypeStruct((n, D_MODEL), jnp.float32)
    scratch = [pltpu.VMEM((tm, D_FF), jnp.bfloat16)]
    semantics, rider_layer = ("parallel",), None
    if rider is not None:
        qs, ks, vs, cache_kt, cache_vt, rider_layer, bias_s, q_len = rider
        per_step = RIDER_PHASES
        assert qs.shape[0] == n_steps * per_step * q_len
        new = pl.BlockSpec((per_step * q_len, A_WIDTH), lambda i: (i, 0))
        hbm = pl.BlockSpec(memory_space=pl.ANY)
        args += [qs, ks, vs, cache_kt, cache_vt, bias_s]
        specs += [new, new, new, hbm, hbm, _const_spec(bias_s.shape)]
        out_specs = [out_specs, new]
        out_shape = [out_shape, jax.ShapeDtypeStruct((qs.shape[0], A_WIDTH), jnp.float32)]
        slot = pltpu.VMEM((2,) + cache_kt.shape[2:], jnp.float32)
        scratch += [slot, slot, pltpu.SemaphoreType.DMA((2, 2))]
        semantics = ("arbitrary",)
    body = functools.partial(_ffn_body, pre=pre is not None, post=post_g is not None, ff_chunk=ff_chunk,
                             rider_layer=rider_layer)
    return pl.pallas_call(
        body,
        grid=(n_steps,),
        in_specs=specs,
        out_specs=out_specs,
        out_shape=out_shape,
        scratch_shapes=scratch,
        compiler_params=_cparams(semantics),
        name="ffn",
    )(*args)


def _rotate(x, cos, sin_signed):
    w = x.shape[-1]
    lane = lax.broadcasted_iota(jnp.int32, x.shape, 1)
    partner = jnp.where((lane & 1) == 0, pltpu.roll(x, w - 1, 1), pltpu.roll(x, 1, 1))
    return x * cos + partner * sin_signed


def _proj_body(x_ref, g_ref, w_ref, cos_ref, sin_ref,
               qa_ref, ka_ref, va_ref, qr_ref, kr_ref, vb_ref, gb_ref):
    xn = _rms(x_ref[...], g_ref[...]).astype(jnp.bfloat16)
    a, qk, v = A_WIDTH, RET_QK_WIDTH, RET_V_WIDTH
    seg = lambda lo, width: _bdot(xn, w_ref[:, lo:lo + width])
    qa_ref[...] = seg(0, a) * (HD_A ** -0.5)
    ka_ref[...] = seg(a, a)
    va_ref[...] = seg(2 * a, a)
    cos, sin = cos_ref[...], sin_ref[...]
    qr_ref[...] = _rotate(seg(3 * a, qk), cos, sin).astype(qr_ref.dtype)
    kr_ref[...] = (_rotate(seg(3 * a + qk, qk), cos, sin) * (DK_B ** -0.5)).astype(kr_ref.dtype)
    vb_ref[...] = seg(3 * a + 2 * qk, v).astype(vb_ref.dtype)
    gb_ref[...] = seg(3 * a + 2 * qk + v, v)


def _proj(x, g, w_in, layer, cos_tab, sin_tab, ret_dtype):
    n = x.shape[0]
    tm = ROW_TILE
    tab_blocks = cos_tab.shape[0] // tm
    row = lambda w: pl.BlockSpec((tm, w), lambda i: (i, 0))
    tab = pl.BlockSpec((tm, RET_QK_WIDTH), lambda i: (i % tab_blocks, 0))
    f32 = jnp.float32
    outs = [(A_WIDTH, f32), (A_WIDTH, f32), (A_WIDTH, f32), (RET_QK_WIDTH, ret_dtype),
            (RET_QK_WIDTH, ret_dtype), (RET_V_WIDTH, ret_dtype), (RET_V_WIDTH, f32)]
    return pl.pallas_call(
        _proj_body,
        grid=(n // tm,),
        in_specs=[row(D_MODEL), _const_spec((1, D_MODEL)), _stack_spec(w_in, (layer,)), tab, tab],
        out_specs=[row(w) for w, _ in outs],
        out_shape=[jax.ShapeDtypeStruct((n, w), dt) for w, dt in outs],
        compiler_params=_cparams(("parallel",)),
        name="mix_proj",
    )(x, g.reshape(1, D_MODEL), w_in, cos_tab, sin_tab)


def _band_unit(q, kb, vb, bias, low, prev=None):
    nq = q.shape[0]
    zero = jnp.zeros_like(q)
    lhs = jnp.concatenate([jnp.where(low, q, zero), jnp.where(low, zero, q)], axis=0)
    s = _bdot_nt(lhs.astype(jnp.bfloat16), kb.astype(jnp.bfloat16)) + bias
    m_cur = jnp.max(s, axis=1, keepdims=True)
    if prev is not None:
        m_pa, m_pb, l_pa, l_pb, acc_prev = prev
        m_prev = jnp.concatenate([m_pa, m_pb], axis=0)
        m_next = jnp.maximum(m_prev, m_cur)
        alpha = jnp.exp(m_prev - m_next)
    else:
        m_next = jnp.broadcast_to(m_cur, (2 * nq, LANES))
    prob = jnp.exp(s - jnp.concatenate([m_next, m_next], axis=1))
    l_next = jnp.sum(prob, axis=1, keepdims=True)
    pv = _bdot(prob.astype(jnp.bfloat16), vb.astype(jnp.bfloat16))
    acc_next = jnp.where(low, pv[:nq], pv[nq:])
    if prev is not None:
        l_next = alpha * jnp.concatenate([l_pa, l_pb], axis=0) + l_next
        acc_next = jnp.where(low, alpha[:nq], alpha[nq:]) * acc_prev + acc_next
    else:
        l_next = jnp.broadcast_to(l_next, (2 * nq, LANES))
    return m_next, l_next, acc_next


NEAR_PATTERNS = DIL_PATTERNS[:2]


def _attn_near_body(q_ref, kp_ref, kc_ref, vp_ref, vc_ref, bias_ref, o_ref, lse_ref,
                    kk, vv, m_a, m_b, l_a, l_b, acc):
    first = (pl.program_id(2) == 0).astype(jnp.int32)
    span_rows = ATT_SPAN
    kk[0:span_rows, :] = kp_ref[...]
    kk[span_rows:, :] = kc_ref[...]
    vv[0:span_rows, :] = vp_ref[...]
    vv[span_rows:, :] = vc_ref[...]
    nq = BAND_Q
    low = lax.broadcasted_iota(jnp.int32, (nq, LANES), 1) < HD_A
    n_units = span_rows // nq

    for p, (_, dil) in enumerate(NEAR_PATTERNS):
        span = nq * dil
        shift = dil.bit_length() - 1
        init = p == 0

        def load(u, p=p, dil=dil, span=span, shift=shift, init=init):
            blk = u >> shift
            base = blk * span + (u & (dil - 1))
            rows = pl.ds(base, nq, stride=dil)
            band = pl.ds(span_rows + base - span, 2 * nq, stride=dil)
            flag = jnp.where(blk == 0, first, 0)
            vals = [q_ref[rows, :], kk[band, :], vv[band, :], bias_ref[p, flag]]
            prev = None if init else (m_a[rows, :], m_b[rows, :], l_a[rows, :], l_b[rows, :], acc[rows, :])
            return rows, vals, prev

        def group(g, carry, load=load):
            loaded = [load(g * ATT_UNROLL + j) for j in range(ATT_UNROLL)]
            results = [_band_unit(*vals, low, prev) for _, vals, prev in loaded]
            for (rows, _, _), (m_next, l_next, acc_next) in zip(loaded, results):
                acc[rows, :] = acc_next
                m_a[rows, :] = m_next[:nq]
                m_b[rows, :] = m_next[nq:]
                l_a[rows, :] = l_next[:nq]
                l_b[rows, :] = l_next[nq:]
            return carry

        lax.fori_loop(0, n_units // ATT_UNROLL, group, 0)

    low_all = lax.broadcasted_iota(jnp.int32, acc.shape, 1) < HD_A
    den = jnp.where(low_all, l_a[...], l_b[...])
    o_ref[...] = acc[...] / den
    lse_ref[...] = jnp.where(low_all, m_a[...], m_b[...]) + jnp.log(den)


FAR_DIL = DIL_PATTERNS[-1][1]
FAR_CLASSES = 4
FAR_STREAMS = 7


def _attn_far_body(q_hbm, k_hbm, v_hbm, near_hbm, lse_hbm, bias_ref, o_hbm,
                   qbuf, kbuf, vbuf, nearbuf, lsebuf, obuf, in_sem, out_sem, *, nblk):
    step = pl.program_id(0)
    n_steps = pl.num_programs(0)
    nq = BAND_Q
    per_span = FAR_DIL // FAR_CLASSES

    def locate(s):
        span = s // per_span
        first = (span % nblk) == 0
        return span, jnp.where(first, span, span - 1), (s % per_span) * FAR_CLASSES, first

    def in_copies(s, slot):
        span, prev, r0, _ = locate(s)
        copies = []
        for c in range(FAR_CLASSES):
            r = r0 + c
            pairs = [(q_hbm.at[span, :, r, :], qbuf.at[slot, c]),
                     (k_hbm.at[prev, :, r, :], kbuf.at[slot, c, pl.ds(0, nq), :]),
                     (k_hbm.at[span, :, r, :], kbuf.at[slot, c, pl.ds(nq, nq), :]),
                     (v_hbm.at[prev, :, r, :], vbuf.at[slot, c, pl.ds(0, nq), :]),
                     (v_hbm.at[span, :, r, :], vbuf.at[slot, c, pl.ds(nq, nq), :]),
                     (near_hbm.at[span, :, r, :], nearbuf.at[slot, c]),
                     (lse_hbm.at[span, :, r, :], lsebuf.at[slot, c])]
            copies += [pltpu.make_async_copy(src, dst, in_sem.at[slot, c, i])
                       for i, (src, dst) in enumerate(pairs)]
        return copies

    def out_copies(s, slot):
        span, _, r0, _ = locate(s)
        return [pltpu.make_async_copy(obuf.at[slot, c], o_hbm.at[span, :, r0 + c, :], out_sem.at[slot, c])
                for c in range(FAR_CLASSES)]

    slot = step % 2

    @pl.when(step == 0)
    def _():
        for cp in in_copies(0, 0):
            cp.start()

    @pl.when(step + 1 < n_steps)
    def _():
        for cp in in_copies(step + 1, 1 - slot):
            cp.start()

    for cp in in_copies(step, slot):
        cp.wait()

    @pl.when(step >= 2)
    def _():
        for cp in out_copies(step - 2, slot):
            cp.wait()

    first = locate(step)[3].astype(jnp.int32)
    low = lax.broadcasted_iota(jnp.int32, (nq, LANES), 1) < HD_A
    for c in range(FAR_CLASSES):
        for pair in range(H_A // 2):
            lanes = slice(pair * LANES, (pair + 1) * LANES)
            m_far, l_far, acc_far = _band_unit(qbuf[slot, c, :, lanes], kbuf[slot, c, :, lanes],
                                               vbuf[slot, c, :, lanes], bias_ref[pair, first], low)
            m_far = jnp.where(low, m_far[:nq], m_far[nq:])
            l_far = jnp.where(low, l_far[:nq], l_far[nq:])
            lse_near = lsebuf[slot, c, :, lanes]
            m_all = jnp.maximum(lse_near, m_far)
            w_near = jnp.exp(lse_near - m_all)
            w_far = jnp.exp(m_far - m_all)
            obuf[slot, c, :, lanes] = ((w_near * nearbuf[slot, c, :, lanes] + w_far * acc_far)
                                       / (w_near + w_far * l_far))

    for cp in out_copies(step, slot):
        cp.start()

    @pl.when(step == n_steps - 1)
    def _():
        for cp in out_copies(step, slot) + out_copies(step - 1, 1 - slot):
            cp.wait()


def _prompt_bias(t5_table):
    n = BAND_Q
    after_start = np.arange(2 * n)[None, None, :] >= n
    per_pattern = []
    for _, dil in DIL_PATTERNS:
        per_step = t5_table[_t5_bucket(np.arange(n + 1) * dil)].T.astype(jnp.float32)
        diag = jnp.concatenate([jnp.full((H_A, n - 1), NEG, jnp.float32), per_step[:, ::-1],
                                jnp.full((H_A, n), NEG, jnp.float32)], axis=1)
        skew = jnp.broadcast_to(diag[:, None, :], (H_A, n, 3 * n)).reshape(H_A, -1)
        skew = skew[:, :n * (3 * n - 1)].reshape(H_A, n, 3 * n - 1)
        full = skew[:, :, n - 1:3 * n - 1]
        start = jnp.where(after_start, full, NEG)
        per_pattern.append(jnp.stack([full, start], axis=0))
    b = jnp.stack(per_pattern, axis=0)
    b = b.reshape(len(DIL_PATTERNS), 2, H_A // 2, 2 * n, 2 * n)
    return jnp.transpose(b, (2, 0, 1, 3, 4))


def _attn_prompt(qa, ka, va, bias, batch, seq):
    assert ATT_SPAN == BAND_Q * FAR_DIL and FAR_DIL % FAR_CLASSES == 0
    nblk = seq // ATT_SPAN
    n = qa.shape[0]
    cur = pl.BlockSpec((ATT_SPAN, LANES), lambda hp, b, j: (b * nblk + j, hp))
    prev = pl.BlockSpec((ATT_SPAN, LANES), lambda hp, b, j: (b * nblk + jnp.maximum(j - 1, 0), hp))
    n_near = len(NEAR_PATTERNS)
    bias_near, bias_far = bias[:, :n_near], bias[:, n_near]
    bias_spec = pl.BlockSpec((None,) + bias_near.shape[1:], lambda hp, b, j: (hp, 0, 0, 0, 0))
    f32 = jnp.float32
    o_near, lse_near = pl.pallas_call(
        _attn_near_body,
        grid=(H_A // 2, batch, nblk),
        in_specs=[cur, prev, cur, prev, cur, bias_spec],
        out_specs=[cur, cur],
        out_shape=[jax.ShapeDtypeStruct((n, A_WIDTH), f32)] * 2,
        scratch_shapes=[pltpu.VMEM((2 * ATT_SPAN, LANES), f32), pltpu.VMEM((2 * ATT_SPAN, LANES), f32)]
        + [pltpu.VMEM((ATT_SPAN, LANES), f32)] * 5,
        compiler_params=_cparams(("parallel", "parallel", "parallel")),
        name="attn_near",
    )(qa, ka, ka, va, va, bias_near)

    spans = batch * nblk
    view = lambda a: a.reshape(spans, BAND_Q, FAR_DIL, A_WIDTH)
    n_steps = spans * FAR_DIL // FAR_CLASSES
    assert n_steps >= 2
    hbm = pl.BlockSpec(memory_space=pl.ANY)
    slots = lambda rows: pltpu.VMEM((2, FAR_CLASSES, rows, A_WIDTH), f32)
    out = pl.pallas_call(
        functools.partial(_attn_far_body, nblk=nblk),
        grid=(n_steps,),
        in_specs=[hbm] * 5 + [_const_spec(bias_far.shape)],
        out_specs=hbm,
        out_shape=jax.ShapeDtypeStruct((spans, BAND_Q, FAR_DIL, A_WIDTH), f32),
        scratch_shapes=[slots(BAND_Q), slots(2 * BAND_Q), slots(2 * BAND_Q), slots(BAND_Q), slots(BAND_Q),
                        slots(BAND_Q), pltpu.SemaphoreType.DMA((2, FAR_CLASSES, FAR_STREAMS)),
                        pltpu.SemaphoreType.DMA((2, FAR_CLASSES))],
        compiler_params=_cparams(("arbitrary",)),
        name="attn_far",
    )(view(qa), view(ka), view(va), view(o_near), view(lse_near), bias_far)
    return out.reshape(n, A_WIDTH)


def _ret_tables(length):
    lg = jnp.log(1.0 - 2.0 ** (-5.0 - jnp.arange(H_B, dtype=jnp.float32)))
    i = jnp.arange(length, dtype=jnp.float32)
    diff = i[:, None] - i[None, :]
    dmask = jnp.where(diff >= 0, jnp.exp(jnp.maximum(diff, 0.0)[None] * lg[:, None, None]), 0.0)
    dq = jnp.exp((i[:, None] + 1.0) * lg[None, :])
    dq = jnp.broadcast_to(dq.T[:, :, None], (H_B, length, DV_B))
    kd = jnp.exp((length - 1.0 - i)[:, None] * lg[None, :])
    kdec = jnp.repeat(kd, DK_B, axis=1).reshape(length, H_B // 2, 2 * DK_B).transpose(1, 0, 2)
    sd = jnp.exp(length * lg)
    sdec = jnp.broadcast_to(jnp.repeat(sd, DK_B).reshape(H_B // 2, 2 * DK_B, 1), (H_B // 2, 2 * DK_B, DV_B))
    return dmask, dq, kdec, sdec


def _norm_gate(o, gate):
    on = o * lax.rsqrt(jnp.mean(o * o, axis=-1, keepdims=True) + RMS_EPS)
    return _silu(gate) * on


def _ret_prompt_body(q_ref, k_ref, v_ref, g_ref, dmask_ref, dq_ref, kdec_ref, sdec_ref,
                     o_ref, s_out_ref, state):
    step = pl.program_id(1)

    @pl.when(step == 0)
    def _():
        state[...] = jnp.zeros(state.shape, jnp.float32)

    c_len = RET_CHUNK
    low = lax.broadcasted_iota(jnp.int32, (c_len, LANES), 1) < DK_B
    top = lax.broadcasted_iota(jnp.int32, (2 * DK_B, DV_B), 0) < DK_B

    def chunk(c, s_pairs):
        rows = pl.ds(pl.multiple_of(c * c_len, c_len), c_len)
        s_next = []
        for pair in range(H_B // 2):
            lanes = slice(pair * LANES, (pair + 1) * LANES)
            q2 = q_ref[rows, lanes]
            k2 = k_ref[rows, lanes]
            s_old = s_pairs[pair]
            kt = (k2.astype(jnp.float32) * kdec_ref[pair]).T.astype(jnp.bfloat16)
            zero = jnp.zeros_like(q2)
            qm = jnp.concatenate([jnp.where(low, q2, zero), jnp.where(low, zero, q2)], axis=0)
            vl = slice(2 * pair * DV_B, 2 * (pair + 1) * DV_B)
            v2 = v_ref[rows, vl]
            vz = jnp.zeros((c_len, DV_B), v2.dtype)
            v_diag = jnp.concatenate([jnp.concatenate([v2[:, :DV_B], vz], axis=1),
                                      jnp.concatenate([vz, v2[:, DV_B:]], axis=1)], axis=0)
            a = (_bdot_nt(qm, k2) * dmask_ref[pair]).astype(jnp.bfloat16)
            intra = _bdot(jnp.concatenate([a[:c_len], a[c_len:]], axis=1), v_diag)
            cross = _bdot(qm, s_old.astype(jnp.bfloat16))
            o = intra + jnp.concatenate([cross[:c_len], cross[c_len:]], axis=1) * dq_ref[pair]
            gate = g_ref[rows, vl]
            o_ref[rows, vl] = jnp.concatenate(
                [_norm_gate(o[:, :DV_B], gate[:, :DV_B]), _norm_gate(o[:, DV_B:], gate[:, DV_B:])],
                axis=1).astype(o_ref.dtype)
            upd = _bdot(kt, v2)
            s_next.append(s_old * sdec_ref[pair] + jnp.where(top, upd[:, :DV_B], upd[:, DV_B:]))
        return s_next

    def group(gi, carry):
        s_pairs = [state[pair] for pair in range(H_B // 2)]
        for j in range(RET_UNROLL):
            s_pairs = chunk(gi * RET_UNROLL + j, s_pairs)
        for pair in range(H_B // 2):
            state[pair] = s_pairs[pair]
        return carry

    lax.fori_loop(0, q_ref.shape[0] // (c_len * RET_UNROLL), group, 0)

    @pl.when(step == pl.num_programs(1) - 1)
    def _():
        for h in range(H_B):
            s_out_ref[h] = state[h // 2, (h % 2) * DK_B:(h % 2 + 1) * DK_B, :]


def _ret_prompt(qr, kr, vb, gb, batch, seq):
    rows = 1024
    nblk = seq // rows
    n = qr.shape[0]
    dmask, dq, kdec, sdec = _ret_tables(RET_CHUNK)
    dmask = dmask.reshape(H_B // 2, 2 * RET_CHUNK, RET_CHUNK)
    dq = dq.reshape(H_B // 2, 2, RET_CHUNK, DV_B).transpose(0, 2, 1, 3).reshape(H_B // 2, RET_CHUNK, 2 * DV_B)
    row = lambda w: pl.BlockSpec((rows, w), lambda b, j: (b * nblk + j, 0))
    return pl.pallas_call(
        _ret_prompt_body,
        grid=(batch, nblk),
        in_specs=[row(RET_QK_WIDTH), row(RET_QK_WIDTH), row(RET_V_WIDTH), row(RET_V_WIDTH),
                  _const_spec(dmask.shape), _const_spec(dq.shape), _const_spec(kdec.shape),
                  _const_spec(sdec.shape)],
        out_specs=[row(RET_V_WIDTH),
                   pl.BlockSpec((None, H_B, DK_B, DV_B), lambda b, j: (b, 0, 0, 0))],
        out_shape=[jax.ShapeDtypeStruct((n, RET_V_WIDTH), jnp.bfloat16),
                   jax.ShapeDtypeStruct((batch, H_B, DK_B, DV_B), jnp.float32)],
        scratch_shapes=[pltpu.VMEM((H_B // 2, 2 * DK_B, DV_B), jnp.float32)],
        compiler_params=_cparams(("parallel", "arbitrary")),
        name="ret_prompt",
    )(qr, kr, vb, gb, dmask, dq, kdec, sdec)


def _sample_bias(t5_table, w_buf, q_len):
    dist = np.arange(w_buf + q_len)
    by_bucket = t5_table[_t5_bucket(dist)].T.astype(jnp.float32)
    merged = None
    for win, dil in DIL_PATTERNS:
        ok = (dist % dil == 0) & (dist // dil <= win // dil)
        b = jnp.where(ok[None], by_bucket, -jnp.inf)
        merged = b if merged is None else jnp.logaddexp(merged, b)
    merged = jnp.where(jnp.isfinite(merged), merged, NEG)
    rev = jnp.concatenate([jnp.full((H_A, q_len - 1), NEG, jnp.float32), merged], axis=1)[:, ::-1]
    per_query = jnp.stack([rev[:, q_len - 1 - i:q_len - 1 - i + w_buf + q_len] for i in range(q_len)], axis=1)
    pad = jnp.full((H_A, q_len, LANES - q_len), NEG, jnp.float32)
    return jnp.concatenate([per_query, pad], axis=2).reshape(H_A * q_len, w_buf + LANES)


def _own_head_mask(q_len):
    rows = H_A * q_len
    row_head = jnp.right_shift(lax.broadcasted_iota(jnp.int32, (rows, A_WIDTH), 0), q_len.bit_length() - 1)
    lane_head = jnp.right_shift(lax.broadcasted_iota(jnp.int32, (rows, A_WIDTH), 1), HD_A.bit_length() - 1)
    return row_head == lane_head


def _sample_scores(q, k_new, kt_ref, bias_ref, own):
    q_len = q.shape[0]
    w_buf = kt_ref.shape[-1]
    bf16 = jnp.bfloat16
    q_rep = jnp.concatenate([q] * H_A, axis=0)
    lhs = jnp.where(own, q_rep, 0.0).astype(bf16)
    padz = jnp.zeros((LANES - q_len, A_WIDTH), jnp.float32)
    k_new = jnp.concatenate([k_new, padz], axis=0).astype(bf16)
    kt = kt_ref[...].reshape(A_WIDTH, w_buf).astype(bf16)
    s_buf = _bdot(lhs, kt) + bias_ref[:, 0:w_buf]
    s_new = _bdot_nt(lhs, k_new) + bias_ref[:, w_buf:]
    return s_buf, s_new


def _sample_output(s_buf, s_new, v_new, vt_ref, own):
    q_len = v_new.shape[0]
    w_buf = vt_ref.shape[-1]
    bf16 = jnp.bfloat16
    padz = jnp.zeros((LANES - q_len, A_WIDTH), jnp.float32)
    v_new = jnp.concatenate([v_new, padz], axis=0).astype(bf16)
    vt = vt_ref[...].reshape(A_WIDTH, w_buf).astype(bf16)
    m = jnp.maximum(jnp.max(s_buf, axis=1, keepdims=True), jnp.max(s_new, axis=1, keepdims=True))
    p_buf = jnp.exp(s_buf - m)
    p_new = jnp.exp(s_new - m)
    den = jnp.sum(p_buf, axis=1, keepdims=True) + jnp.sum(p_new, axis=1, keepdims=True)
    o_all = (_bdot_nt(p_buf.astype(bf16), vt) + _bdot(p_new.astype(bf16), v_new)) / den
    o_all = jnp.where(own, o_all, 0.0)
    out = o_all[0:q_len]
    for h in range(1, H_A):
        out = out + o_all[h * q_len:(h + 1) * q_len]
    return out


RET_GROUP = 16


def _ret_sample_body(q_ref, k_ref, v_ref, g_ref, s_ref, dmask_ref, dq_ref, kdec_ref, sdec_ref,
                     o_ref, s_out_ref, *, q_len):
    rows = q_ref.shape[0]
    low = lax.broadcasted_iota(jnp.int32, (rows, LANES), 1) < DK_B
    top = lax.broadcasted_iota(jnp.int32, (2 * DK_B, DV_B), 0) < DK_B
    col = lax.broadcasted_iota(jnp.int32, (2 * DK_B, rows), 1)
    for pair in range(H_B // 2):
        lanes = slice(pair * LANES, (pair + 1) * LANES)
        q2 = q_ref[:, lanes]
        k2 = k_ref[:, lanes]
        kt = (k2 * kdec_ref[pair]).T
        zero = jnp.zeros_like(q2)
        upd = [[], []]
        for hh in range(2):
            h = 2 * pair + hh
            vl = slice(h * DV_B, (h + 1) * DV_B)
            qm = jnp.where(low, q2, zero) if hh == 0 else jnp.where(low, zero, q2)
            vh = v_ref[:, vl]
            a = _bdot_nt(qm, k2) * dmask_ref[h]
            cross = jnp.concatenate(
                [_bdot(qm[r * q_len:(r + 1) * q_len], s_ref[r, pair]) for r in range(RET_GROUP)], axis=0)
            o = _bdot(a, vh) + cross * dq_ref[h]
            o_ref[:, vl] = _norm_gate(o, g_ref[:, vl]).astype(o_ref.dtype)
            for r in range(RET_GROUP):
                upd[hh].append(_bdot(jnp.where((col >= r * q_len) & (col < (r + 1) * q_len), kt, 0.0), vh))
        for r in range(RET_GROUP):
            s_out_ref[r, pair] = s_ref[r, pair] * sdec_ref[pair] + jnp.where(top, upd[0][r], upd[1][r])


def _ret_sample(qr, kr, vb, gb, state, layer, q_len):
    n = qr.shape[0]
    batch = n // q_len
    rows = RET_GROUP * q_len
    dmask1, dq1, kdec1, sdec = _ret_tables(q_len)
    eye = jnp.eye(RET_GROUP, dtype=jnp.float32)
    dmask = jnp.einsum("rs,hij->hrisj", eye, dmask1).reshape(H_B, rows, rows)
    dq = jnp.tile(dq1, (1, RET_GROUP, 1))
    kdec = jnp.tile(kdec1, (1, RET_GROUP, 1))
    row = lambda w: pl.BlockSpec((rows, w), lambda i: (i, 0))
    st_in = pl.BlockSpec((None, RET_GROUP, H_B // 2, 2 * DK_B, DV_B), lambda i: (layer, i, 0, 0, 0))
    st_out = pl.BlockSpec((RET_GROUP, H_B // 2, 2 * DK_B, DV_B), lambda i: (i, 0, 0, 0))
    return pl.pallas_call(
        functools.partial(_ret_sample_body, q_len=q_len),
        grid=(batch // RET_GROUP,),
        in_specs=[row(RET_QK_WIDTH), row(RET_QK_WIDTH), row(RET_V_WIDTH), row(RET_V_WIDTH), st_in,
                  _const_spec(dmask.shape), _const_spec(dq.shape), _const_spec(kdec.shape),
                  _const_spec(sdec.shape)],
        out_specs=[row(RET_V_WIDTH), st_out],
        out_shape=[jax.ShapeDtypeStruct((n, RET_V_WIDTH), jnp.bfloat16),
                   jax.ShapeDtypeStruct((batch, H_B // 2, 2 * DK_B, DV_B), jnp.float32)],
        compiler_params=_cparams(("parallel",)),
        name="ret_sample",
    )(qr, kr, vb, gb, state, dmask, dq, kdec, sdec)


CONV_PAD = 32
CONV_BLK = 64
CONV_LANES = 128
CONV_SPLIT = 2


def _glu(x, g, w1_ref, b1_ref):
    xn = _rms(x, g).astype(jnp.bfloat16)
    a = _bdot(xn, w1_ref[:, 0:D_MODEL]) + b1_ref[:, 0:D_MODEL]
    gate = _bdot(xn, w1_ref[:, D_MODEL:]) + b1_ref[:, D_MODEL:]
    return a / (1.0 + jnp.exp(-gate))


def _conv_tail(x, c, lng_ref, lnb_ref, w2_ref, b2_ref):
    mu = jnp.mean(c, axis=-1, keepdims=True)
    d = c - mu
    var = jnp.mean(d * d, axis=-1, keepdims=True)
    nrm = d * lax.rsqrt(var + LN_EPS) * lng_ref[...] + lnb_ref[...]
    return x + _bdot(_silu(nrm).astype(jnp.bfloat16), w2_ref[...]) + b2_ref[...]


def _conv_prompt_body(x_ref, g_ref, w1_ref, b1_ref, wdw_ref, bdw_ref, lng_ref, lnb_ref, w2_ref, b2_ref,
                      shift_ref, o_ref, buf_ref, ext, cbuf):
    tm = x_ref.shape[0]
    hist = CONV_W - 1

    @pl.when(pl.program_id(1) == 0)
    def _():
        ext[0:CONV_PAD, :] = jnp.zeros((CONV_PAD, D_MODEL), jnp.float32)

    blk, lane_blk, off = CONV_BLK, CONV_LANES, CONV_PAD - hist

    def depthwise(lo, hi):
        for r0 in range(lo, hi, blk):
            for lb in range(D_MODEL // lane_blk):
                lanes = slice(lb * lane_blk, (lb + 1) * lane_blk)
                win = ext[r0:r0 + blk + CONV_PAD, lanes]
                acc = jnp.broadcast_to(bdw_ref[:, lanes], (blk, lane_blk))
                shifted = []
                for r in range(SUBLANES):
                    taps = [k for k in range(CONV_W) if (k + off) % SUBLANES == r]
                    rows = blk if r == 0 else blk + SUBLANES
                    y = None
                    for k in taps:
                        a0 = k + off - r
                        tiles = win[a0:a0 + rows].reshape(rows // SUBLANES, SUBLANES, lane_blk)
                        term = (tiles * wdw_ref[k, :, lanes]).reshape(rows, lane_blk)
                        y = term if y is None else y + term
                    if r == 0:
                        acc = acc + y
                    else:
                        shifted.append(y)
                shifted.append(jnp.zeros((SUBLANES, lane_blk), jnp.float32))
                stacked = jnp.concatenate(shifted, axis=0).astype(jnp.bfloat16)
                cbuf[r0:r0 + blk, lanes] = acc + _bdot(shift_ref[...], stacked)

    part = tm // CONV_SPLIT
    g = g_ref[...]
    for h in range(CONV_SPLIT):
        rows = slice(h * part, (h + 1) * part)
        ext[CONV_PAD + h * part:CONV_PAD + (h + 1) * part, :] = _glu(x_ref[rows, :], g, w1_ref, b1_ref)
    for h in range(CONV_SPLIT):
        rows = slice(h * part, (h + 1) * part)
        depthwise(h * part, (h + 1) * part)
        o_ref[rows, :] = _conv_tail(x_ref[rows, :], cbuf[rows, :], lng_ref, lnb_ref, w2_ref, b2_ref)
    buf_ref[...] = ext[tm + CONV_PAD - hist:, :]
    ext[0:CONV_PAD, :] = ext[tm:, :]


def _conv_prompt(x, g, w1, b1, wdw, bdw, lng, lnb, w2, b2, layer, batch, seq):
    tm = ROW_TILE
    nblk = seq // tm
    n = x.shape[0]
    row = pl.BlockSpec((tm, D_MODEL), lambda b, j: (b * nblk + j, 0))
    vec = lambda a: a.reshape(1, -1)
    f32 = jnp.float32
    wdw = jnp.broadcast_to(wdw[:, None, :], (CONV_W, SUBLANES, D_MODEL))
    part = CONV_BLK + SUBLANES
    shift = np.zeros((CONV_BLK, SUBLANES * part), np.float32)
    for r in range(1, SUBLANES):
        shift[np.arange(CONV_BLK), (r - 1) * part + np.arange(CONV_BLK) + r] = 1.0
    shift = jnp.asarray(shift[:, :(SUBLANES - 1) * part + SUBLANES], jnp.bfloat16)
    return pl.pallas_call(
        _conv_prompt_body,
        grid=(batch, nblk),
        in_specs=[row, _const_spec((1, D_MODEL)), _stack_spec(w1, (layer,)), _const_spec((1, 2 * D_MODEL)),
                  _const_spec(wdw.shape), _const_spec((1, D_MODEL)), _const_spec((1, D_MODEL)),
                  _const_spec((1, D_MODEL)), _stack_spec(w2, (layer,)), _const_spec((1, D_MODEL)),
                  _const_spec(shift.shape)],
        out_specs=[row, pl.BlockSpec((None, CONV_W - 1, D_MODEL), lambda b, j: (b, 0, 0))],
        out_shape=[jax.ShapeDtypeStruct((n, D_MODEL), f32),
                   jax.ShapeDtypeStruct((batch, CONV_W - 1, D_MODEL), f32)],
        scratch_shapes=[pltpu.VMEM((tm + CONV_PAD, D_MODEL), f32), pltpu.VMEM((tm, D_MODEL), f32)],
        compiler_params=_cparams(("parallel", "arbitrary")),
        name="conv_prompt",
    )(x, vec(g), w1, vec(b1), wdw, vec(bdw), vec(lng), vec(lnb), w2, vec(b2), shift)


CONV_GROUP = 32


def _conv_sample_body(x_ref, g_ref, w1_ref, b1_ref, wdw_ref, bdw_ref, lng_ref, lnb_ref, w2_ref, b2_ref,
                      st_ref, o_ref, buf_ref, ext, cbuf, *, q_len):
    hist = CONV_W - 1
    x = x_ref[...]
    glu = _glu(x, g_ref[...], w1_ref, b1_ref)
    for r in range(CONV_GROUP):
        ext[r, 0:hist, :] = st_ref[r]
        ext[r, hist:hist + q_len, :] = glu[r * q_len:(r + 1) * q_len]
    for r in range(CONV_GROUP):
        acc = jnp.broadcast_to(bdw_ref[...], (q_len, D_MODEL))
        for k in range(CONV_W):
            acc = acc + ext[r, k:k + q_len, :] * wdw_ref[k:k + 1, :]
        cbuf[r * q_len:(r + 1) * q_len, :] = acc
        buf_ref[r] = ext[r, q_len:q_len + hist, :]
    o_ref[...] = _conv_tail(x, cbuf[...], lng_ref, lnb_ref, w2_ref, b2_ref)


def _conv_sample(x, g, w1, b1, wdw, bdw, lng, lnb, w2, b2, state, layer, q_len):
    n = x.shape[0]
    batch = n // q_len
    rows = CONV_GROUP * q_len
    hist = CONV_W - 1
    row = pl.BlockSpec((rows, D_MODEL), lambda i: (i, 0))
    vec = lambda a: a.reshape(1, -1)
    f32 = jnp.float32
    return pl.pallas_call(
        functools.partial(_conv_sample_body, q_len=q_len),
        grid=(batch // CONV_GROUP,),
        in_specs=[row, _const_spec((1, D_MODEL)), _stack_spec(w1, (layer,)), _const_spec((1, 2 * D_MODEL)),
                  _const_spec(wdw.shape), _const_spec((1, D_MODEL)), _const_spec((1, D_MODEL)),
                  _const_spec((1, D_MODEL)), _stack_spec(w2, (layer,)), _const_spec((1, D_MODEL)),
                  pl.BlockSpec((None, CONV_GROUP, hist, D_MODEL), lambda i: (layer, i, 0, 0))],
        out_specs=[row, pl.BlockSpec((CONV_GROUP, hist, D_MODEL), lambda i: (i, 0, 0))],
        out_shape=[jax.ShapeDtypeStruct((n, D_MODEL), f32),
                   jax.ShapeDtypeStruct((batch, hist, D_MODEL), f32)],
        scratch_shapes=[pltpu.VMEM((CONV_GROUP, hist + q_len + 2, D_MODEL), f32),
                        pltpu.VMEM((rows, D_MODEL), f32)],
        compiler_params=_cparams(("parallel",)),
        name="conv_sample",
    )(x, vec(g), w1, vec(b1), wdw, vec(bdw), vec(lng), vec(lnb), w2, vec(b2), state)


def _t5_bucket(dist):
    dist = np.asarray(dist).astype(np.int32)
    max_exact = T5_BUCKETS // 2
    log_ratio = np.log(np.maximum(dist, 1) / max_exact) / math.log(T5_MAX_DIST / max_exact)
    large = np.minimum(max_exact + (log_ratio * (T5_BUCKETS - max_exact)).astype(np.int32), T5_BUCKETS - 1)
    return np.where(dist < max_exact, dist, large).astype(np.int32)


def _rot_tables(pos):
    half = DK_B // 2
    ang = 1.0 / (10000.0 ** jnp.linspace(0.0, 1.0, half, dtype=jnp.float32))
    th = pos.astype(jnp.float32)[:, None] * ang[None, :]
    cos = jnp.repeat(jnp.cos(th), 2, axis=1)
    sin = jnp.repeat(jnp.sin(th), 2, axis=1) * jnp.tile(jnp.array([-1.0, 1.0], jnp.float32), half)[None, :]
    return jnp.tile(cos, (1, H_B)), jnp.tile(sin, (1, H_B))


def kernel(x_prompt, x_sample, cache_win_k, cache_win_v, state_ret, state_conv, t5_table, rms_g, final_g,
           w_ffn_gate, w_ffn_up, w_ffn_down, w_mix_in, w_mix_out, w_pw1, b_pw1, w_dw, b_dw, ln_g, ln_b,
           w_pw2, b_pw2):
    batch, seq, _ = x_prompt.shape
    dec_batch, dec_seq, _ = x_sample.shape
    depth = rms_g.shape[0]
    n_even = cache_win_k.shape[0]
    w_buf = cache_win_k.shape[2]
    keep = min(DIL_PATTERNS[-1][0], seq)
    bf16 = jnp.bfloat16

    xp = x_prompt.reshape(batch * seq, D_MODEL)
    xs = x_sample.reshape(dec_batch * dec_seq, D_MODEL)

    wg, wu, wd = w_ffn_gate.astype(bf16), w_ffn_up.astype(bf16), w_ffn_down.astype(bf16)
    w_in, w_out = w_mix_in.astype(bf16), w_mix_out.astype(bf16)
    w1, w2 = w_pw1.astype(bf16), w_pw2.astype(bf16)

    cos_p, sin_p = _rot_tables(jnp.arange(seq))
    cos_s, sin_s = _rot_tables(PAST_LEN + jnp.arange(ROW_TILE) % dec_seq)
    bias_p = _prompt_bias(t5_table)
    bias_s = _sample_bias(t5_table, w_buf, dec_seq)
    assert dec_seq & (dec_seq - 1) == 0 and dec_seq <= LANES
    ck = jnp.transpose(cache_win_k, (0, 1, 3, 4, 2))
    cv = jnp.transpose(cache_win_v, (0, 1, 3, 4, 2))
    st_ret = state_ret.reshape(n_even, dec_batch, H_B // 2, 2 * DK_B, DV_B)

    kp_l, vp_l, sp_l, cp_l, ks_l, vs_l, ss_l, cs_l = [], [], [], [], [], [], [], []
    for layer in range(depth):
        g = rms_g[layer]
        xp = _ffn(xp, g[0], wg, wu, wd, (layer, 0))
        xs = _ffn(xs, g[0], wg, wu, wd, (layer, 0))
        last = layer == depth - 1
        post = final_g if last else None
        ffn_b = (g[2], wg, wu, wd, (layer, 1))
        if layer % 2 == 0:
            e = layer // 2
            qa, ka, va, qr, kr, vb, gb = _proj(xp, g[1], w_in, e, cos_p, sin_p, bf16)
            oa = _attn_prompt(qa, ka, va, bias_p, batch, seq)
            ob, s_end = _ret_prompt(qr, kr, vb, gb, batch, seq)
            kp_l.append(ka.reshape(batch, seq, A_WIDTH)[:, -keep:].reshape(batch, keep, H_A, HD_A))
            vp_l.append(va.reshape(batch, seq, A_WIDTH)[:, -keep:].reshape(batch, keep, H_A, HD_A))
            sp_l.append(s_end)

            qa, ka, va, qr, kr, vb, gb = _proj(xs, g[1], w_in, e, cos_s, sin_s, jnp.float32)
            xp, oa = _ffn(xp, *ffn_b, pre=(oa, ob, w_out, (e,)), post_g=post,
                          rider=(qa, ka, va, ck, cv, e, bias_s, dec_seq))
            ob, s_new = _ret_sample(qr, kr, vb, gb, st_ret, e, dec_seq)
            xs = _ffn(xs, *ffn_b, pre=(oa, ob, w_out, (e,)), post_g=post)
            ks_l.append(ka.reshape(dec_batch, dec_seq, H_A, HD_A))
            vs_l.append(va.reshape(dec_batch, dec_seq, H_A, HD_A))
            ss_l.append(s_new.reshape(dec_batch, H_B, DK_B, DV_B))
        else:
            o = layer // 2
            conv = (g[1], w1, b_pw1[o], w_dw[o], b_dw[o], ln_g[o], ln_b[o], w2, b_pw2[o])
            xp, cp = _conv_prompt(xp, *conv, o, batch, seq)
            xs, cs = _conv_sample(xs, *conv, state_conv, o, dec_seq)
            cp_l.append(cp)
            cs_l.append(cs)
            xp = _ffn(xp, *ffn_b, post_g=post)
            xs = _ffn(xs, *ffn_b, post_g=post)

    stk = jnp.stack
    return (xp.reshape(batch, seq, D_MODEL), xs.reshape(dec_batch, dec_seq, D_MODEL),
            stk(kp_l), stk(vp_l), stk(sp_l), stk(cp_l), stk(ks_l), stk(vs_l), stk(ss_l), stk(cs_l))
```

```python
import functools
import math

import jax
import jax.numpy as jnp
import numpy as np
from jax import lax
from jax.experimental import pallas as pl
from jax.experimental.pallas import tpu as pltpu

D_MODEL = 1024
D_FF = 2816
H_A, HD_A = 8, 64
A_WIDTH = H_A * HD_A
DIL_PATTERNS = ((128, 1), (512, 4), (2048, 16))
H_B, DK_B, DV_B = 4, 64, 128
RET_QK_WIDTH = H_B * DK_B
RET_V_WIDTH = H_B * DV_B
RET_CHUNK = 128
PROJ_COLS = 3 * A_WIDTH + 2 * RET_QK_WIDTH + 2 * RET_V_WIDTH
CONV_W = 31
T5_BUCKETS = 32
T5_MAX_DIST = 2048
PAST_LEN = 2048
RMS_EPS = 1e-6
LN_EPS = 1e-5

LANES = 128
SUBLANES = 8
VMEM_LIMIT_BYTES = 56 * 1024 * 1024

NEG = -1e30
BAND_Q = 128
ATT_SPAN = 2048
RIDER_PHASES = 4
RET_UNROLL = 8
ATT_UNROLL = 8
ROW_TILE = 512
FFN_TILE = 1024


def _cparams(sem):
    return pltpu.CompilerParams(dimension_semantics=sem, vmem_limit_bytes=VMEM_LIMIT_BYTES)


def _const_spec(shape):
    nd = len(shape)
    return pl.BlockSpec(shape, lambda *_: (0,) * nd, pipeline_mode=pl.Buffered(1))


def _stack_spec(stacked, idx):
    tail = stacked.shape[len(idx):]
    return pl.BlockSpec((None,) * len(idx) + tail, lambda *_: tuple(idx) + (0,) * len(tail),
                        pipeline_mode=pl.Buffered(1))


def _rms(x, g):
    y = x * lax.rsqrt(jnp.mean(x * x, axis=-1, keepdims=True) + RMS_EPS)
    return y * g


def _silu(x):
    return x / (1.0 + jnp.exp(-x))


def _bdot(a, b):
    return jnp.dot(a, b, preferred_element_type=jnp.float32)


def _bdot_nt(a, b):
    return lax.dot_general(a, b, (((1,), (1,)), ((), ())), preferred_element_type=jnp.float32)


def _ffn_body(*refs, pre, post, ff_chunk, rider_layer):
    it = iter(refs)
    x_ref = next(it)
    if pre:
        oa_ref, ob_ref, wo_ref = next(it), next(it), next(it)
    g_ref, wg_ref, wu_ref, wd_ref = next(it), next(it), next(it), next(it)
    if post:
        fg_ref = next(it)
    if rider_layer is not None:
        qs_ref, ks_ref, vs_ref, ck_hbm, cv_hbm, bias_ref = [next(it) for _ in range(6)]
    o_ref = next(it)
    if rider_layer is not None:
        oas_ref = next(it)
    h_scr = next(it)
    live = {}

    def phase_in():
        x = x_ref[...]
        if pre:
            oa = oa_ref[...].astype(jnp.bfloat16)
            x = x + _bdot(oa, wo_ref[0:A_WIDTH, :]) + _bdot(ob_ref[...], wo_ref[A_WIDTH:, :])
        live["x"] = x
        live["xn"] = _rms(x, g_ref[...]).astype(jnp.bfloat16)

    def hidden(lo, hi):
        def run():
            gate = _bdot(live["xn"], wg_ref[:, lo:hi])
            up = _bdot(live["xn"], wu_ref[:, lo:hi])
            h_scr[:, lo:hi] = (_silu(gate) * up).astype(jnp.bfloat16)
        return run

    def out(lo, hi):
        def run():
            y = live["x"][:, lo:hi] + 0.5 * _bdot(h_scr[...], wd_ref[:, lo:hi])
            if post:
                assert (lo, hi) == (0, D_MODEL)
                y = _rms(y, fg_ref[...])
            o_ref[:, lo:hi] = y
        return run

    if rider_layer is None:
        for op in [phase_in, hidden(0, ff_chunk), hidden(ff_chunk, D_FF), out(0, D_MODEL)]:
            op()
        return

    q1, q2, q3 = 5 * LANES, ff_chunk, ff_chunk + 5 * LANES
    half = D_MODEL // 2
    phases = [[phase_in, hidden(0, q1)], [hidden(q1, q2), hidden(q2, q3)],
              [hidden(q3, D_FF), out(0, half)], [out(half, D_MODEL)]]
    kbuf, vbuf, sem = next(it), next(it), next(it)
    per_step = len(phases)
    assert per_step == RIDER_PHASES and per_step % 2 == 0
    q_len = qs_ref.shape[0] // per_step
    step = pl.program_id(0)

    def fetch(req, slot):
        return [pltpu.make_async_copy(ck_hbm.at[rider_layer, req], kbuf.at[slot], sem.at[slot, 0]),
                pltpu.make_async_copy(cv_hbm.at[rider_layer, req], vbuf.at[slot], sem.at[slot, 1])]

    @pl.when(step == 0)
    def _():
        for cp in fetch(0, 0):
            cp.start()

    own = _own_head_mask(q_len)
    for r, phase in enumerate(phases):
        req, slot = step * per_step + r, r % 2
        for cp in fetch(req, slot):
            cp.wait()
        if r + 1 < per_step:
            for cp in fetch(req + 1, 1 - slot):
                cp.start()
        else:
            @pl.when(step + 1 < pl.num_programs(0))
            def _():
                for cp in fetch(req + 1, 1 - slot):
                    cp.start()
        rows = slice(r * q_len, (r + 1) * q_len)
        s_buf, s_new = _sample_scores(qs_ref[rows, :], ks_ref[rows, :], kbuf.at[slot], bias_ref, own)
        phase[0]()
        oas_ref[rows, :] = _sample_output(s_buf, s_new, vs_ref[rows, :], vbuf.at[slot], own)
        for op in phase[1:]:
            op()


def _ffn(x, g, wg, wu, wd, idx, pre=None, post_g=None, rider=None):
    n = x.shape[0]
    plain = pre is None and rider is None
    tm = FFN_TILE if plain and n >= 2 * FFN_TILE else ROW_TILE
    n_steps = n // tm
    ff_chunk = D_FF // 2
    row = lambda w: pl.BlockSpec((tm, w), lambda i: (i, 0))
    args, specs = [x], [row(D_MODEL)]
    if pre is not None:
        oa, ob, wo, wo_idx = pre
        args += [oa, ob, wo]
        specs += [row(A_WIDTH), row(RET_V_WIDTH), _stack_spec(wo, wo_idx)]
    args += [g.reshape(1, D_MODEL), wg, wu, wd]
    specs += [_const_spec((1, D_MODEL)), _stack_spec(wg, idx), _stack_spec(wu, idx), _stack_spec(wd, idx)]
    if post_g is not None:
        args.append(post_g.reshape(1, D_MODEL))
        specs.append(_const_spec((1, D_MODEL)))
    out_specs, out_shape = row(D_MODEL), jax.ShapeDtypeStruct((n, D_MODEL), jnp.float32)
    scratch = [pltpu.VMEM((tm, D_FF), jnp.bfloat16)]
    semantics, rider_layer = ("parallel",), None
    if rider is not None:
        qs, ks, vs, cache_kt, cache_vt, rider_layer, bias_s, q_len = rider
        per_step = RIDER_PHASES
        assert qs.shape[0] == n_steps * per_step * q_len
        new = pl.BlockSpec((per_step * q_len, A_WIDTH), lambda i: (i, 0))
        hbm = pl.BlockSpec(memory_space=pl.ANY)
        args += [qs, ks, vs, cache_kt, cache_vt, bias_s]
        specs += [new, new, new, hbm, hbm, _const_spec(bias_s.shape)]
        out_specs = [out_specs, new]
        out_shape = [out_shape, jax.ShapeDtypeStruct((qs.shape[0], A_WIDTH), jnp.float32)]
        slot = pltpu.VMEM((2,) + cache_kt.shape[2:], jnp.float32)
        scratch += [slot, slot, pltpu.SemaphoreType.DMA((2, 2))]
        semantics = ("arbitrary",)
    body = functools.partial(_ffn_body, pre=pre is not None, post=post_g is not None, ff_chunk=ff_chunk,
                             rider_layer=rider_layer)
    return pl.pallas_call(
        body,
        grid=(n_steps,),
        in_specs=specs,
        out_specs=out_specs,
        out_shape=out_shape,
        scratch_shapes=scratch,
        compiler_params=_cparams(semantics),
        name="ffn",
    )(*args)


def _rotate(x, cos, sin_signed):
    w = x.shape[-1]
    lane = lax.broadcasted_iota(jnp.int32, x.shape, 1)
    partner = jnp.where((lane & 1) == 0, pltpu.roll(x, w - 1, 1), pltpu.roll(x, 1, 1))
    return x * cos + partner * sin_signed


def _proj_body(x_ref, g_ref, w_ref, cos_ref, sin_ref,
               qa_ref, ka_ref, va_ref, qr_ref, kr_ref, vb_ref, gb_ref):
    xn = _rms(x_ref[...], g_ref[...]).astype(jnp.bfloat16)
    a, qk, v = A_WIDTH, RET_QK_WIDTH, RET_V_WIDTH
    seg = lambda lo, width: _bdot(xn, w_ref[:, lo:lo + width])
    qa_ref[...] = seg(0, a) * (HD_A ** -0.5)
    ka_ref[...] = seg(a, a)
    va_ref[...] = seg(2 * a, a)
    cos, sin = cos_ref[...], sin_ref[...]
    qr_ref[...] = _rotate(seg(3 * a, qk), cos, sin).astype(qr_ref.dtype)
    kr_ref[...] = (_rotate(seg(3 * a + qk, qk), cos, sin) * (DK_B ** -0.5)).astype(kr_ref.dtype)
    vb_ref[...] = seg(3 * a + 2 * qk, v).astype(vb_ref.dtype)
    gb_ref[...] = seg(3 * a + 2 * qk + v, v)


def _proj(x, g, w_in, layer, cos_tab, sin_tab, ret_dtype):
    n = x.shape[0]
    tm = ROW_TILE
    tab_blocks = cos_tab.shape[0] // tm
    row = lambda w: pl.BlockSpec((tm, w), lambda i: (i, 0))
    tab = pl.BlockSpec((tm, RET_QK_WIDTH), lambda i: (i % tab_blocks, 0))
    f32 = jnp.float32
    outs = [(A_WIDTH, f32), (A_WIDTH, f32), (A_WIDTH, f32), (RET_QK_WIDTH, ret_dtype),
            (RET_QK_WIDTH, ret_dtype), (RET_V_WIDTH, ret_dtype), (RET_V_WIDTH, f32)]
    return pl.pallas_call(
        _proj_body,
        grid=(n // tm,),
        in_specs=[row(D_MODEL), _const_spec((1, D_MODEL)), _stack_spec(w_in, (layer,)), tab, tab],
        out_specs=[row(w) for w, _ in outs],
        out_shape=[jax.ShapeDtypeStruct((n, w), dt) for w, dt in outs],
        compiler_params=_cparams(("parallel",)),
        name="mix_proj",
    )(x, g.reshape(1, D_MODEL), w_in, cos_tab, sin_tab)


def _band_unit(q, kb, vb, bias, low, prev=None):
    nq = q.shape[0]
    zero = jnp.zeros_like(q)
    lhs = jnp.concatenate([jnp.where(low, q, zero), jnp.where(low, zero, q)], axis=0)
    s = _bdot_nt(lhs.astype(jnp.bfloat16), kb.astype(jnp.bfloat16)) + bias
    m_cur = jnp.max(s, axis=1, keepdims=True)
    if prev is not None:
        m_pa, m_pb, l_pa, l_pb, acc_prev = prev
        m_prev = jnp.concatenate([m_pa, m_pb], axis=0)
        m_next = jnp.maximum(m_prev, m_cur)
        alpha = jnp.exp(m_prev - m_next)
    else:
        m_next = jnp.broadcast_to(m_cur, (2 * nq, LANES))
    prob = jnp.exp(s - jnp.concatenate([m_next, m_next], axis=1))
    l_next = jnp.sum(prob, axis=1, keepdims=True)
    pv = _bdot(prob.astype(jnp.bfloat16), vb.astype(jnp.bfloat16))
    acc_next = jnp.where(low, pv[:nq], pv[nq:])
    if prev is not None:
        l_next = alpha * jnp.concatenate([l_pa, l_pb], axis=0) + l_next
        acc_next = jnp.where(low, alpha[:nq], alpha[nq:]) * acc_prev + acc_next
    else:
        l_next = jnp.broadcast_to(l_next, (2 * nq, LANES))
    return m_next, l_next, acc_next


NEAR_PATTERNS = DIL_PATTERNS[:2]


def _attn_near_body(q_ref, kp_ref, kc_ref, vp_ref, vc_ref, bias_ref, o_ref, lse_ref,
                    kk, vv, m_a, m_b, l_a, l_b, acc):
    first = (pl.program_id(2) == 0).astype(jnp.int32)
    span_rows = ATT_SPAN
    kk[0:span_rows, :] = kp_ref[...]
    kk[span_rows:, :] = kc_ref[...]
    vv[0:span_rows, :] = vp_ref[...]
    vv[span_rows:, :] = vc_ref[...]
    nq = BAND_Q
    low = lax.broadcasted_iota(jnp.int32, (nq, LANES), 1) < HD_A
    n_units = span_rows // nq

    for p, (_, dil) in enumerate(NEAR_PATTERNS):
        span = nq * dil
        shift = dil.bit_length() - 1
        init = p == 0

        def load(u, p=p, dil=dil, span=span, shift=shift, init=init):
            blk = u >> shift
            base = blk * span + (u & (dil - 1))
            rows = pl.ds(base, nq, stride=dil)
            band = pl.ds(span_rows + base - span, 2 * nq, stride=dil)
            flag = jnp.where(blk == 0, first, 0)
            vals = [q_ref[rows, :], kk[band, :], vv[band, :], bias_ref[p, flag]]
            prev = None if init else (m_a[rows, :], m_b[rows, :], l_a[rows, :], l_b[rows, :], acc[rows, :])
            return rows, vals, prev

        unroll = n_units if init else ATT_UNROLL

        def group(g, carry, load=load, unroll=unroll):
            loaded = [load(g * unroll + j) for j in range(unroll)]
            results = [_band_unit(*vals, low, prev) for _, vals, prev in loaded]
            for (rows, _, _), (m_next, l_next, acc_next) in zip(loaded, results):
                acc[rows, :] = acc_next
                m_a[rows, :] = m_next[:nq]
                m_b[rows, :] = m_next[nq:]
                l_a[rows, :] = l_next[:nq]
                l_b[rows, :] = l_next[nq:]
            return carry

        lax.fori_loop(0, n_units // unroll, group, 0)

    low_all = lax.broadcasted_iota(jnp.int32, acc.shape, 1) < HD_A
    den = jnp.where(low_all, l_a[...], l_b[...])
    o_ref[...] = acc[...] / den
    lse_ref[...] = jnp.where(low_all, m_a[...], m_b[...]) + jnp.log(den)


FAR_DIL = DIL_PATTERNS[-1][1]
FAR_CLASSES = 4
FAR_STREAMS = 7


def _attn_far_body(q_hbm, k_hbm, v_hbm, near_hbm, lse_hbm, bias_ref, o_hbm,
                   qbuf, kbuf, vbuf, nearbuf, lsebuf, obuf, in_sem, out_sem, *, nblk):
    step = pl.program_id(0)
    n_steps = pl.num_programs(0)
    nq = BAND_Q
    per_span = FAR_DIL // FAR_CLASSES

    def locate(s):
        span = s // per_span
        first = (span % nblk) == 0
        return span, jnp.where(first, span, span - 1), (s % per_span) * FAR_CLASSES, first

    def in_copies(s, slot):
        span, prev, r0, _ = locate(s)
        copies = []
        for c in range(FAR_CLASSES):
            r = r0 + c
            pairs = [(q_hbm.at[span, :, r, :], qbuf.at[slot, c]),
                     (k_hbm.at[prev, :, r, :], kbuf.at[slot, c, pl.ds(0, nq), :]),
                     (k_hbm.at[span, :, r, :], kbuf.at[slot, c, pl.ds(nq, nq), :]),
                     (v_hbm.at[prev, :, r, :], vbuf.at[slot, c, pl.ds(0, nq), :]),
                     (v_hbm.at[span, :, r, :], vbuf.at[slot, c, pl.ds(nq, nq), :]),
                     (near_hbm.at[span, :, r, :], nearbuf.at[slot, c]),
                     (lse_hbm.at[span, :, r, :], lsebuf.at[slot, c])]
            copies += [pltpu.make_async_copy(src, dst, in_sem.at[slot, c, i])
                       for i, (src, dst) in enumerate(pairs)]
        return copies

    def out_copies(s, slot):
        span, _, r0, _ = locate(s)
        return [pltpu.make_async_copy(obuf.at[slot, c], o_hbm.at[span, :, r0 + c, :], out_sem.at[slot, c])
                for c in range(FAR_CLASSES)]

    slot = step % 2

    @pl.when(step == 0)
    def _():
        for cp in in_copies(0, 0):
            cp.start()

    @pl.when(step + 1 < n_steps)
    def _():
        for cp in in_copies(step + 1, 1 - slot):
            cp.start()

    for cp in in_copies(step, slot):
        cp.wait()

    @pl.when(step >= 2)
    def _():
        for cp in out_copies(step - 2, slot):
            cp.wait()

    first = locate(step)[3].astype(jnp.int32)
    low = lax.broadcasted_iota(jnp.int32, (nq, LANES), 1) < HD_A
    for c in range(FAR_CLASSES):
        for pair in range(H_A // 2):
            lanes = slice(pair * LANES, (pair + 1) * LANES)
            m_far, l_far, acc_far = _band_unit(qbuf[slot, c, :, lanes], kbuf[slot, c, :, lanes],
                                               vbuf[slot, c, :, lanes], bias_ref[pair, first], low)
            m_far = jnp.where(low, m_far[:nq], m_far[nq:])
            l_far = jnp.where(low, l_far[:nq], l_far[nq:])
            lse_near = lsebuf[slot, c, :, lanes]
            m_all = jnp.maximum(lse_near, m_far)
            w_near = jnp.exp(lse_near - m_all)
            w_far = jnp.exp(m_far - m_all)
            obuf[slot, c, :, lanes] = ((w_near * nearbuf[slot, c, :, lanes] + w_far * acc_far)
                                       / (w_near + w_far * l_far))

    for cp in out_copies(step, slot):
        cp.start()

    @pl.when(step == n_steps - 1)
    def _():
        for cp in out_copies(step, slot) + out_copies(step - 1, 1 - slot):
            cp.wait()


def _prompt_bias(t5_table):
    n = BAND_Q
    after_start = np.arange(2 * n)[None, None, :] >= n
    per_pattern = []
    for _, dil in DIL_PATTERNS:
        per_step = t5_table[_t5_bucket(np.arange(n + 1) * dil)].T.astype(jnp.float32)
        diag = jnp.concatenate([jnp.full((H_A, n - 1), NEG, jnp.float32), per_step[:, ::-1],
                                jnp.full((H_A, n), NEG, jnp.float32)], axis=1)
        skew = jnp.broadcast_to(diag[:, None, :], (H_A, n, 3 * n)).reshape(H_A, -1)
        skew = skew[:, :n * (3 * n - 1)].reshape(H_A, n, 3 * n - 1)
        full = skew[:, :, n - 1:3 * n - 1]
        start = jnp.where(after_start, full, NEG)
        per_pattern.append(jnp.stack([full, start], axis=0))
    b = jnp.stack(per_pattern, axis=0)
    b = b.reshape(len(DIL_PATTERNS), 2, H_A // 2, 2 * n, 2 * n)
    return jnp.transpose(b, (2, 0, 1, 3, 4))


def _attn_prompt(qa, ka, va, bias, batch, seq):
    assert ATT_SPAN == BAND_Q * FAR_DIL and FAR_DIL % FAR_CLASSES == 0
    nblk = seq // ATT_SPAN
    n = qa.shape[0]
    cur = pl.BlockSpec((ATT_SPAN, LANES), lambda hp, b, j: (b * nblk + j, hp))
    prev = pl.BlockSpec((ATT_SPAN, LANES), lambda hp, b, j: (b * nblk + jnp.maximum(j - 1, 0), hp))
    n_near = len(NEAR_PATTERNS)
    bias_near, bias_far = bias[:, :n_near], bias[:, n_near]
    bias_spec = pl.BlockSpec((None,) + bias_near.shape[1:], lambda hp, b, j: (hp, 0, 0, 0, 0))
    f32 = jnp.float32
    o_near, lse_near = pl.pallas_call(
        _attn_near_body,
        grid=(H_A // 2, batch, nblk),
        in_specs=[cur, prev, cur, prev, cur, bias_spec],
        out_specs=[cur, cur],
        out_shape=[jax.ShapeDtypeStruct((n, A_WIDTH), f32)] * 2,
        scratch_shapes=[pltpu.VMEM((2 * ATT_SPAN, LANES), f32), pltpu.VMEM((2 * ATT_SPAN, LANES), f32)]
        + [pltpu.VMEM((ATT_SPAN, LANES), f32)] * 5,
        compiler_params=_cparams(("parallel", "parallel", "parallel")),
        name="attn_near",
    )(qa, ka, ka, va, va, bias_near)

    spans = batch * nblk
    view = lambda a: a.reshape(spans, BAND_Q, FAR_DIL, A_WIDTH)
    n_steps = spans * FAR_DIL // FAR_CLASSES
    assert n_steps >= 2
    hbm = pl.BlockSpec(memory_space=pl.ANY)
    slots = lambda rows: pltpu.VMEM((2, FAR_CLASSES, rows, A_WIDTH), f32)
    out = pl.pallas_call(
        functools.partial(_attn_far_body, nblk=nblk),
        grid=(n_steps,),
        in_specs=[hbm] * 5 + [_const_spec(bias_far.shape)],
        out_specs=hbm,
        out_shape=jax.ShapeDtypeStruct((spans, BAND_Q, FAR_DIL, A_WIDTH), f32),
        scratch_shapes=[slots(BAND_Q), slots(2 * BAND_Q), slots(2 * BAND_Q), slots(BAND_Q), slots(BAND_Q),
                        slots(BAND_Q), pltpu.SemaphoreType.DMA((2, FAR_CLASSES, FAR_STREAMS)),
                        pltpu.SemaphoreType.DMA((2, FAR_CLASSES))],
        compiler_params=_cparams(("arbitrary",)),
        name="attn_far",
    )(view(qa), view(ka), view(va), view(o_near), view(lse_near), bias_far)
    return out.reshape(n, A_WIDTH)


def _ret_tables(length):
    lg = jnp.log(1.0 - 2.0 ** (-5.0 - jnp.arange(H_B, dtype=jnp.float32)))
    i = jnp.arange(length, dtype=jnp.float32)
    diff = i[:, None] - i[None, :]
    dmask = jnp.where(diff >= 0, jnp.exp(jnp.maximum(diff, 0.0)[None] * lg[:, None, None]), 0.0)
    dq = jnp.exp((i[:, None] + 1.0) * lg[None, :])
    dq = jnp.broadcast_to(dq.T[:, :, None], (H_B, length, DV_B))
    kd = jnp.exp((length - 1.0 - i)[:, None] * lg[None, :])
    kdec = jnp.repeat(kd, DK_B, axis=1).reshape(length, H_B // 2, 2 * DK_B).transpose(1, 0, 2)
    sd = jnp.exp(length * lg)
    sdec = jnp.broadcast_to(jnp.repeat(sd, DK_B).reshape(H_B // 2, 2 * DK_B, 1), (H_B // 2, 2 * DK_B, DV_B))
    return dmask, dq, kdec, sdec


def _norm_gate(o, gate):
    on = o * lax.rsqrt(jnp.mean(o * o, axis=-1, keepdims=True) + RMS_EPS)
    return _silu(gate) * on


def _ret_prompt_body(q_ref, k_ref, v_ref, g_ref, dmask_ref, dq_ref, kdec_ref, sdec_ref,
                     o_ref, s_out_ref, state):
    step = pl.program_id(1)

    @pl.when(step == 0)
    def _():
        state[...] = jnp.zeros(state.shape, jnp.float32)

    c_len = RET_CHUNK
    low = lax.broadcasted_iota(jnp.int32, (c_len, LANES), 1) < DK_B
    top = lax.broadcasted_iota(jnp.int32, (2 * DK_B, DV_B), 0) < DK_B

    def chunk(c, s_pairs):
        rows = pl.ds(pl.multiple_of(c * c_len, c_len), c_len)
        s_next = []
        for pair in range(H_B // 2):
            lanes = slice(pair * LANES, (pair + 1) * LANES)
            q2 = q_ref[rows, lanes]
            k2 = k_ref[rows, lanes]
            s_old = s_pairs[pair]
            kt = (k2.astype(jnp.float32) * kdec_ref[pair]).T.astype(jnp.bfloat16)
            zero = jnp.zeros_like(q2)
            qm = jnp.concatenate([jnp.where(low, q2, zero), jnp.where(low, zero, q2)], axis=0)
            vl = slice(2 * pair * DV_B, 2 * (pair + 1) * DV_B)
            v2 = v_ref[rows, vl]
            vz = jnp.zeros((c_len, DV_B), v2.dtype)
            v_diag = jnp.concatenate([jnp.concatenate([v2[:, :DV_B], vz], axis=1),
                                      jnp.concatenate([vz, v2[:, DV_B:]], axis=1)], axis=0)
            a = (_bdot_nt(qm, k2) * dmask_ref[pair]).astype(jnp.bfloat16)
            intra = _bdot(jnp.concatenate([a[:c_len], a[c_len:]], axis=1), v_diag)
            cross = _bdot(qm, s_old.astype(jnp.bfloat16))
            o = intra + jnp.concatenate([cross[:c_len], cross[c_len:]], axis=1) * dq_ref[pair]
            gate = g_ref[rows, vl]
            o_ref[rows, vl] = jnp.concatenate(
                [_norm_gate(o[:, :DV_B], gate[:, :DV_B]), _norm_gate(o[:, DV_B:], gate[:, DV_B:])],
                axis=1).astype(o_ref.dtype)
            upd = _bdot(kt, v2)
            s_next.append(s_old * sdec_ref[pair] + jnp.where(top, upd[:, :DV_B], upd[:, DV_B:]))
        return s_next

    def group(gi, carry):
        s_pairs = [state[pair] for pair in range(H_B // 2)]
        for j in range(RET_UNROLL):
            s_pairs = chunk(gi * RET_UNROLL + j, s_pairs)
        for pair in range(H_B // 2):
            state[pair] = s_pairs[pair]
        return carry

    lax.fori_loop(0, q_ref.shape[0] // (c_len * RET_UNROLL), group, 0)

    @pl.when(step == pl.num_programs(1) - 1)
    def _():
        for h in range(H_B):
            s_out_ref[h] = state[h // 2, (h % 2) * DK_B:(h % 2 + 1) * DK_B, :]


def _ret_prompt(qr, kr, vb, gb, batch, seq):
    rows = 1024
    nblk = seq // rows
    n = qr.shape[0]
    dmask, dq, kdec, sdec = _ret_tables(RET_CHUNK)
    dmask = dmask.reshape(H_B // 2, 2 * RET_CHUNK, RET_CHUNK)
    dq = dq.reshape(H_B // 2, 2, RET_CHUNK, DV_B).transpose(0, 2, 1, 3).reshape(H_B // 2, RET_CHUNK, 2 * DV_B)
    row = lambda w: pl.BlockSpec((rows, w), lambda b, j: (b * nblk + j, 0))
    return pl.pallas_call(
        _ret_prompt_body,
        grid=(batch, nblk),
        in_specs=[row(RET_QK_WIDTH), row(RET_QK_WIDTH), row(RET_V_WIDTH), row(RET_V_WIDTH),
                  _const_spec(dmask.shape), _const_spec(dq.shape), _const_spec(kdec.shape),
                  _const_spec(sdec.shape)],
        out_specs=[row(RET_V_WIDTH),
                   pl.BlockSpec((None, H_B, DK_B, DV_B), lambda b, j: (b, 0, 0, 0))],
        out_shape=[jax.ShapeDtypeStruct((n, RET_V_WIDTH), jnp.bfloat16),
                   jax.ShapeDtypeStruct((batch, H_B, DK_B, DV_B), jnp.float32)],
        scratch_shapes=[pltpu.VMEM((H_B // 2, 2 * DK_B, DV_B), jnp.float32)],
        compiler_params=_cparams(("parallel", "arbitrary")),
        name="ret_prompt",
    )(qr, kr, vb, gb, dmask, dq, kdec, sdec)


def _sample_bias(t5_table, w_buf, q_len):
    dist = np.arange(w_buf + q_len)
    by_bucket = t5_table[_t5_bucket(dist)].T.astype(jnp.float32)
    merged = None
    for win, dil in DIL_PATTERNS:
        ok = (dist % dil == 0) & (dist // dil <= win // dil)
        b = jnp.where(ok[None], by_bucket, -jnp.inf)
        merged = b if merged is None else jnp.logaddexp(merged, b)
    merged = jnp.where(jnp.isfinite(merged), merged, NEG)
    rev = jnp.concatenate([jnp.full((H_A, q_len - 1), NEG, jnp.float32), merged], axis=1)[:, ::-1]
    per_query = jnp.stack([rev[:, q_len - 1 - i:q_len - 1 - i + w_buf + q_len] for i in range(q_len)], axis=1)
    pad = jnp.full((H_A, q_len, LANES - q_len), NEG, jnp.float32)
    return jnp.concatenate([per_query, pad], axis=2).reshape(H_A * q_len, w_buf + LANES)


def _own_head_mask(q_len):
    rows = H_A * q_len
    row_head = jnp.right_shift(lax.broadcasted_iota(jnp.int32, (rows, A_WIDTH), 0), q_len.bit_length() - 1)
    lane_head = jnp.right_shift(lax.broadcasted_iota(jnp.int32, (rows, A_WIDTH), 1), HD_A.bit_length() - 1)
    return row_head == lane_head


def _sample_scores(q, k_new, kt_ref, bias_ref, own):
    q_len = q.shape[0]
    w_buf = kt_ref.shape[-1]
    bf16 = jnp.bfloat16
    q_rep = jnp.concatenate([q] * H_A, axis=0)
    lhs = jnp.where(own, q_rep, 0.0).astype(bf16)
    padz = jnp.zeros((LANES - q_len, A_WIDTH), jnp.float32)
    k_new = jnp.concatenate([k_new, padz], axis=0).astype(bf16)
    kt = kt_ref[...].reshape(A_WIDTH, w_buf).astype(bf16)
    s_buf = _bdot(lhs, kt) + bias_ref[:, 0:w_buf]
    s_new = _bdot_nt(lhs, k_new) + bias_ref[:, w_buf:]
    return s_buf, s_new


def _sample_output(s_buf, s_new, v_new, vt_ref, own):
    q_len = v_new.shape[0]
    w_buf = vt_ref.shape[-1]
    bf16 = jnp.bfloat16
    padz = jnp.zeros((LANES - q_len, A_WIDTH), jnp.float32)
    v_new = jnp.concatenate([v_new, padz], axis=0).astype(bf16)
    vt = vt_ref[...].reshape(A_WIDTH, w_buf).astype(bf16)
    m = jnp.maximum(jnp.max(s_buf, axis=1, keepdims=True), jnp.max(s_new, axis=1, keepdims=True))
    p_buf = jnp.exp(s_buf - m)
    p_new = jnp.exp(s_new - m)
    den = jnp.sum(p_buf, axis=1, keepdims=True) + jnp.sum(p_new, axis=1, keepdims=True)
    o_all = (_bdot_nt(p_buf.astype(bf16), vt) + _bdot(p_new.astype(bf16), v_new)) / den
    o_all = jnp.where(own, o_all, 0.0)
    out = o_all[0:q_len]
    for h in range(1, H_A):
        out = out + o_all[h * q_len:(h + 1) * q_len]
    return out


RET_GROUP = 16


def _ret_sample_body(q_ref, k_ref, v_ref, g_ref, s_ref, dmask_ref, dq_ref, kdec_ref, sdec_ref,
                     o_ref, s_out_ref, *, q_len):
    rows = q_ref.shape[0]
    low = lax.broadcasted_iota(jnp.int32, (rows, LANES), 1) < DK_B
    top = lax.broadcasted_iota(jnp.int32, (2 * DK_B, DV_B), 0) < DK_B
    col = lax.broadcasted_iota(jnp.int32, (2 * DK_B, rows), 1)
    for pair in range(H_B // 2):
        lanes = slice(pair * LANES, (pair + 1) * LANES)
        q2 = q_ref[:, lanes]
        k2 = k_ref[:, lanes]
        kt = (k2 * kdec_ref[pair]).T
        zero = jnp.zeros_like(q2)
        upd = [[], []]
        for hh in range(2):
            h = 2 * pair + hh
            vl = slice(h * DV_B, (h + 1) * DV_B)
            qm = jnp.where(low, q2, zero) if hh == 0 else jnp.where(low, zero, q2)
            vh = v_ref[:, vl]
            a = _bdot_nt(qm, k2) * dmask_ref[h]
            cross = jnp.concatenate(
                [_bdot(qm[r * q_len:(r + 1) * q_len], s_ref[r, pair]) for r in range(RET_GROUP)], axis=0)
            o = _bdot(a, vh) + cross * dq_ref[h]
            o_ref[:, vl] = _norm_gate(o, g_ref[:, vl]).astype(o_ref.dtype)
            for r in range(RET_GROUP):
                upd[hh].append(_bdot(jnp.where((col >= r * q_len) & (col < (r + 1) * q_len), kt, 0.0), vh))
        for r in range(RET_GROUP):
            s_out_ref[r, pair] = s_ref[r, pair] * sdec_ref[pair] + jnp.where(top, upd[0][r], upd[1][r])


def _ret_sample(qr, kr, vb, gb, state, layer, q_len):
    n = qr.shape[0]
    batch = n // q_len
    rows = RET_GROUP * q_len
    dmask1, dq1, kdec1, sdec = _ret_tables(q_len)
    eye = jnp.eye(RET_GROUP, dtype=jnp.float32)
    dmask = jnp.einsum("rs,hij->hrisj", eye, dmask1).reshape(H_B, rows, rows)
    dq = jnp.tile(dq1, (1, RET_GROUP, 1))
    kdec = jnp.tile(kdec1, (1, RET_GROUP, 1))
    row = lambda w: pl.BlockSpec((rows, w), lambda i: (i, 0))
    st_in = pl.BlockSpec((None, RET_GROUP, H_B // 2, 2 * DK_B, DV_B), lambda i: (layer, i, 0, 0, 0))
    st_out = pl.BlockSpec((RET_GROUP, H_B // 2, 2 * DK_B, DV_B), lambda i: (i, 0, 0, 0))
    return pl.pallas_call(
        functools.partial(_ret_sample_body, q_len=q_len),
        grid=(batch // RET_GROUP,),
        in_specs=[row(RET_QK_WIDTH), row(RET_QK_WIDTH), row(RET_V_WIDTH), row(RET_V_WIDTH), st_in,
                  _const_spec(dmask.shape), _const_spec(dq.shape), _const_spec(kdec.shape),
                  _const_spec(sdec.shape)],
        out_specs=[row(RET_V_WIDTH), st_out],
        out_shape=[jax.ShapeDtypeStruct((n, RET_V_WIDTH), jnp.bfloat16),
                   jax.ShapeDtypeStruct((batch, H_B // 2, 2 * DK_B, DV_B), jnp.float32)],
        compiler_params=_cparams(("parallel",)),
        name="ret_sample",
    )(qr, kr, vb, gb, state, dmask, dq, kdec, sdec)


CONV_PAD = 32
CONV_BLK = 64
CONV_LANES = 128
CONV_SPLIT = 2


def _glu(x, g, w1_ref, b1_ref):
    xn = _rms(x, g).astype(jnp.bfloat16)
    a = _bdot(xn, w1_ref[:, 0:D_MODEL]) + b1_ref[:, 0:D_MODEL]
    gate = _bdot(xn, w1_ref[:, D_MODEL:]) + b1_ref[:, D_MODEL:]
    return a / (1.0 + jnp.exp(-gate))


def _conv_tail(x, c, lng_ref, lnb_ref, w2_ref, b2_ref):
    mu = jnp.mean(c, axis=-1, keepdims=True)
    d = c - mu
    var = jnp.mean(d * d, axis=-1, keepdims=True)
    nrm = d * lax.rsqrt(var + LN_EPS) * lng_ref[...] + lnb_ref[...]
    return x + _bdot(_silu(nrm).astype(jnp.bfloat16), w2_ref[...]) + b2_ref[...]


def _conv_prompt_body(x_ref, g_ref, w1_ref, b1_ref, wdw_ref, bdw_ref, lng_ref, lnb_ref, w2_ref, b2_ref,
                      shift_ref, o_ref, buf_ref, ext, cbuf):
    tm = x_ref.shape[0]
    hist = CONV_W - 1

    @pl.when(pl.program_id(1) == 0)
    def _():
        ext[0:CONV_PAD, :] = jnp.zeros((CONV_PAD, D_MODEL), jnp.float32)

    blk, lane_blk, off = CONV_BLK, CONV_LANES, CONV_PAD - hist

    def depthwise(lo, hi):
        for r0 in range(lo, hi, blk):
            for lb in range(D_MODEL // lane_blk):
                lanes = slice(lb * lane_blk, (lb + 1) * lane_blk)
                win = ext[r0:r0 + blk + CONV_PAD, lanes]
                acc = jnp.broadcast_to(bdw_ref[:, lanes], (blk, lane_blk))
                shifted = []
                for r in range(SUBLANES):
                    taps = [k for k in range(CONV_W) if (k + off) % SUBLANES == r]
                    rows = blk if r == 0 else blk + SUBLANES
                    y = None
                    for k in taps:
                        a0 = k + off - r
                        tiles = win[a0:a0 + rows].reshape(rows // SUBLANES, SUBLANES, lane_blk)
                        term = (tiles * wdw_ref[k, :, lanes]).reshape(rows, lane_blk)
                        y = term if y is None else y + term
                    if r == 0:
                        acc = acc + y
                    else:
                        shifted.append(y)
                shifted.append(jnp.zeros((SUBLANES, lane_blk), jnp.float32))
                stacked = jnp.concatenate(shifted, axis=0).astype(jnp.bfloat16)
                cbuf[r0:r0 + blk, lanes] = acc + _bdot(shift_ref[...], stacked)

    part = tm // CONV_SPLIT
    g = g_ref[...]
    for h in range(CONV_SPLIT):
        rows = slice(h * part, (h + 1) * part)
        ext[CONV_PAD + h * part:CONV_PAD + (h + 1) * part, :] = _glu(x_ref[rows, :], g, w1_ref, b1_ref)
    for h in range(CONV_SPLIT):
        rows = slice(h * part, (h + 1) * part)
        depthwise(h * part, (h + 1) * part)
        o_ref[rows, :] = _conv_tail(x_ref[rows, :], cbuf[rows, :], lng_ref, lnb_ref, w2_ref, b2_ref)
    buf_ref[...] = ext[tm + CONV_PAD - hist:, :]
    ext[0:CONV_PAD, :] = ext[tm:, :]


def _conv_prompt(x, g, w1, b1, wdw, bdw, lng, lnb, w2, b2, layer, batch, seq):
    tm = ROW_TILE
    nblk = seq // tm
    n = x.shape[0]
    row = pl.BlockSpec((tm, D_MODEL), lambda b, j: (b * nblk + j, 0))
    vec = lambda a: a.reshape(1, -1)
    f32 = jnp.float32
    wdw = jnp.broadcast_to(wdw[:, None, :], (CONV_W, SUBLANES, D_MODEL))
    part = CONV_BLK + SUBLANES
    shift = np.zeros((CONV_BLK, SUBLANES * part), np.float32)
    for r in range(1, SUBLANES):
        shift[np.arange(CONV_BLK), (r - 1) * part + np.arange(CONV_BLK) + r] = 1.0
    shift = jnp.asarray(shift[:, :(SUBLANES - 1) * part + SUBLANES], jnp.bfloat16)
    return pl.pallas_call(
        _conv_prompt_body,
        grid=(batch, nblk),
        in_specs=[row, _const_spec((1, D_MODEL)), _stack_spec(w1, (layer,)), _const_spec((1, 2 * D_MODEL)),
                  _const_spec(wdw.shape), _const_spec((1, D_MODEL)), _const_spec((1, D_MODEL)),
                  _const_spec((1, D_MODEL)), _stack_spec(w2, (layer,)), _const_spec((1, D_MODEL)),
                  _const_spec(shift.shape)],
        out_specs=[row, pl.BlockSpec((None, CONV_W - 1, D_MODEL), lambda b, j: (b, 0, 0))],
        out_shape=[jax.ShapeDtypeStruct((n, D_MODEL), f32),
                   jax.ShapeDtypeStruct((batch, CONV_W - 1, D_MODEL), f32)],
        scratch_shapes=[pltpu.VMEM((tm + CONV_PAD, D_MODEL), f32), pltpu.VMEM((tm, D_MODEL), f32)],
        compiler_params=_cparams(("parallel", "arbitrary")),
        name="conv_prompt",
    )(x, vec(g), w1, vec(b1), wdw, vec(bdw), vec(lng), vec(lnb), w2, vec(b2), shift)


CONV_GROUP = 32


def _conv_sample_body(x_ref, g_ref, w1_ref, b1_ref, wdw_ref, bdw_ref, lng_ref, lnb_ref, w2_ref, b2_ref,
                      st_ref, o_ref, buf_ref, ext, cbuf, *, q_len):
    hist = CONV_W - 1
    x = x_ref[...]
    glu = _glu(x, g_ref[...], w1_ref, b1_ref)
    for r in range(CONV_GROUP):
        ext[r, 0:hist, :] = st_ref[r]
        ext[r, hist:hist + q_len, :] = glu[r * q_len:(r + 1) * q_len]
    for r in range(CONV_GROUP):
        acc = jnp.broadcast_to(bdw_ref[...], (q_len, D_MODEL))
        for k in range(CONV_W):
            acc = acc + ext[r, k:k + q_len, :] * wdw_ref[k:k + 1, :]
        cbuf[r * q_len:(r + 1) * q_len, :] = acc
        buf_ref[r] = ext[r, q_len:q_len + hist, :]
    o_ref[...] = _conv_tail(x, cbuf[...], lng_ref, lnb_ref, w2_ref, b2_ref)


def _conv_sample(x, g, w1, b1, wdw, bdw, lng, lnb, w2, b2, state, layer, q_len):
    n = x.shape[0]
    batch = n // q_len
    rows = CONV_GROUP * q_len
    hist = CONV_W - 1
    row = pl.BlockSpec((rows, D_MODEL), lambda i: (i, 0))
    vec = lambda a: a.reshape(1, -1)
    f32 = jnp.float32
    return pl.pallas_call(
        functools.partial(_conv_sample_body, q_len=q_len),
        grid=(batch // CONV_GROUP,),
        in_specs=[row, _const_spec((1, D_MODEL)), _stack_spec(w1, (layer,)), _const_spec((1, 2 * D_MODEL)),
                  _const_spec(wdw.shape), _const_spec((1, D_MODEL)), _const_spec((1, D_MODEL)),
                  _const_spec((1, D_MODEL)), _stack_spec(w2, (layer,)), _const_spec((1, D_MODEL)),
                  pl.BlockSpec((None, CONV_GROUP, hist, D_MODEL), lambda i: (layer, i, 0, 0))],
        out_specs=[row, pl.BlockSpec((CONV_GROUP, hist, D_MODEL), lambda i: (i, 0, 0))],
        out_shape=[jax.ShapeDtypeStruct((n, D_MODEL), f32),
                   jax.ShapeDtypeStruct((batch, hist, D_MODEL), f32)],
        scratch_shapes=[pltpu.VMEM((CONV_GROUP, hist + q_len + 2, D_MODEL), f32),
                        pltpu.VMEM((rows, D_MODEL), f32)],
        compiler_params=_cparams(("parallel",)),
        name="conv_sample",
    )(x, vec(g), w1, vec(b1), wdw, vec(bdw), vec(lng), vec(lnb), w2, vec(b2), state)


def _t5_bucket(dist):
    dist = np.asarray(dist).astype(np.int32)
    max_exact = T5_BUCKETS // 2
    log_ratio = np.log(np.maximum(dist, 1) / max_exact) / math.log(T5_MAX_DIST / max_exact)
    large = np.minimum(max_exact + (log_ratio * (T5_BUCKETS - max_exact)).astype(np.int32), T5_BUCKETS - 1)
    return np.where(dist < max_exact, dist, large).astype(np.int32)


def _rot_tables(pos):
    half = DK_B // 2
    ang = 1.0 / (10000.0 ** jnp.linspace(0.0, 1.0, half, dtype=jnp.float32))
    th = pos.astype(jnp.float32)[:, None] * ang[None, :]
    cos = jnp.repeat(jnp.cos(th), 2, axis=1)
    sin = jnp.repeat(jnp.sin(th), 2, axis=1) * jnp.tile(jnp.array([-1.0, 1.0], jnp.float32), half)[None, :]
    return jnp.tile(cos, (1, H_B)), jnp.tile(sin, (1, H_B))


def kernel(x_prompt, x_sample, cache_win_k, cache_win_v, state_ret, state_conv, t5_table, rms_g, final_g,
           w_ffn_gate, w_ffn_up, w_ffn_down, w_mix_in, w_mix_out, w_pw1, b_pw1, w_dw, b_dw, ln_g, ln_b,
           w_pw2, b_pw2):
    batch, seq, _ = x_prompt.shape
    dec_batch, dec_seq, _ = x_sample.shape
    depth = rms_g.shape[0]
    n_even = cache_win_k.shape[0]
    w_buf = cache_win_k.shape[2]
    keep = min(DIL_PATTERNS[-1][0], seq)
    bf16 = jnp.bfloat16

    xp = x_prompt.reshape(batch * seq, D_MODEL)
    xs = x_sample.reshape(dec_batch * dec_seq, D_MODEL)

    wg, wu, wd = w_ffn_gate.astype(bf16), w_ffn_up.astype(bf16), w_ffn_down.astype(bf16)
    w_in, w_out = w_mix_in.astype(bf16), w_mix_out.astype(bf16)
    w1, w2 = w_pw1.astype(bf16), w_pw2.astype(bf16)

    cos_p, sin_p = _rot_tables(jnp.arange(seq))
    cos_s, sin_s = _rot_tables(PAST_LEN + jnp.arange(ROW_TILE) % dec_seq)
    bias_p = _prompt_bias(t5_table)
    bias_s = _sample_bias(t5_table, w_buf, dec_seq)
    assert dec_seq & (dec_seq - 1) == 0 and dec_seq <= LANES
    ck = jnp.transpose(cache_win_k, (0, 1, 3, 4, 2))
    cv = jnp.transpose(cache_win_v, (0, 1, 3, 4, 2))
    st_ret = state_ret.reshape(n_even, dec_batch, H_B // 2, 2 * DK_B, DV_B)

    kp_l, vp_l, sp_l, cp_l, ks_l, vs_l, ss_l, cs_l = [], [], [], [], [], [], [], []
    for layer in range(depth):
        g = rms_g[layer]
        xp = _ffn(xp, g[0], wg, wu, wd, (layer, 0))
        xs = _ffn(xs, g[0], wg, wu, wd, (layer, 0))
        last = layer == depth - 1
        post = final_g if last else None
        ffn_b = (g[2], wg, wu, wd, (layer, 1))
        if layer % 2 == 0:
            e = layer // 2
            qa, ka, va, qr, kr, vb, gb = _proj(xp, g[1], w_in, e, cos_p, sin_p, bf16)
            oa = _attn_prompt(qa, ka, va, bias_p, batch, seq)
            ob, s_end = _ret_prompt(qr, kr, vb, gb, batch, seq)
            kp_l.append(ka.reshape(batch, seq, A_WIDTH)[:, -keep:].reshape(batch, keep, H_A, HD_A))
            vp_l.append(va.reshape(batch, seq, A_WIDTH)[:, -keep:].reshape(batch, keep, H_A, HD_A))
            sp_l.append(s_end)

            qa, ka, va, qr, kr, vb, gb = _proj(xs, g[1], w_in, e, cos_s, sin_s, jnp.float32)
            xp, oa = _ffn(xp, *ffn_b, pre=(oa, ob, w_out, (e,)), post_g=post,
                          rider=(qa, ka, va, ck, cv, e, bias_s, dec_seq))
            ob, s_new = _ret_sample(qr, kr, vb, gb, st_ret, e, dec_seq)
            xs = _ffn(xs, *ffn_b, pre=(oa, ob, w_out, (e,)), post_g=post)
            ks_l.append(ka.reshape(dec_batch, dec_seq, H_A, HD_A))
            vs_l.append(va.reshape(dec_batch, dec_seq, H_A, HD_A))
            ss_l.append(s_new.reshape(dec_batch, H_B, DK_B, DV_B))
        else:
            o = layer // 2
            conv = (g[1], w1, b_pw1[o], w_dw[o], b_dw[o], ln_g[o], ln_b[o], w2, b_pw2[o])
            xp, cp = _conv_prompt(xp, *conv, o, batch, seq)
            xs, cs = _conv_sample(xs, *conv, state_conv, o, dec_seq)
            cp_l.append(cp)
            cs_l.append(cs)
            xp = _ffn(xp, *ffn_b, post_g=post)
            xs = _ffn(xs, *ffn_b, post_g=post)

    stk = jnp.stack
    return (xp.reshape(batch, seq, D_MODEL), xs.reshape(dec_batch, dec_seq, D_MODEL),
            stk(kp_l), stk(vp_l), stk(sp_l), stk(cp_l), stk(ks_l), stk(vs_l), stk(ss_l), stk(cs_l))
```

```python
import functools
import math

import jax
import jax.numpy as jnp
import numpy as np
from jax import lax
from jax.experimental import pallas as pl
from jax.experimental.pallas import tpu as pltpu

D_MODEL = 1024
D_FF = 2816
H_A, HD_A = 8, 64
A_WIDTH = H_A * HD_A
DIL_PATTERNS = ((128, 1), (512, 4), (2048, 16))
H_B, DK_B, DV_B = 4, 64, 128
RET_QK_WIDTH = H_B * DK_B
RET_V_WIDTH = H_B * DV_B
RET_CHUNK = 128
PROJ_COLS = 3 * A_WIDTH + 2 * RET_QK_WIDTH + 2 * RET_V_WIDTH
CONV_W = 31
T5_BUCKETS = 32
T5_MAX_DIST = 2048
PAST_LEN = 2048
RMS_EPS = 1e-6
LN_EPS = 1e-5

LANES = 128
SUBLANES = 8
VMEM_LIMIT_BYTES = 56 * 1024 * 1024

NEG = -1e30
BAND_Q = 128
ATT_SPAN = 2048
RIDER_PHASES = 4
RET_UNROLL = 8
ATT_UNROLL = 16
ROW_TILE = 512
FFN_TILE = 1024


def _cparams(sem):
    return pltpu.CompilerParams(dimension_semantics=sem, vmem_limit_bytes=VMEM_LIMIT_BYTES)


def _const_spec(shape):
    nd = len(shape)
    return pl.BlockSpec(shape, lambda *_: (0,) * nd, pipeline_mode=pl.Buffered(1))


def _stack_spec(stacked, idx):
    tail = stacked.shape[len(idx):]
    return pl.BlockSpec((None,) * len(idx) + tail, lambda *_: tuple(idx) + (0,) * len(tail),
                        pipeline_mode=pl.Buffered(1))


def _rms(x, g):
    y = x * lax.rsqrt(jnp.mean(x * x, axis=-1, keepdims=True) + RMS_EPS)
    return y * g


def _silu(x):
    return x / (1.0 + jnp.exp(-x))


def _bdot(a, b):
    return jnp.dot(a, b, preferred_element_type=jnp.float32)


def _bdot_nt(a, b):
    return lax.dot_general(a, b, (((1,), (1,)), ((), ())), preferred_element_type=jnp.float32)


def _ffn_body(*refs, pre, post, ff_chunk, rider_layer):
    it = iter(refs)
    x_ref = next(it)
    if pre:
        oa_ref, ob_ref, wo_ref = next(it), next(it), next(it)
    g_ref, wg_ref, wu_ref, wd_ref = next(it), next(it), next(it), next(it)
    if post:
        fg_ref = next(it)
    if rider_layer is not None:
        qs_ref, ks_ref, vs_ref, ck_hbm, cv_hbm, bias_ref = [next(it) for _ in range(6)]
    o_ref = next(it)
    if rider_layer is not None:
        oas_ref = next(it)
    h_scr = next(it)
    live = {}

    def phase_in():
        x = x_ref[...]
        if pre:
            oa = oa_ref[...].astype(jnp.bfloat16)
            x = x + _bdot(oa, wo_ref[0:A_WIDTH, :]) + _bdot(ob_ref[...], wo_ref[A_WIDTH:, :])
        live["x"] = x
        live["xn"] = _rms(x, g_ref[...]).astype(jnp.bfloat16)

    def hidden(lo, hi):
        def run():
            gate = _bdot(live["xn"], wg_ref[:, lo:hi])
            up = _bdot(live["xn"], wu_ref[:, lo:hi])
            h_scr[:, lo:hi] = (_silu(gate) * up).astype(jnp.bfloat16)
        return run

    def out(lo, hi):
        def run():
            y = live["x"][:, lo:hi] + 0.5 * _bdot(h_scr[...], wd_ref[:, lo:hi])
            if post:
                assert (lo, hi) == (0, D_MODEL)
                y = _rms(y, fg_ref[...])
            o_ref[:, lo:hi] = y
        return run

    if rider_layer is None:
        for op in [phase_in, hidden(0, ff_chunk), hidden(ff_chunk, D_FF), out(0, D_MODEL)]:
            op()
        return

    q1, q2, q3 = 5 * LANES, ff_chunk, ff_chunk + 5 * LANES
    half = D_MODEL // 2
    phases = [[phase_in, hidden(0, q1)], [hidden(q1, q2), hidden(q2, q3)],
              [hidden(q3, D_FF), out(0, half)], [out(half, D_MODEL)]]
    kbuf, vbuf, sem = next(it), next(it), next(it)
    per_step = len(phases)
    assert per_step == RIDER_PHASES and per_step % 2 == 0
    q_len = qs_ref.shape[0] // per_step
    step = pl.program_id(0)

    def fetch(req, slot):
        return [pltpu.make_async_copy(ck_hbm.at[rider_layer, req], kbuf.at[slot], sem.at[slot, 0]),
                pltpu.make_async_copy(cv_hbm.at[rider_layer, req], vbuf.at[slot], sem.at[slot, 1])]

    @pl.when(step == 0)
    def _():
        for cp in fetch(0, 0):
            cp.start()

    own = _own_head_mask(q_len)
    for r, phase in enumerate(phases):
        req, slot = step * per_step + r, r % 2
        for cp in fetch(req, slot):
            cp.wait()
        if r + 1 < per_step:
            for cp in fetch(req + 1, 1 - slot):
                cp.start()
        else:
            @pl.when(step + 1 < pl.num_programs(0))
            def _():
                for cp in fetch(req + 1, 1 - slot):
                    cp.start()
        rows = slice(r * q_len, (r + 1) * q_len)
        s_buf, s_new = _sample_scores(qs_ref[rows, :], ks_ref[rows, :], kbuf.at[slot], bias_ref, own)
        phase[0]()
        oas_ref[rows, :] = _sample_output(s_buf, s_new, vs_ref[rows, :], vbuf.at[slot], own)
        for op in phase[1:]:
            op()


def _ffn(x, g, wg, wu, wd, idx, pre=None, post_g=None, rider=None):
    n = x.shape[0]
    plain = pre is None and rider is None
    tm = FFN_TILE if plain and n >= 2 * FFN_TILE else ROW_TILE
    n_steps = n // tm
    ff_chunk = D_FF // 2
    row = lambda w: pl.BlockSpec((tm, w), lambda i: (i, 0))
    args, specs = [x], [row(D_MODEL)]
    if pre is not None:
        oa, ob, wo, wo_idx = pre
        args += [oa, ob, wo]
        specs += [row(A_WIDTH), row(RET_V_WIDTH), _stack_spec(wo, wo_idx)]
    args += [g.reshape(1, D_MODEL), wg, wu, wd]
    specs += [_const_spec((1, D_MODEL)), _stack_spec(wg, idx), _stack_spec(wu, idx), _stack_spec(wd, idx)]
    if post_g is not None:
        args.append(post_g.reshape(1, D_MODEL))
        specs.append(_const_spec((1, D_MODEL)))
    out_specs, out_shape = row(D_MODEL), jax.ShapeDtypeStruct((n, D_MODEL), jnp.float32)
    scratch = [pltpu.VMEM((tm, D_FF), jnp.bfloat16)]
    semantics, rider_layer = ("parallel",), None
    if rider is not None:
        qs, ks, vs, cache_kt, cache_vt, rider_layer, bias_s, q_len = rider
        per_step = RIDER_PHASES
        assert qs.shape[0] == n_steps * per_step * q_len
        new = pl.BlockSpec((per_step * q_len, A_WIDTH), lambda i: (i, 0))
        hbm = pl.BlockSpec(memory_space=pl.ANY)
        args += [qs, ks, vs, cache_kt, cache_vt, bias_s]
        specs += [new, new, new, hbm, hbm, _const_spec(bias_s.shape)]
        out_specs = [out_specs, new]
        out_shape = [out_shape, jax.ShapeDtypeStruct((qs.shape[0], A_WIDTH), jnp.float32)]
        slot = pltpu.VMEM((2,) + cache_kt.shape[2:], jnp.float32)
        scratch += [slot, slot, pltpu.SemaphoreType.DMA((2, 2))]
        semantics = ("arbitrary",)
    body = functools.partial(_ffn_body, pre=pre is not None, post=post_g is not None, ff_chunk=ff_chunk,
                             rider_layer=rider_layer)
    return pl.pallas_call(
        body,
        grid=(n_steps,),
        in_specs=specs,
        out_specs=out_specs,
        out_shape=out_shape,
        scratch_shapes=scratch,
        compiler_params=_cparams(semantics),
        name="ffn",
    )(*args)


def _rotate(x, cos, sin_signed):
    w = x.shape[-1]
    lane = lax.broadcasted_iota(jnp.int32, x.shape, 1)
    partner = jnp.where((lane & 1) == 0, pltpu.roll(x, w - 1, 1), pltpu.roll(x, 1, 1))
    return x * cos + partner * sin_signed


def _proj_body(x_ref, g_ref, w_ref, cos_ref, sin_ref,
               qa_ref, ka_ref, va_ref, qr_ref, kr_ref, vb_ref, gb_ref):
    xn = _rms(x_ref[...], g_ref[...]).astype(jnp.bfloat16)
    a, qk, v = A_WIDTH, RET_QK_WIDTH, RET_V_WIDTH
    seg = lambda lo, width: _bdot(xn, w_ref[:, lo:lo + width])
    qa_ref[...] = seg(0, a) * (HD_A ** -0.5)
    ka_ref[...] = seg(a, a)
    va_ref[...] = seg(2 * a, a)
    cos, sin = cos_ref[...], sin_ref[...]
    qr_ref[...] = _rotate(seg(3 * a, qk), cos, sin).astype(qr_ref.dtype)
    kr_ref[...] = (_rotate(seg(3 * a + qk, qk), cos, sin) * (DK_B ** -0.5)).astype(kr_ref.dtype)
    vb_ref[...] = seg(3 * a + 2 * qk, v).astype(vb_ref.dtype)
    gb_ref[...] = seg(3 * a + 2 * qk + v, v)


def _proj(x, g, w_in, layer, cos_tab, sin_tab, ret_dtype):
    n = x.shape[0]
    tm = ROW_TILE
    tab_blocks = cos_tab.shape[0] // tm
    row = lambda w: pl.BlockSpec((tm, w), lambda i: (i, 0))
    tab = pl.BlockSpec((tm, RET_QK_WIDTH), lambda i: (i % tab_blocks, 0))
    f32 = jnp.float32
    outs = [(A_WIDTH, f32), (A_WIDTH, f32), (A_WIDTH, f32), (RET_QK_WIDTH, ret_dtype),
            (RET_QK_WIDTH, ret_dtype), (RET_V_WIDTH, ret_dtype), (RET_V_WIDTH, f32)]
    return pl.pallas_call(
        _proj_body,
        grid=(n // tm,),
        in_specs=[row(D_MODEL), _const_spec((1, D_MODEL)), _stack_spec(w_in, (layer,)), tab, tab],
        out_specs=[row(w) for w, _ in outs],
        out_shape=[jax.ShapeDtypeStruct((n, w), dt) for w, dt in outs],
        compiler_params=_cparams(("parallel",)),
        name="mix_proj",
    )(x, g.reshape(1, D_MODEL), w_in, cos_tab, sin_tab)


def _band_unit(q, kb, vb, bias, low, prev=None):
    nq = q.shape[0]
    zero = jnp.zeros_like(q)
    lhs = jnp.concatenate([jnp.where(low, q, zero), jnp.where(low, zero, q)], axis=0)
    s = _bdot_nt(lhs.astype(jnp.bfloat16), kb.astype(jnp.bfloat16)) + bias
    m_cur = jnp.max(s, axis=1, keepdims=True)
    if prev is not None:
        m_pa, m_pb, l_pa, l_pb, acc_prev = prev
        m_prev = jnp.concatenate([m_pa, m_pb], axis=0)
        m_next = jnp.maximum(m_prev, m_cur)
        alpha = jnp.exp(m_prev - m_next)
    else:
        m_next = jnp.broadcast_to(m_cur, (2 * nq, LANES))
    prob = jnp.exp(s - jnp.concatenate([m_next, m_next], axis=1))
    l_next = jnp.sum(prob, axis=1, keepdims=True)
    pv = _bdot(prob.astype(jnp.bfloat16), vb.astype(jnp.bfloat16))
    acc_next = jnp.where(low, pv[:nq], pv[nq:])
    if prev is not None:
        l_next = alpha * jnp.concatenate([l_pa, l_pb], axis=0) + l_next
        acc_next = jnp.where(low, alpha[:nq], alpha[nq:]) * acc_prev + acc_next
    else:
        l_next = jnp.broadcast_to(l_next, (2 * nq, LANES))
    return m_next, l_next, acc_next


NEAR_PATTERNS = DIL_PATTERNS[:2]


def _attn_near_body(q_ref, kp_ref, kc_ref, vp_ref, vc_ref, bias_ref, o_ref, lse_ref,
                    kk, vv, m_a, m_b, l_a, l_b, acc):
    first = (pl.program_id(2) == 0).astype(jnp.int32)
    span_rows = ATT_SPAN
    kk[0:span_rows, :] = kp_ref[...]
    kk[span_rows:, :] = kc_ref[...]
    vv[0:span_rows, :] = vp_ref[...]
    vv[span_rows:, :] = vc_ref[...]
    nq = BAND_Q
    low = lax.broadcasted_iota(jnp.int32, (nq, LANES), 1) < HD_A
    n_units = span_rows // nq

    for p, (_, dil) in enumerate(NEAR_PATTERNS):
        span = nq * dil
        shift = dil.bit_length() - 1
        init = p == 0

        def load(u, p=p, dil=dil, span=span, shift=shift, init=init):
            blk = u >> shift
            base = blk * span + (u & (dil - 1))
            rows = pl.ds(base, nq, stride=dil)
            band = pl.ds(span_rows + base - span, 2 * nq, stride=dil)
            flag = jnp.where(blk == 0, first, 0)
            vals = [q_ref[rows, :], kk[band, :], vv[band, :], bias_ref[p, flag]]
            prev = None if init else (m_a[rows, :], m_b[rows, :], l_a[rows, :], l_b[rows, :], acc[rows, :])
            return rows, vals, prev

        unroll = n_units if init else ATT_UNROLL

        def group(g, carry, load=load, unroll=unroll):
            loaded = [load(g * unroll + j) for j in range(unroll)]
            results = [_band_unit(*vals, low, prev) for _, vals, prev in loaded]
            for (rows, _, _), (m_next, l_next, acc_next) in zip(loaded, results):
                acc[rows, :] = acc_next
                m_a[rows, :] = m_next[:nq]
                m_b[rows, :] = m_next[nq:]
                l_a[rows, :] = l_next[:nq]
                l_b[rows, :] = l_next[nq:]
            return carry

        lax.fori_loop(0, n_units // unroll, group, 0)

    low_all = lax.broadcasted_iota(jnp.int32, acc.shape, 1) < HD_A
    den = jnp.where(low_all, l_a[...], l_b[...])
    o_ref[...] = acc[...] / den
    lse_ref[...] = jnp.where(low_all, m_a[...], m_b[...]) + jnp.log(den)


FAR_DIL = DIL_PATTERNS[-1][1]
FAR_CLASSES = 4
FAR_STREAMS = 7


def _attn_far_body(q_hbm, k_hbm, v_hbm, near_hbm, lse_hbm, bias_ref, o_hbm,
                   qbuf, kbuf, vbuf, nearbuf, lsebuf, obuf, in_sem, out_sem, *, nblk):
    step = pl.program_id(0)
    n_steps = pl.num_programs(0)
    nq = BAND_Q
    per_span = FAR_DIL // FAR_CLASSES

    def locate(s):
        span = s // per_span
        first = (span % nblk) == 0
        return span, jnp.where(first, span, span - 1), (s % per_span) * FAR_CLASSES, first

    def in_copies(s, slot):
        span, prev, r0, _ = locate(s)
        copies = []
        for c in range(FAR_CLASSES):
            r = r0 + c
            pairs = [(q_hbm.at[span, :, r, :], qbuf.at[slot, c]),
                     (k_hbm.at[prev, :, r, :], kbuf.at[slot, c, pl.ds(0, nq), :]),
                     (k_hbm.at[span, :, r, :], kbuf.at[slot, c, pl.ds(nq, nq), :]),
                     (v_hbm.at[prev, :, r, :], vbuf.at[slot, c, pl.ds(0, nq), :]),
                     (v_hbm.at[span, :, r, :], vbuf.at[slot, c, pl.ds(nq, nq), :]),
                     (near_hbm.at[span, :, r, :], nearbuf.at[slot, c]),
                     (lse_hbm.at[span, :, r, :], lsebuf.at[slot, c])]
            copies += [pltpu.make_async_copy(src, dst, in_sem.at[slot, c, i])
                       for i, (src, dst) in enumerate(pairs)]
        return copies

    def out_copies(s, slot):
        span, _, r0, _ = locate(s)
        return [pltpu.make_async_copy(obuf.at[slot, c], o_hbm.at[span, :, r0 + c, :], out_sem.at[slot, c])
                for c in range(FAR_CLASSES)]

    slot = step % 2

    @pl.when(step == 0)
    def _():
        for cp in in_copies(0, 0):
            cp.start()

    @pl.when(step + 1 < n_steps)
    def _():
        for cp in in_copies(step + 1, 1 - slot):
            cp.start()

    for cp in in_copies(step, slot):
        cp.wait()

    @pl.when(step >= 2)
    def _():
        for cp in out_copies(step - 2, slot):
            cp.wait()

    first = locate(step)[3].astype(jnp.int32)
    low = lax.broadcasted_iota(jnp.int32, (nq, LANES), 1) < HD_A
    for c in range(FAR_CLASSES):
        for pair in range(H_A // 2):
            lanes = slice(pair * LANES, (pair + 1) * LANES)
            m_far, l_far, acc_far = _band_unit(qbuf[slot, c, :, lanes], kbuf[slot, c, :, lanes],
                                               vbuf[slot, c, :, lanes], bias_ref[pair, first], low)
            m_far = jnp.where(low, m_far[:nq], m_far[nq:])
            l_far = jnp.where(low, l_far[:nq], l_far[nq:])
            lse_near = lsebuf[slot, c, :, lanes]
            m_all = jnp.maximum(lse_near, m_far)
            w_near = jnp.exp(lse_near - m_all)
            w_far = jnp.exp(m_far - m_all)
            obuf[slot, c, :, lanes] = ((w_near * nearbuf[slot, c, :, lanes] + w_far * acc_far)
                                       / (w_near + w_far * l_far))

    for cp in out_copies(step, slot):
        cp.start()

    @pl.when(step == n_steps - 1)
    def _():
        for cp in out_copies(step, slot) + out_copies(step - 1, 1 - slot):
            cp.wait()


def _prompt_bias(t5_table):
    n = BAND_Q
    after_start = np.arange(2 * n)[None, None, :] >= n
    per_pattern = []
    for _, dil in DIL_PATTERNS:
        per_step = t5_table[_t5_bucket(np.arange(n + 1) * dil)].T.astype(jnp.float32)
        diag = jnp.concatenate([jnp.full((H_A, n - 1), NEG, jnp.float32), per_step[:, ::-1],
                                jnp.full((H_A, n), NEG, jnp.float32)], axis=1)
        skew = jnp.broadcast_to(diag[:, None, :], (H_A, n, 3 * n)).reshape(H_A, -1)
        skew = skew[:, :n * (3 * n - 1)].reshape(H_A, n, 3 * n - 1)
        full = skew[:, :, n - 1:3 * n - 1]
        start = jnp.where(after_start, full, NEG)
        per_pattern.append(jnp.stack([full, start], axis=0))
    b = jnp.stack(per_pattern, axis=0)
    b = b.reshape(len(DIL_PATTERNS), 2, H_A // 2, 2 * n, 2 * n)
    return jnp.transpose(b, (2, 0, 1, 3, 4))


def _attn_prompt(qa, ka, va, bias, batch, seq):
    assert ATT_SPAN == BAND_Q * FAR_DIL and FAR_DIL % FAR_CLASSES == 0
    nblk = seq // ATT_SPAN
    n = qa.shape[0]
    cur = pl.BlockSpec((ATT_SPAN, LANES), lambda hp, b, j: (b * nblk + j, hp))
    prev = pl.BlockSpec((ATT_SPAN, LANES), lambda hp, b, j: (b * nblk + jnp.maximum(j - 1, 0), hp))
    n_near = len(NEAR_PATTERNS)
    bias_near, bias_far = bias[:, :n_near], bias[:, n_near]
    bias_spec = pl.BlockSpec((None,) + bias_near.shape[1:], lambda hp, b, j: (hp, 0, 0, 0, 0))
    f32 = jnp.float32
    o_near, lse_near = pl.pallas_call(
        _attn_near_body,
        grid=(H_A // 2, batch, nblk),
        in_specs=[cur, prev, cur, prev, cur, bias_spec],
        out_specs=[cur, cur],
        out_shape=[jax.ShapeDtypeStruct((n, A_WIDTH), f32)] * 2,
        scratch_shapes=[pltpu.VMEM((2 * ATT_SPAN, LANES), f32), pltpu.VMEM((2 * ATT_SPAN, LANES), f32)]
        + [pltpu.VMEM((ATT_SPAN, LANES), f32)] * 5,
        compiler_params=_cparams(("parallel", "parallel", "parallel")),
        name="attn_near",
    )(qa, ka, ka, va, va, bias_near)

    spans = batch * nblk
    view = lambda a: a.reshape(spans, BAND_Q, FAR_DIL, A_WIDTH)
    n_steps = spans * FAR_DIL // FAR_CLASSES
    assert n_steps >= 2
    hbm = pl.BlockSpec(memory_space=pl.ANY)
    slots = lambda rows: pltpu.VMEM((2, FAR_CLASSES, rows, A_WIDTH), f32)
    out = pl.pallas_call(
        functools.partial(_attn_far_body, nblk=nblk),
        grid=(n_steps,),
        in_specs=[hbm] * 5 + [_const_spec(bias_far.shape)],
        out_specs=hbm,
        out_shape=jax.ShapeDtypeStruct((spans, BAND_Q, FAR_DIL, A_WIDTH), f32),
        scratch_shapes=[slots(BAND_Q), slots(2 * BAND_Q), slots(2 * BAND_Q), slots(BAND_Q), slots(BAND_Q),
                        slots(BAND_Q), pltpu.SemaphoreType.DMA((2, FAR_CLASSES, FAR_STREAMS)),
                        pltpu.SemaphoreType.DMA((2, FAR_CLASSES))],
        compiler_params=_cparams(("arbitrary",)),
        name="attn_far",
    )(view(qa), view(ka), view(va), view(o_near), view(lse_near), bias_far)
    return out.reshape(n, A_WIDTH)


def _ret_tables(length):
    lg = jnp.log(1.0 - 2.0 ** (-5.0 - jnp.arange(H_B, dtype=jnp.float32)))
    i = jnp.arange(length, dtype=jnp.float32)
    diff = i[:, None] - i[None, :]
    dmask = jnp.where(diff >= 0, jnp.exp(jnp.maximum(diff, 0.0)[None] * lg[:, None, None]), 0.0)
    dq = jnp.exp((i[:, None] + 1.0) * lg[None, :])
    dq = jnp.broadcast_to(dq.T[:, :, None], (H_B, length, DV_B))
    kd = jnp.exp((length - 1.0 - i)[:, None] * lg[None, :])
    kdec = jnp.repeat(kd, DK_B, axis=1).reshape(length, H_B // 2, 2 * DK_B).transpose(1, 0, 2)
    sd = jnp.exp(length * lg)
    sdec = jnp.broadcast_to(jnp.repeat(sd, DK_B).reshape(H_B // 2, 2 * DK_B, 1), (H_B // 2, 2 * DK_B, DV_B))
    return dmask, dq, kdec, sdec


def _norm_gate(o, gate):
    on = o * lax.rsqrt(jnp.mean(o * o, axis=-1, keepdims=True) + RMS_EPS)
    return _silu(gate) * on


def _ret_prompt_body(q_ref, k_ref, v_ref, g_ref, dmask_ref, dq_ref, kdec_ref, sdec_ref,
                     o_ref, s_out_ref, state):
    step = pl.program_id(1)

    @pl.when(step == 0)
    def _():
        state[...] = jnp.zeros(state.shape, jnp.float32)

    c_len = RET_CHUNK
    low = lax.broadcasted_iota(jnp.int32, (c_len, LANES), 1) < DK_B
    top = lax.broadcasted_iota(jnp.int32, (2 * DK_B, DV_B), 0) < DK_B

    def chunk(c, s_pairs):
        rows = pl.ds(pl.multiple_of(c * c_len, c_len), c_len)
        s_next = []
        for pair in range(H_B // 2):
            lanes = slice(pair * LANES, (pair + 1) * LANES)
            q2 = q_ref[rows, lanes]
            k2 = k_ref[rows, lanes]
            s_old = s_pairs[pair]
            kt = (k2.astype(jnp.float32) * kdec_ref[pair]).T.astype(jnp.bfloat16)
            zero = jnp.zeros_like(q2)
            qm = jnp.concatenate([jnp.where(low, q2, zero), jnp.where(low, zero, q2)], axis=0)
            vl = slice(2 * pair * DV_B, 2 * (pair + 1) * DV_B)
            v2 = v_ref[rows, vl]
            vz = jnp.zeros((c_len, DV_B), v2.dtype)
            v_diag = jnp.concatenate([jnp.concatenate([v2[:, :DV_B], vz], axis=1),
                                      jnp.concatenate([vz, v2[:, DV_B:]], axis=1)], axis=0)
            a = (_bdot_nt(qm, k2) * dmask_ref[pair]).astype(jnp.bfloat16)
            intra = _bdot(jnp.concatenate([a[:c_len], a[c_len:]], axis=1), v_diag)
            cross = _bdot(qm, s_old.astype(jnp.bfloat16))
            o = intra + jnp.concatenate([cross[:c_len], cross[c_len:]], axis=1) * dq_ref[pair]
            gate = g_ref[rows, vl]
            o_ref[rows, vl] = jnp.concatenate(
                [_norm_gate(o[:, :DV_B], gate[:, :DV_B]), _norm_gate(o[:, DV_B:], gate[:, DV_B:])],
                axis=1).astype(o_ref.dtype)
            upd = _bdot(kt, v2)
            s_next.append(s_old * sdec_ref[pair] + jnp.where(top, upd[:, :DV_B], upd[:, DV_B:]))
        return s_next

    def group(gi, carry):
        s_pairs = [state[pair] for pair in range(H_B // 2)]
        for j in range(RET_UNROLL):
            s_pairs = chunk(gi * RET_UNROLL + j, s_pairs)
        for pair in range(H_B // 2):
            state[pair] = s_pairs[pair]
        return carry

    lax.fori_loop(0, q_ref.shape[0] // (c_len * RET_UNROLL), group, 0)

    @pl.when(step == pl.num_programs(1) - 1)
    def _():
        for h in range(H_B):
            s_out_ref[h] = state[h // 2, (h % 2) * DK_B:(h % 2 + 1) * DK_B, :]


def _ret_prompt(qr, kr, vb, gb, batch, seq):
    rows = 1024
    nblk = seq // rows
    n = qr.shape[0]
    dmask, dq, kdec, sdec = _ret_tables(RET_CHUNK)
    dmask = dmask.reshape(H_B // 2, 2 * RET_CHUNK, RET_CHUNK)
    dq = dq.reshape(H_B // 2, 2, RET_CHUNK, DV_B).transpose(0, 2, 1, 3).reshape(H_B // 2, RET_CHUNK, 2 * DV_B)
    row = lambda w: pl.BlockSpec((rows, w), lambda b, j: (b * nblk + j, 0))
    return pl.pallas_call(
        _ret_prompt_body,
        grid=(batch, nblk),
        in_specs=[row(RET_QK_WIDTH), row(RET_QK_WIDTH), row(RET_V_WIDTH), row(RET_V_WIDTH),
                  _const_spec(dmask.shape), _const_spec(dq.shape), _const_spec(kdec.shape),
                  _const_spec(sdec.shape)],
        out_specs=[row(RET_V_WIDTH),
                   pl.BlockSpec((None, H_B, DK_B, DV_B), lambda b, j: (b, 0, 0, 0))],
        out_shape=[jax.ShapeDtypeStruct((n, RET_V_WIDTH), jnp.bfloat16),
                   jax.ShapeDtypeStruct((batch, H_B, DK_B, DV_B), jnp.float32)],
        scratch_shapes=[pltpu.VMEM((H_B // 2, 2 * DK_B, DV_B), jnp.float32)],
        compiler_params=_cparams(("parallel", "arbitrary")),
        name="ret_prompt",
    )(qr, kr, vb, gb, dmask, dq, kdec, sdec)


def _sample_bias(t5_table, w_buf, q_len):
    dist = np.arange(w_buf + q_len)
    by_bucket = t5_table[_t5_bucket(dist)].T.astype(jnp.float32)
    merged = None
    for win, dil in DIL_PATTERNS:
        ok = (dist % dil == 0) & (dist // dil <= win // dil)
        b = jnp.where(ok[None], by_bucket, -jnp.inf)
        merged = b if merged is None else jnp.logaddexp(merged, b)
    merged = jnp.where(jnp.isfinite(merged), merged, NEG)
    rev = jnp.concatenate([jnp.full((H_A, q_len - 1), NEG, jnp.float32), merged], axis=1)[:, ::-1]
    per_query = jnp.stack([rev[:, q_len - 1 - i:q_len - 1 - i + w_buf + q_len] for i in range(q_len)], axis=1)
    pad = jnp.full((H_A, q_len, LANES - q_len), NEG, jnp.float32)
    return jnp.concatenate([per_query, pad], axis=2).reshape(H_A * q_len, w_buf + LANES)


def _own_head_mask(q_len):
    rows = H_A * q_len
    row_head = jnp.right_shift(lax.broadcasted_iota(jnp.int32, (rows, A_WIDTH), 0), q_len.bit_length() - 1)
    lane_head = jnp.right_shift(lax.broadcasted_iota(jnp.int32, (rows, A_WIDTH), 1), HD_A.bit_length() - 1)
    return row_head == lane_head


def _sample_scores(q, k_new, kt_ref, bias_ref, own):
    q_len = q.shape[0]
    w_buf = kt_ref.shape[-1]
    bf16 = jnp.bfloat16
    q_rep = jnp.concatenate([q] * H_A, axis=0)
    lhs = jnp.where(own, q_rep, 0.0).astype(bf16)
    padz = jnp.zeros((LANES - q_len, A_WIDTH), jnp.float32)
    k_new = jnp.concatenate([k_new, padz], axis=0).astype(bf16)
    kt = kt_ref[...].reshape(A_WIDTH, w_buf).astype(bf16)
    s_buf = _bdot(lhs, kt) + bias_ref[:, 0:w_buf]
    s_new = _bdot_nt(lhs, k_new) + bias_ref[:, w_buf:]
    return s_buf, s_new


def _sample_output(s_buf, s_new, v_new, vt_ref, own):
    q_len = v_new.shape[0]
    w_buf = vt_ref.shape[-1]
    bf16 = jnp.bfloat16
    padz = jnp.zeros((LANES - q_len, A_WIDTH), jnp.float32)
    v_new = jnp.concatenate([v_new, padz], axis=0).astype(bf16)
    vt = vt_ref[...].reshape(A_WIDTH, w_buf).astype(bf16)
    m = jnp.maximum(jnp.max(s_buf, axis=1, keepdims=True), jnp.max(s_new, axis=1, keepdims=True))
    p_buf = jnp.exp(s_buf - m)
    p_new = jnp.exp(s_new - m)
    den = jnp.sum(p_buf, axis=1, keepdims=True) + jnp.sum(p_new, axis=1, keepdims=True)
    o_all = (_bdot_nt(p_buf.astype(bf16), vt) + _bdot(p_new.astype(bf16), v_new)) / den
    o_all = jnp.where(own, o_all, 0.0)
    out = o_all[0:q_len]
    for h in range(1, H_A):
        out = out + o_all[h * q_len:(h + 1) * q_len]
    return out


RET_GROUP = 16


def _ret_sample_body(q_ref, k_ref, v_ref, g_ref, s_ref, dmask_ref, dq_ref, kdec_ref, sdec_ref,
                     o_ref, s_out_ref, *, q_len):
    rows = q_ref.shape[0]
    low = lax.broadcasted_iota(jnp.int32, (rows, LANES), 1) < DK_B
    top = lax.broadcasted_iota(jnp.int32, (2 * DK_B, DV_B), 0) < DK_B
    col = lax.broadcasted_iota(jnp.int32, (2 * DK_B, rows), 1)
    for pair in range(H_B // 2):
        lanes = slice(pair * LANES, (pair + 1) * LANES)
        q2 = q_ref[:, lanes]
        k2 = k_ref[:, lanes]
        kt = (k2 * kdec_ref[pair]).T
        zero = jnp.zeros_like(q2)
        upd = [[], []]
        for hh in range(2):
            h = 2 * pair + hh
            vl = slice(h * DV_B, (h + 1) * DV_B)
            qm = jnp.where(low, q2, zero) if hh == 0 else jnp.where(low, zero, q2)
            vh = v_ref[:, vl]
            a = _bdot_nt(qm, k2) * dmask_ref[h]
            cross = jnp.concatenate(
                [_bdot(qm[r * q_len:(r + 1) * q_len], s_ref[r, pair]) for r in range(RET_GROUP)], axis=0)
            o = _bdot(a, vh) + cross * dq_ref[h]
            o_ref[:, vl] = _norm_gate(o, g_ref[:, vl]).astype(o_ref.dtype)
            for r in range(RET_GROUP):
                upd[hh].append(_bdot(jnp.where((col >= r * q_len) & (col < (r + 1) * q_len), kt, 0.0), vh))
        for r in range(RET_GROUP):
            s_out_ref[r, pair] = s_ref[r, pair] * sdec_ref[pair] + jnp.where(top, upd[0][r], upd[1][r])


def _ret_sample(qr, kr, vb, gb, state, layer, q_len):
    n = qr.shape[0]
    batch = n // q_len
    rows = RET_GROUP * q_len
    dmask1, dq1, kdec1, sdec = _ret_tables(q_len)
    eye = jnp.eye(RET_GROUP, dtype=jnp.float32)
    dmask = jnp.einsum("rs,hij->hrisj", eye, dmask1).reshape(H_B, rows, rows)
    dq = jnp.tile(dq1, (1, RET_GROUP, 1))
    kdec = jnp.tile(kdec1, (1, RET_GROUP, 1))
    row = lambda w: pl.BlockSpec((rows, w), lambda i: (i, 0))
    st_in = pl.BlockSpec((None, RET_GROUP, H_B // 2, 2 * DK_B, DV_B), lambda i: (layer, i, 0, 0, 0))
    st_out = pl.BlockSpec((RET_GROUP, H_B // 2, 2 * DK_B, DV_B), lambda i: (i, 0, 0, 0))
    return pl.pallas_call(
        functools.partial(_ret_sample_body, q_len=q_len),
        grid=(batch // RET_GROUP,),
        in_specs=[row(RET_QK_WIDTH), row(RET_QK_WIDTH), row(RET_V_WIDTH), row(RET_V_WIDTH), st_in,
                  _const_spec(dmask.shape), _const_spec(dq.shape), _const_spec(kdec.shape),
                  _const_spec(sdec.shape)],
        out_specs=[row(RET_V_WIDTH), st_out],
        out_shape=[jax.ShapeDtypeStruct((n, RET_V_WIDTH), jnp.bfloat16),
                   jax.ShapeDtypeStruct((batch, H_B // 2, 2 * DK_B, DV_B), jnp.float32)],
        compiler_params=_cparams(("parallel",)),
        name="ret_sample",
    )(qr, kr, vb, gb, state, dmask, dq, kdec, sdec)


CONV_PAD = 32
CONV_BLK = 64
CONV_LANES = 128
CONV_SPLIT = 2


def _glu(x, g, w1_ref, b1_ref):
    xn = _rms(x, g).astype(jnp.bfloat16)
    a = _bdot(xn, w1_ref[:, 0:D_MODEL]) + b1_ref[:, 0:D_MODEL]
    gate = _bdot(xn, w1_ref[:, D_MODEL:]) + b1_ref[:, D_MODEL:]
    return a / (1.0 + jnp.exp(-gate))


def _conv_tail(x, c, lng_ref, lnb_ref, w2_ref, b2_ref):
    mu = jnp.mean(c, axis=-1, keepdims=True)
    d = c - mu
    var = jnp.mean(d * d, axis=-1, keepdims=True)
    nrm = d * lax.rsqrt(var + LN_EPS) * lng_ref[...] + lnb_ref[...]
    return x + _bdot(_silu(nrm).astype(jnp.bfloat16), w2_ref[...]) + b2_ref[...]


def _conv_prompt_body(x_ref, g_ref, w1_ref, b1_ref, wdw_ref, bdw_ref, lng_ref, lnb_ref, w2_ref, b2_ref,
                      shift_ref, o_ref, buf_ref, ext, cbuf):
    tm = x_ref.shape[0]
    hist = CONV_W - 1

    @pl.when(pl.program_id(1) == 0)
    def _():
        ext[0:CONV_PAD, :] = jnp.zeros((CONV_PAD, D_MODEL), jnp.float32)

    blk, lane_blk, off = CONV_BLK, CONV_LANES, CONV_PAD - hist

    def depthwise(lo, hi):
        for r0 in range(lo, hi, blk):
            for lb in range(D_MODEL // lane_blk):
                lanes = slice(lb * lane_blk, (lb + 1) * lane_blk)
                win = ext[r0:r0 + blk + CONV_PAD, lanes]
                acc = jnp.broadcast_to(bdw_ref[:, lanes], (blk, lane_blk))
                shifted = []
                for r in range(SUBLANES):
                    taps = [k for k in range(CONV_W) if (k + off) % SUBLANES == r]
                    rows = blk if r == 0 else blk + SUBLANES
                    y = None
                    for k in taps:
                        a0 = k + off - r
                        tiles = win[a0:a0 + rows].reshape(rows // SUBLANES, SUBLANES, lane_blk)
                        term = (tiles * wdw_ref[k, :, lanes]).reshape(rows, lane_blk)
                        y = term if y is None else y + term
                    if r == 0:
                        acc = acc + y
                    else:
                        shifted.append(y)
                shifted.append(jnp.zeros((SUBLANES, lane_blk), jnp.float32))
                stacked = jnp.concatenate(shifted, axis=0).astype(jnp.bfloat16)
                cbuf[r0:r0 + blk, lanes] = acc + _bdot(shift_ref[...], stacked)

    part = tm // CONV_SPLIT
    g = g_ref[...]
    for h in range(CONV_SPLIT):
        rows = slice(h * part, (h + 1) * part)
        ext[CONV_PAD + h * part:CONV_PAD + (h + 1) * part, :] = _glu(x_ref[rows, :], g, w1_ref, b1_ref)
    for h in range(CONV_SPLIT):
        rows = slice(h * part, (h + 1) * part)
        depthwise(h * part, (h + 1) * part)
        o_ref[rows, :] = _conv_tail(x_ref[rows, :], cbuf[rows, :], lng_ref, lnb_ref, w2_ref, b2_ref)
    buf_ref[...] = ext[tm + CONV_PAD - hist:, :]
    ext[0:CONV_PAD, :] = ext[tm:, :]


def _conv_prompt(x, g, w1, b1, wdw, bdw, lng, lnb, w2, b2, layer, batch, seq):
    tm = ROW_TILE
    nblk = seq // tm
    n = x.shape[0]
    row = pl.BlockSpec((tm, D_MODEL), lambda b, j: (b * nblk + j, 0))
    vec = lambda a: a.reshape(1, -1)
    f32 = jnp.float32
    wdw = jnp.broadcast_to(wdw[:, None, :], (CONV_W, SUBLANES, D_MODEL))
    part = CONV_BLK + SUBLANES
    shift = np.zeros((CONV_BLK, SUBLANES * part), np.float32)
    for r in range(1, SUBLANES):
        shift[np.arange(CONV_BLK), (r - 1) * part + np.arange(CONV_BLK) + r] = 1.0
    shift = jnp.asarray(shift[:, :(SUBLANES - 1) * part + SUBLANES], jnp.bfloat16)
    return pl.pallas_call(
        _conv_prompt_body,
        grid=(batch, nblk),
        in_specs=[row, _const_spec((1, D_MODEL)), _stack_spec(w1, (layer,)), _const_spec((1, 2 * D_MODEL)),
                  _const_spec(wdw.shape), _const_spec((1, D_MODEL)), _const_spec((1, D_MODEL)),
                  _const_spec((1, D_MODEL)), _stack_spec(w2, (layer,)), _const_spec((1, D_MODEL)),
                  _const_spec(shift.shape)],
        out_specs=[row, pl.BlockSpec((None, CONV_W - 1, D_MODEL), lambda b, j: (b, 0, 0))],
        out_shape=[jax.ShapeDtypeStruct((n, D_MODEL), f32),
                   jax.ShapeDtypeStruct((batch, CONV_W - 1, D_MODEL), f32)],
        scratch_shapes=[pltpu.VMEM((tm + CONV_PAD, D_MODEL), f32), pltpu.VMEM((tm, D_MODEL), f32)],
        compiler_params=_cparams(("parallel", "arbitrary")),
        name="conv_prompt",
    )(x, vec(g), w1, vec(b1), wdw, vec(bdw), vec(lng), vec(lnb), w2, vec(b2), shift)


CONV_GROUP = 32


def _conv_sample_body(x_ref, g_ref, w1_ref, b1_ref, wdw_ref, bdw_ref, lng_ref, lnb_ref, w2_ref, b2_ref,
                      st_ref, o_ref, buf_ref, ext, cbuf, *, q_len):
    hist = CONV_W - 1
    x = x_ref[...]
    glu = _glu(x, g_ref[...], w1_ref, b1_ref)
    for r in range(CONV_GROUP):
        ext[r, 0:hist, :] = st_ref[r]
        ext[r, hist:hist + q_len, :] = glu[r * q_len:(r + 1) * q_len]
    for r in range(CONV_GROUP):
        acc = jnp.broadcast_to(bdw_ref[...], (q_len, D_MODEL))
        for k in range(CONV_W):
            acc = acc + ext[r, k:k + q_len, :] * wdw_ref[k:k + 1, :]
        cbuf[r * q_len:(r + 1) * q_len, :] = acc
        buf_ref[r] = ext[r, q_len:q_len + hist, :]
    o_ref[...] = _conv_tail(x, cbuf[...], lng_ref, lnb_ref, w2_ref, b2_ref)


def _conv_sample(x, g, w1, b1, wdw, bdw, lng, lnb, w2, b2, state, layer, q_len):
    n = x.shape[0]
    batch = n // q_len
    rows = CONV_GROUP * q_len
    hist = CONV_W - 1
    row = pl.BlockSpec((rows, D_MODEL), lambda i: (i, 0))
    vec = lambda a: a.reshape(1, -1)
    f32 = jnp.float32
    return pl.pallas_call(
        functools.partial(_conv_sample_body, q_len=q_len),
        grid=(batch // CONV_GROUP,),
        in_specs=[row, _const_spec((1, D_MODEL)), _stack_spec(w1, (layer,)), _const_spec((1, 2 * D_MODEL)),
                  _const_spec(wdw.shape), _const_spec((1, D_MODEL)), _const_spec((1, D_MODEL)),
                  _const_spec((1, D_MODEL)), _stack_spec(w2, (layer,)), _const_spec((1, D_MODEL)),
                  pl.BlockSpec((None, CONV_GROUP, hist, D_MODEL), lambda i: (layer, i, 0, 0))],
        out_specs=[row, pl.BlockSpec((CONV_GROUP, hist, D_MODEL), lambda i: (i, 0, 0))],
        out_shape=[jax.ShapeDtypeStruct((n, D_MODEL), f32),
                   jax.ShapeDtypeStruct((batch, hist, D_MODEL), f32)],
        scratch_shapes=[pltpu.VMEM((CONV_GROUP, hist + q_len + 2, D_MODEL), f32),
                        pltpu.VMEM((rows, D_MODEL), f32)],
        compiler_params=_cparams(("parallel",)),
        name="conv_sample",
    )(x, vec(g), w1, vec(b1), wdw, vec(bdw), vec(lng), vec(lnb), w2, vec(b2), state)


def _t5_bucket(dist):
    dist = np.asarray(dist).astype(np.int32)
    max_exact = T5_BUCKETS // 2
    log_ratio = np.log(np.maximum(dist, 1) / max_exact) / math.log(T5_MAX_DIST / max_exact)
    large = np.minimum(max_exact + (log_ratio * (T5_BUCKETS - max_exact)).astype(np.int32), T5_BUCKETS - 1)
    return np.where(dist < max_exact, dist, large).astype(np.int32)


def _rot_tables(pos):
    half = DK_B // 2
    ang = 1.0 / (10000.0 ** jnp.linspace(0.0, 1.0, half, dtype=jnp.float32))
    th = pos.astype(jnp.float32)[:, None] * ang[None, :]
    cos = jnp.repeat(jnp.cos(th), 2, axis=1)
    sin = jnp.repeat(jnp.sin(th), 2, axis=1) * jnp.tile(jnp.array([-1.0, 1.0], jnp.float32), half)[None, :]
    return jnp.tile(cos, (1, H_B)), jnp.tile(sin, (1, H_B))


def kernel(x_prompt, x_sample, cache_win_k, cache_win_v, state_ret, state_conv, t5_table, rms_g, final_g,
           w_ffn_gate, w_ffn_up, w_ffn_down, w_mix_in, w_mix_out, w_pw1, b_pw1, w_dw, b_dw, ln_g, ln_b,
           w_pw2, b_pw2):
    batch, seq, _ = x_prompt.shape
    dec_batch, dec_seq, _ = x_sample.shape
    depth = rms_g.shape[0]
    n_even = cache_win_k.shape[0]
    w_buf = cache_win_k.shape[2]
    keep = min(DIL_PATTERNS[-1][0], seq)
    bf16 = jnp.bfloat16

    xp = x_prompt.reshape(batch * seq, D_MODEL)
    xs = x_sample.reshape(dec_batch * dec_seq, D_MODEL)

    wg, wu, wd = w_ffn_gate.astype(bf16), w_ffn_up.astype(bf16), w_ffn_down.astype(bf16)
    w_in, w_out = w_mix_in.astype(bf16), w_mix_out.astype(bf16)
    w1, w2 = w_pw1.astype(bf16), w_pw2.astype(bf16)

    cos_p, sin_p = _rot_tables(jnp.arange(seq))
    cos_s, sin_s = _rot_tables(PAST_LEN + jnp.arange(ROW_TILE) % dec_seq)
    bias_p = _prompt_bias(t5_table)
    bias_s = _sample_bias(t5_table, w_buf, dec_seq)
    assert dec_seq & (dec_seq - 1) == 0 and dec_seq <= LANES
    ck = jnp.transpose(cache_win_k, (0, 1, 3, 4, 2))
    cv = jnp.transpose(cache_win_v, (0, 1, 3, 4, 2))
    st_ret = state_ret.reshape(n_even, dec_batch, H_B // 2, 2 * DK_B, DV_B)

    kp_l, vp_l, sp_l, cp_l, ks_l, vs_l, ss_l, cs_l = [], [], [], [], [], [], [], []
    for layer in range(depth):
        g = rms_g[layer]
        xp = _ffn(xp, g[0], wg, wu, wd, (layer, 0))
        xs = _ffn(xs, g[0], wg, wu, wd, (layer, 0))
        last = layer == depth - 1
        post = final_g if last else None
        ffn_b = (g[2], wg, wu, wd, (layer, 1))
        if layer % 2 == 0:
            e = layer // 2
            qa, ka, va, qr, kr, vb, gb = _proj(xp, g[1], w_in, e, cos_p, sin_p, bf16)
            oa = _attn_prompt(qa, ka, va, bias_p, batch, seq)
            ob, s_end = _ret_prompt(qr, kr, vb, gb, batch, seq)
            kp_l.append(ka.reshape(batch, seq, A_WIDTH)[:, -keep:].reshape(batch, keep, H_A, HD_A))
            vp_l.append(va.reshape(batch, seq, A_WIDTH)[:, -keep:].reshape(batch, keep, H_A, HD_A))
            sp_l.append(s_end)

            qa, ka, va, qr, kr, vb, gb = _proj(xs, g[1], w_in, e, cos_s, sin_s, jnp.float32)
            xp, oa = _ffn(xp, *ffn_b, pre=(oa, ob, w_out, (e,)), post_g=post,
                          rider=(qa, ka, va, ck, cv, e, bias_s, dec_seq))
            ob, s_new = _ret_sample(qr, kr, vb, gb, st_ret, e, dec_seq)
            xs = _ffn(xs, *ffn_b, pre=(oa, ob, w_out, (e,)), post_g=post)
            ks_l.append(ka.reshape(dec_batch, dec_seq, H_A, HD_A))
            vs_l.append(va.reshape(dec_batch, dec_seq, H_A, HD_A))
            ss_l.append(s_new.reshape(dec_batch, H_B, DK_B, DV_B))
        else:
            o = layer // 2
            conv = (g[1], w1, b_pw1[o], w_dw[o], b_dw[o], ln_g[o], ln_b[o], w2, b_pw2[o])
            xp, cp = _conv_prompt(xp, *conv, o, batch, seq)
            xs, cs = _conv_sample(xs, *conv, state_conv, o, dec_seq)
            cp_l.append(cp)
            cs_l.append(cs)
            xp = _ffn(xp, *ffn_b, post_g=post)
            xs = _ffn(xs, *ffn_b, post_g=post)

    stk = jnp.stack
    return (xp.reshape(batch, seq, D_MODEL), xs.reshape(dec_batch, dec_seq, D_MODEL),
            stk(kp_l), stk(vp_l), stk(sp_l), stk(cp_l), stk(ks_l), stk(vs_l), stk(ss_l), stk(cs_l))
```
